```python
import jax
import jax.numpy as jnp
from jax import lax
import numpy as np

D_MODEL = 1024
BATCH = 8
SEQ = 2048
DEPTH = 1
DEC_BATCH = 128
DEC_SEQ = 1
PAST_LEN = 16384
PAGE_SIZE = 128

N_META = 16
RWKV_WIDTH = D_MODEL // 2
RWKV_HEAD_DIM = 64
RWKV_HEADS = RWKV_WIDTH // RWKV_HEAD_DIM
DECAY_LORA = 64
AAA_LORA = 64
GATE_LORA = 128
SHIFT_WIDTH = 3 * RWKV_WIDTH + DECAY_LORA + AAA_LORA + GATE_LORA
LNX_EPS = 64e-5
RET_WIDTH = D_MODEL // 2
RET_HEADS = 4
RET_KEY_DIM = RET_WIDTH // RET_HEADS
RET_VAL_DIM = RET_WIDTH // RET_HEADS
RET_CHUNK = 128
ROPE_BASE = 10000.0
IN_WIDTH = SHIFT_WIDTH + 4 * RET_WIDTH + 2 * D_MODEL
N_GROUPS = 4
EXPERTS_PER_GROUP = 4
N_EXPERTS = N_GROUPS * EXPERTS_PER_GROUP
TOP_K = 2
EXPERT_FF = 256
RMS_EPS = 1e-6

kernel_name = "rwkv7_retention_hier_moe_step"

F32 = jnp.float32


def rmsnorm(x, g):
    xf = x.astype(F32)
    y = xf * lax.rsqrt(jnp.mean(xf * xf, axis=-1, keepdims=True) + RMS_EPS)
    return (y * g.astype(F32)).astype(x.dtype)


def rope(x, pos):
    half = x.shape[-1] // 2
    inv = ROPE_BASE ** (-jnp.arange(half, dtype=F32) / half)
    ang = pos.astype(F32)[:, None] * inv[None, :]
    c = jnp.cos(ang)[None, :, None, :]
    s = jnp.sin(ang)[None, :, None, :]
    xf = x.astype(F32)
    x1, x2 = xf[..., 0::2], xf[..., 1::2]
    return jnp.stack([x1 * c - x2 * s, x1 * s + x2 * c], axis=-1).reshape(x.shape)


def rwkv_recurrence(r, w, k, v, kk, a, s0):
    def step(S, inp):
        r_t, w_t, k_t, v_t, kk_t, a_t = inp
        sa = jnp.einsum('bhvk,bhk->bhv', S, -kk_t)
        S = (S * w_t[:, :, None, :]
             + sa[..., None] * (kk_t * a_t)[:, :, None, :]
             + v_t[..., None] * k_t[:, :, None, :])
        y = jnp.einsum('bhvk,bhk->bhv', S, r_t)
        return S, y
    xs = tuple(jnp.moveaxis(t, 1, 0) for t in (r, w, k, v, kk, a))
    s_end, ys = lax.scan(step, s0, xs)
    return jnp.moveaxis(ys, 0, 1), s_end


def retention_chunk(q, k, v, s0, log_gamma):
    C = q.shape[1]
    idx = jnp.arange(C, dtype=F32)
    diff = idx[:, None] - idx[None, :]
    lg = log_gamma[:, None, None]
    decay = jnp.where(diff[None] >= 0, jnp.exp(jnp.maximum(diff, 0.0)[None] * lg), 0.0)
    scores = jnp.einsum('bihd,bjhd->bhij', q, k) * decay[None]
    intra = jnp.einsum('bhij,bjhv->bihv', scores, v)
    q_dec = jnp.exp((idx[None, :] + 1.0) * log_gamma[:, None])
    cross = jnp.einsum('bihd,bhdv->bihv', q, s0) * q_dec.T[None, :, :, None]
    k_dec = jnp.exp((C - 1.0 - idx[None, :]) * log_gamma[:, None])
    s_new = (jnp.exp(C * log_gamma)[None, :, None, None] * s0
             + jnp.einsum('bjhd,bjhv,hj->bhdv', k, v, k_dec))
    return intra + cross, s_new


def retention_blocks(q, k, v, s0, log_gamma, n_lead):
    if n_lead == 0:
        return retention_chunk(q, k, v, s0, log_gamma)
    B = q.shape[0]
    o_lead, S = retention_chunk(q[:, :n_lead], k[:, :n_lead], v[:, :n_lead], s0, log_gamma)
    t_rest = q.shape[1] - n_lead
    nc = t_rest // RET_CHUNK

    def to_chunks(t):
        return jnp.moveaxis(t[:, n_lead:].reshape(B, nc, RET_CHUNK, t.shape[2], t.shape[3]), 1, 0)

    def step(S, c):
        o, S = retention_chunk(c[0], c[1], c[2], S, log_gamma)
        return S, o

    S, o_rest = lax.scan(step, S, (to_chunks(q), to_chunks(k), to_chunks(v)))
    o_rest = jnp.moveaxis(o_rest, 0, 1).reshape(B, t_rest, q.shape[2], v.shape[3])
    return jnp.concatenate([o_lead, o_rest], axis=1), S


def mixer(u, pos, shift0, wkv0, ret0, n_lead, p):
    B, T, _ = u.shape
    proj = u @ p['w_in']
    s_cols = proj[..., :SHIFT_WIDTH]
    ret_cols = proj[..., SHIFT_WIDTH:SHIFT_WIDTH + 4 * RET_WIDTH]
    gate_cols = proj[..., SHIFT_WIDTH + 4 * RET_WIDTH:]

    prev = jnp.concatenate([shift0[:, None, :].astype(s_cols.dtype), s_cols[:, :-1]], axis=1)
    xs = s_cols + p['shift_mu'] * (prev - s_cols)
    new_shift = s_cols[:, -1]
    W = RWKV_WIDTH
    r, k, v, wd, ad, gd = jnp.split(
        xs, [W, 2 * W, 3 * W, 3 * W + DECAY_LORA, 3 * W + DECAY_LORA + AAA_LORA], axis=-1)
    w_raw = -jax.nn.softplus(-(p['decay_w0'] + jnp.tanh(wd) @ p['decay_up'])) - 0.5
    decay = jnp.exp(-jnp.exp(w_raw.astype(F32)))
    a = jax.nn.sigmoid(p['aaa_a0'] + ad @ p['aaa_up'])
    g = jax.nn.sigmoid(gd) @ p['gate_up']

    def heads(t):
        return t.astype(F32).reshape(B, T, RWKV_HEADS, RWKV_HEAD_DIM)

    kk = heads(k * p['k_k'])
    kk = kk / jnp.maximum(jnp.sqrt(jnp.sum(kk * kk, axis=-1, keepdims=True)), 1e-12)
    k_mod = heads(k * (1.0 + (a - 1.0) * p['k_a']))
    r_h, v_h, a_h, w_h = heads(r), heads(v), heads(a), heads(decay)
    y, wkv_new = rwkv_recurrence(r_h, w_h, k_mod, v_h, kk, a_h, wkv0.astype(F32))
    mu = jnp.mean(y, axis=-1, keepdims=True)
    var = jnp.mean(jnp.square(y - mu), axis=-1, keepdims=True)
    yn = ((y - mu) * lax.rsqrt(var + LNX_EPS)).reshape(B, T, W)
    yn = yn * p['lnx_w'].astype(F32) + p['lnx_b'].astype(F32)
    bonus = jnp.sum(r_h * k_mod * p['r_k'].astype(F32), axis=-1, keepdims=True) * v_h
    ya = (yn + bonus.reshape(B, T, W)) * g.astype(F32)
    branch_a = ya.astype(u.dtype) @ p['w_branch_a']

    q_r, k_r, v_r, g_r = jnp.split(ret_cols, 4, axis=-1)
    q_r = rope(q_r.reshape(B, T, RET_HEADS, RET_KEY_DIM), pos)
    k_r = rope(k_r.reshape(B, T, RET_HEADS, RET_KEY_DIM), pos) * (RET_KEY_DIM ** -0.5)
    v_r = v_r.astype(F32).reshape(B, T, RET_HEADS, RET_VAL_DIM)
    log_gamma = jnp.log(1.0 - 2.0 ** (-5.0 - jnp.arange(RET_HEADS, dtype=F32)))
    o, ret_new = retention_blocks(q_r, k_r, v_r, ret0.astype(F32), log_gamma, n_lead)
    o = o * lax.rsqrt(jnp.mean(o * o, axis=-1, keepdims=True) + RMS_EPS)
    o = jax.nn.silu(g_r.astype(F32)) * o.reshape(B, T, RET_WIDTH)
    branch_b = o.astype(u.dtype) @ p['w_branch_b']

    g_a, g_b = jnp.split(gate_cols, 2, axis=-1)
    merged = jax.nn.sigmoid(g_a) * branch_a + jax.nn.sigmoid(g_b) * branch_b
    return merged @ p['w_out'], new_shift, wkv_new, ret_new


def hier_moe(u, p):
    B, T, _ = u.shape
    lg = (u @ p['router_group_w'] + p['router_group_b']).astype(F32)
    pg = jax.nn.softmax(lg, axis=-1)
    grp = jnp.argmax(lg, axis=-1)
    p_grp = jnp.take_along_axis(pg, grp[..., None], axis=-1)
    le = (u @ p['router_expert_w'] + p['router_expert_b']).astype(F32)
    le = le.reshape(B, T, N_GROUPS, EXPERTS_PER_GROUP)
    le_sel = jnp.take_along_axis(le, grp[..., None, None], axis=-2)[..., 0, :]
    top_v, top_i = lax.top_k(le_sel, TOP_K)
    w_top = jax.nn.softmax(top_v, axis=-1)
    within = jnp.sum(jax.nn.one_hot(top_i, EXPERTS_PER_GROUP, dtype=F32) * w_top[..., None], axis=-2)
    combine = (jax.nn.one_hot(grp, N_GROUPS, dtype=F32)[..., :, None]
               * within[..., None, :] * p_grp[..., None]).reshape(B, T, N_EXPERTS)
    h = (jax.nn.silu(jnp.einsum('btd,edf->btef', u, p['expert_w_gate']))
         * jnp.einsum('btd,edf->btef', u, p['expert_w_up']))
    h = h * combine[..., None].astype(h.dtype)
    return jnp.einsum('btef,efd->btd', h, p['expert_w_down'])


def layer(x, pos, shift0, wkv0, ret0, n_lead, p):
    m, s_new, w_new, r_new = mixer(rmsnorm(x, p['norm1_g']), pos, shift0, wkv0, ret0, n_lead, p)
    x = x + m
    x = x + hier_moe(rmsnorm(x, p['norm2_g']), p)
    return x, s_new, w_new, r_new


def setup_inputs(seed: int = 0) -> dict:
    key = jax.random.key(seed)
    ks = jax.random.split(key, 40)
    n = lambda i, shape, s: jax.random.normal(ks[i], shape, F32) * s
    L = DEPTH
    return {
        'x_prompt': n(0, (BATCH, SEQ, D_MODEL), 1.0),
        'x_sample': n(1, (DEC_BATCH, DEC_SEQ, D_MODEL), 1.0),
        'state_wkv': n(2, (L, DEC_BATCH, RWKV_HEADS, RWKV_HEAD_DIM, RWKV_HEAD_DIM), 1.0),
        'state_shift': n(3, (L, DEC_BATCH, SHIFT_WIDTH), 1.0),
        'state_ret': n(4, (L, DEC_BATCH, RET_HEADS, RET_KEY_DIM, RET_VAL_DIM), 0.5),
        'meta_tokens': n(5, (N_META, D_MODEL), 1.0),
        'norm1_g': 1.0 + n(6, (L, D_MODEL), 0.05),
        'w_in': n(7, (L, D_MODEL, IN_WIDTH), D_MODEL ** -0.5),
        'shift_mu': jax.random.uniform(ks[8], (L, SHIFT_WIDTH), F32),
        'decay_w0': jax.random.uniform(ks[9], (L, RWKV_WIDTH), F32, -6.0, 1.0),
        'decay_up': n(10, (L, DECAY_LORA, RWKV_WIDTH), 0.1),
        'aaa_a0': n(11, (L, RWKV_WIDTH), 0.1),
        'aaa_up': n(12, (L, AAA_LORA, RWKV_WIDTH), 0.5 * AAA_LORA ** -0.5),
        'gate_up': n(13, (L, GATE_LORA, RWKV_WIDTH), GATE_LORA ** -0.5),
        'k_k': 0.85 + n(14, (L, RWKV_WIDTH), 0.05),
        'k_a': 1.0 + n(15, (L, RWKV_WIDTH), 0.05),
        'r_k': n(16, (L, RWKV_HEADS, RWKV_HEAD_DIM), 0.1),
        'lnx_w': 1.0 + n(17, (L, RWKV_WIDTH), 0.05),
        'lnx_b': n(18, (L, RWKV_WIDTH), 0.01),
        'w_branch_a': n(19, (L, RWKV_WIDTH, D_MODEL), RWKV_WIDTH ** -0.5),
        'w_branch_b': n(20, (L, RET_WIDTH, D_MODEL), RET_WIDTH ** -0.5),
        'w_out': n(21, (L, D_MODEL, D_MODEL), D_MODEL ** -0.5),
        'norm2_g': 1.0 + n(22, (L, D_MODEL), 0.05),
        'router_group_w': n(23, (L, D_MODEL, N_GROUPS), D_MODEL ** -0.5),
        'router_group_b': n(24, (L, N_GROUPS), 0.01),
        'router_expert_w': n(25, (L, D_MODEL, N_EXPERTS), D_MODEL ** -0.5),
        'router_expert_b': n(26, (L, N_EXPERTS), 0.01),
        'expert_w_gate': n(27, (L, N_EXPERTS, D_MODEL, EXPERT_FF), D_MODEL ** -0.5),
        'expert_w_up': n(28, (L, N_EXPERTS, D_MODEL, EXPERT_FF), D_MODEL ** -0.5),
        'expert_w_down': n(29, (L, N_EXPERTS, EXPERT_FF, D_MODEL), EXPERT_FF ** -0.5),
        'final_norm_g': 1.0 + n(30, (D_MODEL,), 0.05),
    }


def reference(x_prompt, x_sample, state_wkv, state_shift, state_ret, meta_tokens,
              norm1_g, w_in, shift_mu, decay_w0, decay_up, aaa_a0, aaa_up, gate_up,
              k_k, k_a, r_k, lnx_w, lnx_b, w_branch_a, w_branch_b, w_out, norm2_g,
              router_group_w, router_group_b, router_expert_w, router_expert_b,
              expert_w_gate, expert_w_up, expert_w_down, final_norm_g):
    Bp, Tp, _ = x_prompt.shape
    Bs, Ts, _ = x_sample.shape
    meta = jnp.broadcast_to(meta_tokens.astype(x_prompt.dtype)[None], (Bp, N_META, D_MODEL))
    xp = jnp.concatenate([meta, x_prompt], axis=1)
    xs = x_sample
    pos_p = jnp.arange(N_META + Tp, dtype=jnp.int32)
    pos_s = PAST_LEN + jnp.arange(Ts, dtype=jnp.int32)
    zero_shift = jnp.zeros((Bp, SHIFT_WIDTH), x_prompt.dtype)
    zero_wkv = jnp.zeros((Bp, RWKV_HEADS, RWKV_HEAD_DIM, RWKV_HEAD_DIM), F32)
    zero_ret = jnp.zeros((Bp, RET_HEADS, RET_KEY_DIM, RET_VAL_DIM), F32)
    wkv_p, shift_p, ret_p, wkv_s, shift_s, ret_s = [], [], [], [], [], []
    for l in range(DEPTH):
        p = {
            'norm1_g': norm1_g[l], 'w_in': w_in[l], 'shift_mu': shift_mu[l],
            'decay_w0': decay_w0[l], 'decay_up': decay_up[l], 'aaa_a0': aaa_a0[l],
            'aaa_up': aaa_up[l], 'gate_up': gate_up[l], 'k_k': k_k[l], 'k_a': k_a[l],
            'r_k': r_k[l], 'lnx_w': lnx_w[l], 'lnx_b': lnx_b[l],
            'w_branch_a': w_branch_a[l], 'w_branch_b': w_branch_b[l], 'w_out': w_out[l],
            'norm2_g': norm2_g[l], 'router_group_w': router_group_w[l],
            'router_group_b': router_group_b[l], 'router_expert_w': router_expert_w[l],
            'router_expert_b': router_expert_b[l], 'expert_w_gate': expert_w_gate[l],
            'expert_w_up': expert_w_up[l], 'expert_w_down': expert_w_down[l],
        }
        xp, sp, wp, rp = layer(xp, pos_p, zero_shift, zero_wkv, zero_ret, N_META, p)
        xs, ss, ws, rs = layer(xs, pos_s, state_shift[l], state_wkv[l], state_ret[l], 0, p)
        wkv_p.append(wp); shift_p.append(sp); ret_p.append(rp)
        wkv_s.append(ws); shift_s.append(ss); ret_s.append(rs)
    y_prompt = rmsnorm(xp[:, N_META:], final_norm_g)
    y_sample = rmsnorm(xs, final_norm_g)
    return (y_prompt, y_sample,
            jnp.stack(wkv_p, 0), jnp.stack(shift_p, 0), jnp.stack(ret_p, 0),
            jnp.stack(wkv_s, 0), jnp.stack(shift_s, 0), jnp.stack(ret_s, 0))
```

```python
import functools

import jax
import jax.numpy as jnp
from jax import lax
from jax.experimental import pallas as pl
from jax.experimental.pallas import tpu as pltpu

F32 = jnp.float32
BF16 = jnp.bfloat16

D_MODEL = 1024
N_META = 16
RWKV_WIDTH = 512
RWKV_HEADS = 8
RWKV_HEAD_DIM = 64
LORA_PAD = 128
GATE_LORA = 128
SHIFT_WIDTH = 3 * RWKV_WIDTH + 64 + 64 + GATE_LORA
LNX_EPS = 64e-5
RET_WIDTH = 512
RET_HEADS = 4
RET_DIM = 128
RET_CHUNK = 128
ROPE_BASE = 10000.0
RET_COL0 = SHIFT_WIDTH
GATE_COL0 = SHIFT_WIDTH + 4 * RET_WIDTH
IN_WIDTH = GATE_COL0 + 2 * D_MODEL
N_GROUPS = 4
EXPERTS_PER_GROUP = 4
N_EXPERTS = 16
EXPERT_FF = 256
RMS_EPS = 1e-6
PAST_LEN = 16384

LANES = 128
SUBLANES = 8
VMEM_LIMIT = 56 * 1024 * 1024

LEAD = 128
PK_W, PK_NKK, PK_BB, PK_K, PK_R = range(5)
K_LO = 32


def _split_bf16(a, nterms):
    pieces = []
    rem = a
    for i in range(nterms):
        p = rem.astype(BF16)
        pieces.append(p)
        if i + 1 < nterms:
            rem = rem - p.astype(F32)
    return pieces


def _mm(a, w_refs, cols=None):
    def w(i):
        r = w_refs[i]
        return r[...] if cols is None else r[:, cols[0]:cols[1]]

    if a.dtype == BF16:
        acc = jnp.dot(a, w(0), preferred_element_type=F32)
        if len(w_refs) > 1:
            acc = acc + jnp.dot(a, w(1), preferred_element_type=F32)
        return acc
    if len(w_refs) == 1:
        return jnp.dot(a.astype(BF16), w(0), preferred_element_type=F32)
    a_hi, a_lo = _split_bf16(a, 2)
    acc = jnp.dot(a_hi, w(0), preferred_element_type=F32)
    acc = acc + jnp.dot(a_lo, w(0), preferred_element_type=F32)
    acc = acc + jnp.dot(a_hi, w(1), preferred_element_type=F32)
    return acc


def _rmsnorm(x, g):
    return x * lax.rsqrt(jnp.mean(x * x, axis=-1, keepdims=True) + RMS_EPS) * g


def _sigmoid(x):
    return 1.0 / (1.0 + jnp.exp(-x))


def _head64_sum(x):
    lane = lax.broadcasted_iota(jnp.int32, (x.shape[0], LANES), 1)
    lo = lane < RWKV_HEAD_DIM
    outs = []
    for j in range(x.shape[1] // LANES):
        xj = x[:, j * LANES:(j + 1) * LANES]
        s_lo = jnp.sum(jnp.where(lo, xj, 0.0), axis=-1, keepdims=True)
        s_hi = jnp.sum(jnp.where(lo, 0.0, xj), axis=-1, keepdims=True)
        outs.append(jnp.where(lo, s_lo, s_hi))
    return jnp.concatenate(outs, axis=-1)


def _rope(x, cc, ss):
    lane = lax.broadcasted_iota(jnp.int32, x.shape, 1)
    even = (lane & 1) == 0
    swapped = jnp.where(even, pltpu.roll(x, LANES - 1, 1), pltpu.roll(x, 1, 1))
    return x * cc + swapped * ss


def _rwkv_prep(s, prev, prm):
    (mu, w0, dup, a0, aup, gup, k_k, k_a, r_k) = prm
    xs = s + mu[...] * (prev - s)
    W = RWKV_WIDTH
    r = xs[:, 0:W]
    k = xs[:, W:2 * W]
    v = xs[:, 2 * W:3 * W]
    wa = xs[:, 3 * W:3 * W + LORA_PAD]
    gd = xs[:, 3 * W + LORA_PAD:]
    dec = _mm(jnp.tanh(wa), dup)
    z = -(w0[...] + dec)
    softplus = jnp.maximum(z, 0.0) + jnp.log1p(jnp.exp(-jnp.abs(z)))
    decay = jnp.exp(-jnp.exp(-softplus - 0.5))
    a = _sigmoid(a0[...] + _mm(wa, aup))
    g = _mm(_sigmoid(gd), gup)
    kk = k * k_k[...]
    kk = kk / jnp.maximum(jnp.sqrt(_head64_sum(kk * kk)), 1e-12)
    k_mod = k * (1.0 + (a - 1.0) * k_a[...])
    bonus = _head64_sum(r * k_mod * r_k[...]) * v
    return r, decay, k_mod, v, kk, a, g, bonus


def _take_prep_params(refs, n_w):
    it = iter(refs)
    take = lambda n: tuple(next(it) for _ in range(n))
    (mu, w0), dup, (a0,), aup, gup, (k_k, k_a, r_k) = take(2), take(n_w), take(1), take(n_w), take(n_w), take(3)
    return (mu, w0, dup, a0, aup, gup, k_k, k_a, r_k), tuple(it)


def _store_rwkv_operands(ops, pk_ref, v_ref, g_ref, bonus_ref):
    r, decay, k_mod, v, kk, a, g, bonus = ops
    W = RWKV_WIDTH
    pk_ref[:, PK_W * W:(PK_W + 1) * W] = decay
    pk_ref[:, PK_NKK * W:(PK_NKK + 1) * W] = -kk
    pk_ref[:, PK_BB * W:(PK_BB + 1) * W] = kk * a
    pk_ref[:, PK_K * W:(PK_K + 1) * W] = k_mod
    pk_ref[:, PK_R * W:(PK_R + 1) * W] = r
    v_ref[...] = v
    g_ref[...] = g
    bonus_ref[...] = bonus


def _inproj_prompt_kernel(n_w, x_ref, cc_ref, ss_ref, rt_ref, gc_ref, n1_ref, *rest):
    w_in = rest[:n_w]
    prm, rest = _take_prep_params(rest[n_w:], n_w)
    (pk_ref, v_ref, g_ref, bonus_ref, gates_ref, ob_ref, shift_ref, ret_ref, prev_buf, ret_state) = rest
    t = pl.program_id(1)
    nb, tm, _ = x_ref.shape
    rows = [slice(b * tm, (b + 1) * tm) for b in range(nb)]

    @pl.when(t == 0)
    def _():
        prev_buf[:, 0:SUBLANES, :] = jnp.zeros((nb, SUBLANES, SHIFT_WIDTH), F32)
        ret_state[...] = jnp.zeros(ret_state.shape, F32)

    u = _rmsnorm(x_ref[...].reshape(nb * tm, D_MODEL), n1_ref[...])
    u_b = u.astype(BF16) if n_w == 1 else u

    s = _mm(u_b, w_in, (0, SHIFT_WIDTH))
    prevs = []
    for b in range(nb):
        s_b = s[rows[b]]
        prev_buf[b, SUBLANES:SUBLANES + tm, :] = s_b
        prevs.append(prev_buf[b, SUBLANES - 1:SUBLANES - 1 + tm, :])
        prev_buf[b, SUBLANES - 1:SUBLANES, :] = s_b[tm - 1:tm, :]
        shift_ref[b] = s_b[tm - 1:tm, :]
    prev = jnp.concatenate(prevs, axis=0)

    r, decay, k_mod, v, kk, a, g, bonus = _rwkv_prep(s, prev, prm)
    W = RWKV_WIDTH
    for b in range(nb):
        rb = rows[b]
        pk_ref[b, :, PK_W * W:(PK_W + 1) * W] = decay[rb]
        pk_ref[b, :, PK_NKK * W:(PK_NKK + 1) * W] = -kk[rb]
        pk_ref[b, :, PK_BB * W:(PK_BB + 1) * W] = (kk * a)[rb]
        pk_ref[b, :, PK_K * W:(PK_K + 1) * W] = k_mod[rb]
        pk_ref[b, :, PK_R * W:(PK_R + 1) * W] = r[rb]
        v_ref[b] = v[rb]
        g_ref[b] = g[rb]
        bonus_ref[b] = bonus[rb]

    gates = _mm(u_b, w_in, (GATE_COL0, IN_WIDTH))
    for b in range(nb):
        gates_ref[b] = gates[rows[b]]

    ret = _mm(u_b, w_in, (RET_COL0, GATE_COL0))
    cc = cc_ref[...]
    ss = ss_ref[...]
    for b in range(nb):
        rb = rows[b]
        for h in range(RET_HEADS):
            c0 = h * RET_DIM
            qc = _rope(ret[rb, c0:c0 + RET_DIM], cc, ss).astype(BF16)
            kc = _rope(ret[rb, RET_WIDTH + c0:RET_WIDTH + c0 + RET_DIM], cc, ss) * (RET_DIM ** -0.5)
            vc = ret[rb, 2 * RET_WIDTH + c0:2 * RET_WIDTH + c0 + RET_DIM].astype(BF16)
            gr = ret[rb, 3 * RET_WIDTH + c0:3 * RET_WIDTH + c0 + RET_DIM]
            state = ret_state[b, h]
            scores = lax.dot_general(qc, kc.astype(BF16), (((1,), (1,)), ((), ())),
                                     preferred_element_type=F32) * rt_ref[h, 0]
            intra = jnp.dot(scores.astype(BF16), vc, preferred_element_type=F32)
            cross = jnp.dot(qc, state.astype(BF16), preferred_element_type=F32) * rt_ref[h, 1]
            kt = (kc * rt_ref[h, 2]).T.astype(BF16)
            ret_state[b, h] = gc_ref[h, 0:1, :] * state + jnp.dot(kt, vc, preferred_element_type=F32)
            o = intra + cross
            o = o * lax.rsqrt(jnp.mean(o * o, axis=-1, keepdims=True) + RMS_EPS)
            ob_ref[b, :, c0:c0 + RET_DIM] = (gr * _sigmoid(gr) * o).astype(ob_ref.dtype)

    @pl.when(t == pl.num_programs(1) - 1)
    def _():
        ret_ref[...] = ret_state[...]


def _inproj_sample_kernel(n_w, x_ref, prev_ref, cc_ref, ss_ref, n1_ref, *rest):
    w_in = rest[:n_w]
    prm, rest = _take_prep_params(rest[n_w:], n_w)
    (pk_ref, v_ref, g_ref, bonus_ref, gates_ref, retq_ref, shift_ref) = rest
    u = _rmsnorm(x_ref[...], n1_ref[...])
    s = _mm(u, w_in, (0, SHIFT_WIDTH))
    shift_ref[...] = s
    _store_rwkv_operands(_rwkv_prep(s, prev_ref[...], prm), pk_ref, v_ref, g_ref, bonus_ref)
    gates_ref[...] = _mm(u, w_in, (GATE_COL0, IN_WIDTH))
    ret = _mm(u, w_in, (RET_COL0, GATE_COL0))
    cc = cc_ref[...]
    ss = ss_ref[...]
    for h in range(RET_HEADS):
        c0 = h * RET_DIM
        retq_ref[:, c0:c0 + RET_DIM] = _rope(ret[:, c0:c0 + RET_DIM], cc, ss)
        retq_ref[:, RET_WIDTH + c0:RET_WIDTH + c0 + RET_DIM] = (
            _rope(ret[:, RET_WIDTH + c0:RET_WIDTH + c0 + RET_DIM], cc, ss) * (RET_DIM ** -0.5))
    retq_ref[:, 2 * RET_WIDTH:] = ret[:, 2 * RET_WIDTH:]


def _const_spec(arr):
    nd = arr.ndim
    return pl.BlockSpec(arr.shape, lambda *_: (0,) * nd, pipeline_mode=pl.Buffered(1))


def _inproj_prompt(xp, cc, ss, rt_tab, gc_tab, n1, w_in, prm, nb):
    B, TP, _ = xp.shape
    tm = RET_CHUNK
    assert TP % tm == 0 and B % nb == 0
    n_w = len(w_in)
    consts = (rt_tab, gc_tab, n1) + tuple(w_in) + tuple(prm)
    row = lambda width: pl.BlockSpec((nb, tm, width), lambda b, t: (b, t, 0))
    in_specs = [row(D_MODEL),
                pl.BlockSpec((tm, LANES), lambda b, t: (t, 0)),
                pl.BlockSpec((tm, LANES), lambda b, t: (t, 0))] + [_const_spec(c) for c in consts]
    out_shape = (
        jax.ShapeDtypeStruct((B, TP, 5 * RWKV_WIDTH), F32),
        jax.ShapeDtypeStruct((B, TP, RWKV_WIDTH), F32),
        jax.ShapeDtypeStruct((B, TP, RWKV_WIDTH), F32),
        jax.ShapeDtypeStruct((B, TP, RWKV_WIDTH), F32),
        jax.ShapeDtypeStruct((B, TP, 2 * D_MODEL), F32),
        jax.ShapeDtypeStruct((B, TP, RET_WIDTH), BF16),
        jax.ShapeDtypeStruct((B, 1, SHIFT_WIDTH), F32),
        jax.ShapeDtypeStruct((B, RET_HEADS, RET_DIM, RET_DIM), F32),
    )
    out_specs = (
        row(5 * RWKV_WIDTH), row(RWKV_WIDTH), row(RWKV_WIDTH), row(RWKV_WIDTH), row(2 * D_MODEL), row(RET_WIDTH),
        pl.BlockSpec((nb, 1, SHIFT_WIDTH), lambda b, t: (b, 0, 0)),
        pl.BlockSpec((nb, RET_HEADS, RET_DIM, RET_DIM), lambda b, t: (b, 0, 0, 0)),
    )
    return pl.pallas_call(
        functools.partial(_inproj_prompt_kernel, n_w),
        grid=(B // nb, TP // tm),
        in_specs=in_specs,
        out_specs=out_specs,
        out_shape=out_shape,
        scratch_shapes=[pltpu.VMEM((nb, tm + SUBLANES, SHIFT_WIDTH), F32),
                        pltpu.VMEM((nb, RET_HEADS, RET_DIM, RET_DIM), F32)],
        compiler_params=pltpu.CompilerParams(
            dimension_semantics=("arbitrary", "arbitrary"), vmem_limit_bytes=VMEM_LIMIT),
        name="inproj_prompt",
    )(xp, cc, ss, *consts)


def _inproj_sample(xs, prev, cc, ss, n1, w_in, prm):
    N = xs.shape[0]
    n_w = len(w_in)
    args = (xs, prev, cc, ss, n1) + tuple(w_in) + tuple(prm)
    out_shape = (
        jax.ShapeDtypeStruct((N, 5 * RWKV_WIDTH), F32),
        jax.ShapeDtypeStruct((N, RWKV_WIDTH), F32),
        jax.ShapeDtypeStruct((N, RWKV_WIDTH), F32),
        jax.ShapeDtypeStruct((N, RWKV_WIDTH), F32),
        jax.ShapeDtypeStruct((N, 2 * D_MODEL), F32),
        jax.ShapeDtypeStruct((N, 4 * RET_WIDTH), F32),
        jax.ShapeDtypeStruct((N, SHIFT_WIDTH), F32),
    )
    return pl.pallas_call(
        functools.partial(_inproj_sample_kernel, n_w),
        out_shape=out_shape,
        compiler_params=pltpu.CompilerParams(vmem_limit_bytes=VMEM_LIMIT),
        name="inproj_sample",
    )(*args)


def _rwkv_prompt_kernel(pk_ref, v_ref, y_ref, sout_ref, state):
    i = pl.program_id(0)
    HD = RWKV_HEAD_DIM

    @pl.when(i == 0)
    def _():
        state[...] = jnp.zeros(state.shape, F32)

    def step(t, carry):
        sa = jnp.zeros((HD, LANES), F32)
        for kl in range(K_LO):
            sa = sa + state[kl * HD:(kl + 1) * HD, :] * pk_ref[t, PK_NKK, kl:kl + 1, :]
        sa = sa + pltpu.roll(sa, HD, 1)
        vv = v_ref[t]
        yacc = jnp.zeros((HD, LANES), F32)
        for kl in range(K_LO):
            rows = slice(kl * HD, (kl + 1) * HD)
            new = (state[rows, :] * pk_ref[t, PK_W, kl:kl + 1, :]
                   + sa * pk_ref[t, PK_BB, kl:kl + 1, :]
                   + vv * pk_ref[t, PK_K, kl:kl + 1, :])
            state[rows, :] = new
            yacc = yacc + new * pk_ref[t, PK_R, kl:kl + 1, :]
        y_ref[t] = yacc + pltpu.roll(yacc, HD, 1)
        return carry

    lax.fori_loop(0, pk_ref.shape[0], step, 0)

    @pl.when(i == pl.num_programs(0) - 1)
    def _():
        sout_ref[...] = state[...]


def _rwkv_prompt(pk_t, v_t, tb):
    T = pk_t.shape[0]
    HD = RWKV_HEAD_DIM
    return pl.pallas_call(
        _rwkv_prompt_kernel,
        grid=(T // tb,),
        in_specs=[pl.BlockSpec((tb, 5, K_LO, LANES), lambda i: (i, 0, 0, 0)),
                  pl.BlockSpec((tb, HD, LANES), lambda i: (i, 0, 0))],
        out_specs=(pl.BlockSpec((tb, HD, LANES), lambda i: (i, 0, 0)),
                   pl.BlockSpec((K_LO * HD, LANES), lambda i: (0, 0))),
        out_shape=(jax.ShapeDtypeStruct((T, HD, LANES), F32),
                   jax.ShapeDtypeStruct((K_LO * HD, LANES), F32)),
        scratch_shapes=[pltpu.VMEM((K_LO * HD, LANES), F32)],
        compiler_params=pltpu.CompilerParams(dimension_semantics=("arbitrary",)),
        name="rwkv_prompt",
    )(pk_t, v_t)


def _lane_col(tile, h):
    lane = lax.broadcasted_iota(jnp.int32, tile.shape, 1)
    return jnp.sum(jnp.where(lane == h, tile, 0.0), axis=-1, keepdims=True)


def _rwkv_sample_kernel(s_ref, pk_ref, vt_ref, sout_ref, yt_ref):
    for b in range(s_ref.shape[0]):
        v_tile = vt_ref[b]
        lane = lax.broadcasted_iota(jnp.int32, v_tile.shape, 1)
        y_tile = jnp.zeros(v_tile.shape, F32)
        for h in range(RWKV_HEADS):
            S = s_ref[b, h]
            sa = jnp.sum(S * pk_ref[b, PK_NKK, h:h + 1, :], axis=-1, keepdims=True)
            new = (S * pk_ref[b, PK_W, h:h + 1, :]
                   + sa * pk_ref[b, PK_BB, h:h + 1, :]
                   + _lane_col(v_tile, h) * pk_ref[b, PK_K, h:h + 1, :])
            sout_ref[b, h] = new
            y = jnp.sum(new * pk_ref[b, PK_R, h:h + 1, :], axis=-1, keepdims=True)
            y_tile = jnp.where(lane == h, y, y_tile)
        yt_ref[b] = y_tile


def _rwkv_sample(state, pk_h, v_t, bt):
    N = state.shape[0]
    H, HD = RWKV_HEADS, RWKV_HEAD_DIM
    return pl.pallas_call(
        _rwkv_sample_kernel,
        grid=(N // bt,),
        in_specs=[pl.BlockSpec((bt, H, HD, HD), lambda i: (i, 0, 0, 0)),
                  pl.BlockSpec((bt, 5, H, HD), lambda i: (i, 0, 0, 0)),
                  pl.BlockSpec((bt, HD, H), lambda i: (i, 0, 0))],
        out_specs=(pl.BlockSpec((bt, H, HD, HD), lambda i: (i, 0, 0, 0)),
                   pl.BlockSpec((bt, HD, H), lambda i: (i, 0, 0))),
        out_shape=(jax.ShapeDtypeStruct(state.shape, F32),
                   jax.ShapeDtypeStruct((N, HD, H), F32)),
        compiler_params=pltpu.CompilerParams(dimension_semantics=("arbitrary",)),
        name="rwkv_sample",
    )(state, pk_h, v_t)


def _ret_sample_kernel(s_ref, qt_ref, kt_ref, v_ref, g_ref, gam_ref, sout_ref, ob_ref):
    for b in range(s_ref.shape[0]):
        q_tile = qt_ref[b]
        k_tile = kt_ref[b]
        for h in range(RET_HEADS):
            S = s_ref[b, h]
            qc = _lane_col(q_tile, h)
            kc = _lane_col(k_tile, h)
            vrow = v_ref[b, h:h + 1, :]
            gam = gam_ref[h:h + 1, :]
            qk = jnp.sum(qc * kc, axis=0, keepdims=True)
            cross = jnp.sum(qc * S, axis=0, keepdims=True)
            o = qk * vrow + cross * gam
            sout_ref[b, h] = gam * S + kc * vrow
            o = o * lax.rsqrt(jnp.mean(o * o, axis=-1, keepdims=True) + RMS_EPS)
            gr = g_ref[b, h:h + 1, :]
            ob_ref[b, h:h + 1, :] = gr * _sigmoid(gr) * o


def _ret_sample(state, q_t, k_t, v_h, g_h, gam, bt):
    N = state.shape[0]
    H, DK = RET_HEADS, RET_DIM
    col = pl.BlockSpec((bt, DK, H), lambda i: (i, 0, 0))
    rowspec = pl.BlockSpec((bt, H, DK), lambda i: (i, 0, 0))
    st = pl.BlockSpec((bt, H, DK, DK), lambda i: (i, 0, 0, 0))
    return pl.pallas_call(
        _ret_sample_kernel,
        grid=(N // bt,),
        in_specs=[st, col, col, rowspec, rowspec, pl.BlockSpec((H, DK), lambda i: (0, 0))],
        out_specs=(st, rowspec),
        out_shape=(jax.ShapeDtypeStruct(state.shape, F32), jax.ShapeDtypeStruct((N, H, DK), F32)),
        compiler_params=pltpu.CompilerParams(dimension_semantics=("arbitrary",)),
        name="ret_sample",
    )(state, q_t, k_t, v_h, g_h, gam)


def _mixtail_kernel(n_w, x_ref, y_ref, g_ref, bonus_ref, ob_ref, gates_ref,
                    lnw_ref, lnb_ref, n2_ref, br_ref, *rest):
    wa = rest[0:n_w]
    wb = rest[n_w:2 * n_w]
    wo = rest[2 * n_w:3 * n_w]
    wr = rest[3 * n_w:3 * n_w + 2]
    x1_ref, u2_ref, comb_ref = rest[3 * n_w + 2:]

    y = y_ref[...]
    inv_n = 1.0 / RWKV_HEAD_DIM
    mu = _head64_sum(y) * inv_n
    d = y - mu
    var = _head64_sum(d * d) * inv_n
    yn = d * lax.rsqrt(var + LNX_EPS) * lnw_ref[...] + lnb_ref[...]
    ya = (yn + bonus_ref[...]) * g_ref[...]
    branch_a = _mm(ya, wa)
    branch_b = _mm(ob_ref[...], wb)
    gates = gates_ref[...]
    merged = _sigmoid(gates[:, :D_MODEL]) * branch_a + _sigmoid(gates[:, D_MODEL:]) * branch_b
    x1 = x_ref[...] + _mm(merged, wo)
    x1_ref[...] = x1
    u2 = _rmsnorm(x1, n2_ref[...])
    u2_ref[...] = u2.astype(u2_ref.dtype)

    logits = _mm(u2, wr) + br_ref[...]
    lane = lax.broadcasted_iota(jnp.int32, logits.shape, 1)
    lanef = lane.astype(F32)
    neg = jnp.float32(-jnp.inf)
    big = jnp.float32(1e9)
    is_grp = (lane >= N_EXPERTS) & (lane < N_EXPERTS + N_GROUPS)
    lg = jnp.where(is_grp, logits, neg)
    lg_max = jnp.max(lg, axis=-1, keepdims=True)
    p_grp = 1.0 / jnp.sum(jnp.exp(lg - lg_max), axis=-1, keepdims=True)
    grp = jnp.min(jnp.where(lg == lg_max, lanef, big), axis=-1, keepdims=True) - N_EXPERTS
    in_grp = (lanef >= grp * EXPERTS_PER_GROUP) & (lanef < (grp + 1.0) * EXPERTS_PER_GROUP)
    le = jnp.where(in_grp, logits, neg)
    v1 = jnp.max(le, axis=-1, keepdims=True)
    i1 = jnp.min(jnp.where(le == v1, lanef, big), axis=-1, keepdims=True)
    le2 = jnp.where(lanef == i1, neg, le)
    v2 = jnp.max(le2, axis=-1, keepdims=True)
    i2 = jnp.min(jnp.where(le2 == v2, lanef, big), axis=-1, keepdims=True)
    e2 = jnp.exp(v2 - v1)
    w1 = 1.0 / (1.0 + e2)
    w2 = e2 / (1.0 + e2)
    comb_ref[...] = p_grp * (jnp.where(lanef == i1, w1, 0.0) + jnp.where(lanef == i2, w2, 0.0))


def _mixtail(x, y, g, bonus, ob, gates, lnw, lnb, n2, br, wa, wb, wo, wr, tm):
    N = x.shape[0]
    n_w = len(wa)
    consts = (lnw, lnb, n2, br) + tuple(wa) + tuple(wb) + tuple(wo) + tuple(wr)
    row = lambda width: pl.BlockSpec((tm, width), lambda i: (i, 0))
    in_specs = [row(D_MODEL), row(RWKV_WIDTH), row(RWKV_WIDTH), row(RWKV_WIDTH), row(RET_WIDTH),
                row(2 * D_MODEL)] + [_const_spec(c) for c in consts]
    return pl.pallas_call(
        functools.partial(_mixtail_kernel, n_w),
        grid=(N // tm,),
        in_specs=in_specs,
        out_specs=(row(D_MODEL), row(D_MODEL), row(LANES)),
        out_shape=(jax.ShapeDtypeStruct((N, D_MODEL), F32),
                   jax.ShapeDtypeStruct((N, D_MODEL), BF16),
                   jax.ShapeDtypeStruct((N, LANES), F32)),
        compiler_params=pltpu.CompilerParams(dimension_semantics=("arbitrary",), vmem_limit_bytes=VMEM_LIMIT),
        name="mixtail",
    )(x, y, g, bonus, ob, gates, *consts)


def _moe_kernel(x1_ref, u2_ref, comb_ref, wg_ref, wu_ref, wd_ref, fg_ref, o_ref, acc):
    grp = pl.program_id(1)

    @pl.when(grp == 0)
    def _():
        acc[...] = jnp.zeros(acc.shape, F32)

    u2 = u2_ref[...]
    comb = comb_ref[...]
    lane = lax.broadcasted_iota(jnp.int32, comb.shape, 1)
    total = acc[...]
    for j in range(EXPERTS_PER_GROUP):
        c = jnp.sum(jnp.where(lane == grp * EXPERTS_PER_GROUP + j, comb, 0.0), axis=-1, keepdims=True)
        hg = jnp.dot(u2, wg_ref[j], preferred_element_type=F32)
        hu = jnp.dot(u2, wu_ref[j], preferred_element_type=F32)
        h = hg * _sigmoid(hg) * hu * c
        total = total + jnp.dot(h.astype(BF16), wd_ref[j], preferred_element_type=F32)
    acc[...] = total

    @pl.when(grp == pl.num_programs(1) - 1)
    def _():
        o_ref[...] = _rmsnorm(x1_ref[...] + acc[...], fg_ref[...])


def _moe(x1, u2, comb, wg, wu, wd, fg, tm):
    N = x1.shape[0]
    E = EXPERTS_PER_GROUP
    return pl.pallas_call(
        _moe_kernel,
        grid=(N // tm, N_GROUPS),
        in_specs=[pl.BlockSpec((tm, D_MODEL), lambda i, g: (i, 0)),
                  pl.BlockSpec((tm, D_MODEL), lambda i, g: (i, 0)),
                  pl.BlockSpec((tm, LANES), lambda i, g: (i, 0)),
                  pl.BlockSpec((E, D_MODEL, EXPERT_FF), lambda i, g: (g, 0, 0)),
                  pl.BlockSpec((E, D_MODEL, EXPERT_FF), lambda i, g: (g, 0, 0)),
                  pl.BlockSpec((E, EXPERT_FF, D_MODEL), lambda i, g: (g, 0, 0)),
                  pl.BlockSpec((1, D_MODEL), lambda i, g: (0, 0))],
        out_specs=pl.BlockSpec((tm, D_MODEL), lambda i, g: (i, 0)),
        out_shape=jax.ShapeDtypeStruct((N, D_MODEL), F32),
        scratch_shapes=[pltpu.VMEM((tm, D_MODEL), F32)],
        compiler_params=pltpu.CompilerParams(
            dimension_semantics=("arbitrary", "arbitrary"), vmem_limit_bytes=VMEM_LIMIT),
        name="moe",
    )(x1, u2, comb, wg, wu, wd, fg)


def _pieces(w, n):
    hi = w.astype(BF16)
    if n == 1:
        return (hi,)
    return (hi, (w - hi.astype(F32)).astype(BF16))


def _rope_tables(pos):
    half = RET_DIM // 2
    inv = ROPE_BASE ** (-jnp.arange(half, dtype=F32) / half)
    ang = pos.astype(F32)[:, None] * inv[None, :]
    c = jnp.cos(ang)
    s = jnp.sin(ang)
    cc = jnp.repeat(c, 2, axis=-1)
    ss = jnp.stack([-s, s], axis=-1).reshape(pos.shape[0], RET_DIM)
    return cc, ss


def _retention_tables():
    C = RET_CHUNK
    log_gamma = jnp.log(1.0 - 2.0 ** (-5.0 - jnp.arange(RET_HEADS, dtype=F32)))
    idx = jnp.arange(C, dtype=F32)
    diff = idx[:, None] - idx[None, :]
    lg = log_gamma[:, None, None]
    dmask = jnp.where(diff[None] >= 0, jnp.exp(jnp.maximum(diff, 0.0)[None] * lg), 0.0)
    q_dec = jnp.exp((idx[None, :] + 1.0) * log_gamma[:, None])
    k_dec = jnp.exp((C - 1.0 - idx[None, :]) * log_gamma[:, None])
    bc = lambda a: jnp.broadcast_to(a[:, :, None], (RET_HEADS, C, C))
    rt_tab = jnp.stack([dmask, bc(q_dec), bc(k_dec)], axis=1)
    gc_tab = jnp.broadcast_to(jnp.exp(C * log_gamma)[:, None, None], (RET_HEADS, SUBLANES, RET_DIM))
    gam1 = jnp.broadcast_to(jnp.exp(log_gamma)[:, None], (RET_HEADS, RET_DIM))
    return rt_tab, gc_tab, gam1


def kernel(x_prompt, x_sample, state_wkv, state_shift, state_ret, meta_tokens, norm1_g, w_in, shift_mu, decay_w0, decay_up, aaa_a0, aaa_up, gate_up, k_k, k_a, r_k, lnx_w, lnx_b, w_branch_a, w_branch_b, w_out, norm2_g, router_group_w, router_group_b, router_expert_w, router_expert_b, expert_w_gate, expert_w_up, expert_w_down, final_norm_g):
    B, T, D = x_prompt.shape
    NS = x_sample.shape[0]
    TP = LEAD + T
    H, HD = RWKV_HEADS, RWKV_HEAD_DIM
    l = 0
    row = lambda a: a.reshape(1, -1)

    zpad = jnp.zeros((LORA_PAD - 64, RWKV_WIDTH), F32)
    dup_pad = jnp.concatenate([decay_up[l], zpad], axis=0)
    aup_pad = jnp.concatenate([zpad, aaa_up[l]], axis=0)
    wr_full = jnp.concatenate([router_expert_w[l], router_group_w[l]], axis=1)
    wr_full = jnp.pad(wr_full, ((0, 0), (0, LANES - wr_full.shape[1])))
    br_full = jnp.pad(jnp.concatenate([router_expert_b[l], router_group_b[l]]), (0, LANES - N_EXPERTS - N_GROUPS))
    wr = _pieces(wr_full, 2)
    wg = expert_w_gate[l].astype(BF16)
    wu = expert_w_up[l].astype(BF16)
    wd = expert_w_down[l].astype(BF16)
    rt_tab, gc_tab, gam1 = _retention_tables()

    def prep_params(n):
        return (row(shift_mu[l]), row(decay_w0[l]), _pieces(dup_pad, n), row(aaa_a0[l]), _pieces(aup_pad, n),
                _pieces(gate_up[l], n), row(k_k[l]), row(k_a[l]), row(r_k[l]))

    def flat_params(p):
        out = []
        for a in p:
            out.extend(a if isinstance(a, tuple) else (a,))
        return tuple(out)

    def tail(n, x, y, g, bonus, ob, gates, tm_tail, tm_moe):
        x1, u2, comb = _mixtail(x, y, g, bonus, ob, gates, row(lnx_w[l]), row(lnx_b[l]), row(norm2_g[l]),
                                row(br_full), _pieces(w_branch_a[l], n), _pieces(w_branch_b[l], n),
                                _pieces(w_out[l], n), wr, tm_tail)
        return _moe(x1, u2, comb, wg, wu, wd, row(final_norm_g), tm_moe)

    lead = jnp.concatenate([jnp.zeros((LEAD - N_META, D), F32), meta_tokens.astype(F32)], axis=0)
    xp = jnp.concatenate([jnp.broadcast_to(lead[None], (B, LEAD, D)), x_prompt], axis=1)
    cc_p, ss_p = _rope_tables(jnp.arange(TP, dtype=jnp.int32) - (LEAD - N_META))
    pk, v, g, bonus, gates, ob, shift_p, ret_p = _inproj_prompt(
        xp, cc_p, ss_p, rt_tab, gc_tab, row(norm1_g[l]), _pieces(w_in[l], 1), flat_params(prep_params(1)), nb=2)

    pk_t = pk.reshape(B, TP, 5, H, 2, K_LO).transpose(1, 2, 5, 4, 0, 3).reshape(TP, 5, K_LO, LANES)
    v_t = v.reshape(B, TP, H, HD).transpose(1, 3, 0, 2).reshape(TP, HD, B * H)
    v_t = jnp.concatenate([v_t, v_t], axis=-1)
    y_t, s_fin = _rwkv_prompt(pk_t, v_t, tb=16)
    y = y_t[:, :, :B * H].reshape(TP, HD, B, H).transpose(2, 0, 3, 1).reshape(B * TP, RWKV_WIDTH)
    wkv_p = s_fin.reshape(K_LO, HD, 2, B, H).transpose(3, 4, 1, 2, 0).reshape(B, H, HD, HD)

    flat = lambda a: a.reshape(B * TP, a.shape[-1])
    y_full = tail(1, flat(xp), y, flat(g), flat(bonus), flat(ob), flat(gates), 256, 1024)
    y_prompt = y_full.reshape(B, TP, D)[:, LEAD:]

    xs = x_sample.reshape(NS, D)
    cc_s, ss_s = _rope_tables(jnp.full((NS,), PAST_LEN, jnp.int32))
    pk_s, v_s, g_s, bonus_s, gates_s, retq, shift_s = _inproj_sample(
        xs, state_shift[l], cc_s, ss_s, row(norm1_g[l]), _pieces(w_in[l], 2), flat_params(prep_params(2)))
    wkv_s, yt_s = _rwkv_sample(state_wkv[l], pk_s.reshape(NS, 5, H, HD),
                               v_s.reshape(NS, H, HD).transpose(0, 2, 1), bt=8)
    y_s = yt_s.transpose(0, 2, 1).reshape(NS, RWKV_WIDTH)
    heads = lambda a: a.reshape(NS, RET_HEADS, RET_DIM)
    q_s, k_s, v_r, g_r = (retq[:, i * RET_WIDTH:(i + 1) * RET_WIDTH] for i in range(4))
    ret_s, ob_s = _ret_sample(state_ret[l], heads(q_s).transpose(0, 2, 1), heads(k_s).transpose(0, 2, 1),
                              heads(v_r), heads(g_r), gam1, bt=8)
    y_sample = tail(2, xs, y_s, g_s, bonus_s, ob_s.reshape(NS, RET_WIDTH), gates_s, NS, NS)

    return (y_prompt, y_sample.reshape(NS, 1, D),
            wkv_p[None], shift_p.reshape(1, B, SHIFT_WIDTH), ret_p[None],
            wkv_s[None], shift_s[None], ret_s[None])
```

```python
import functools

import jax
import jax.numpy as jnp
from jax import lax
from jax.experimental import pallas as pl
from jax.experimental.pallas import tpu as pltpu

F32 = jnp.float32
BF16 = jnp.bfloat16

D_MODEL = 1024
N_META = 16
RWKV_WIDTH = 512
RWKV_HEADS = 8
RWKV_HEAD_DIM = 64
LORA_PAD = 128
GATE_LORA = 128
SHIFT_WIDTH = 3 * RWKV_WIDTH + 64 + 64 + GATE_LORA
LNX_EPS = 64e-5
RET_WIDTH = 512
RET_HEADS = 4
RET_DIM = 128
RET_CHUNK = 128
ROPE_BASE = 10000.0
RET_COL0 = SHIFT_WIDTH
GATE_COL0 = SHIFT_WIDTH + 4 * RET_WIDTH
IN_WIDTH = GATE_COL0 + 2 * D_MODEL
N_GROUPS = 4
EXPERTS_PER_GROUP = 4
N_EXPERTS = 16
EXPERT_FF = 256
RMS_EPS = 1e-6
PAST_LEN = 16384

LANES = 128
SUBLANES = 8
VMEM_LIMIT = 56 * 1024 * 1024

LEAD = 128
PK_W, PK_NKK, PK_BB, PK_K, PK_R, PK_V = range(6)
K_LO = 32


def _split_bf16(a, nterms):
    pieces = []
    rem = a
    for i in range(nterms):
        p = rem.astype(BF16)
        pieces.append(p)
        if i + 1 < nterms:
            rem = rem - p.astype(F32)
    return pieces


def _mm(a, w_refs, cols=None):
    def w(i):
        r = w_refs[i]
        return r[...] if cols is None else r[:, cols[0]:cols[1]]

    if a.dtype == BF16:
        acc = jnp.dot(a, w(0), preferred_element_type=F32)
        if len(w_refs) > 1:
            acc = acc + jnp.dot(a, w(1), preferred_element_type=F32)
        return acc
    if len(w_refs) == 1:
        return jnp.dot(a.astype(BF16), w(0), preferred_element_type=F32)
    a_hi, a_lo = _split_bf16(a, 2)
    acc = jnp.dot(a_hi, w(0), preferred_element_type=F32)
    acc = acc + jnp.dot(a_lo, w(0), preferred_element_type=F32)
    acc = acc + jnp.dot(a_hi, w(1), preferred_element_type=F32)
    return acc


def _rmsnorm(x, g):
    return x * lax.rsqrt(jnp.mean(x * x, axis=-1, keepdims=True) + RMS_EPS) * g


def _sigmoid(x):
    return 1.0 / (1.0 + jnp.exp(-x))


def _head64_sum(x):
    lane = lax.broadcasted_iota(jnp.int32, (x.shape[0], LANES), 1)
    lo = lane < RWKV_HEAD_DIM
    outs = []
    for j in range(x.shape[1] // LANES):
        xj = x[:, j * LANES:(j + 1) * LANES]
        s_lo = jnp.sum(jnp.where(lo, xj, 0.0), axis=-1, keepdims=True)
        s_hi = jnp.sum(jnp.where(lo, 0.0, xj), axis=-1, keepdims=True)
        outs.append(jnp.where(lo, s_lo, s_hi))
    return jnp.concatenate(outs, axis=-1)


def _rope(x, cc, ss):
    lane = lax.broadcasted_iota(jnp.int32, x.shape, 1)
    even = (lane & 1) == 0
    swapped = jnp.where(even, pltpu.roll(x, LANES - 1, 1), pltpu.roll(x, 1, 1))
    return x * cc + swapped * ss


def _rwkv_prep(s, prev, prm):
    (mu, w0, dup, a0, aup, gup, k_k, k_a, r_k) = prm
    xs = s + mu[...] * (prev - s)
    W = RWKV_WIDTH
    r = xs[:, 0:W]
    k = xs[:, W:2 * W]
    v = xs[:, 2 * W:3 * W]
    wa = xs[:, 3 * W:3 * W + LORA_PAD]
    gd = xs[:, 3 * W + LORA_PAD:]
    dec = _mm(jnp.tanh(wa), dup)
    z = -(w0[...] + dec)
    softplus = jnp.maximum(z, 0.0) + jnp.log1p(jnp.exp(-jnp.abs(z)))
    decay = jnp.exp(-jnp.exp(-softplus - 0.5))
    a = _sigmoid(a0[...] + _mm(wa, aup))
    g = _mm(_sigmoid(gd), gup)
    kk = k * k_k[...]
    kk = kk / jnp.maximum(jnp.sqrt(_head64_sum(kk * kk)), 1e-12)
    k_mod = k * (1.0 + (a - 1.0) * k_a[...])
    bonus = _head64_sum(r * k_mod * r_k[...]) * v
    return r, decay, k_mod, v, kk, a, g, bonus


def _take_prep_params(refs, n_w):
    it = iter(refs)
    take = lambda n: tuple(next(it) for _ in range(n))
    (mu, w0), dup, (a0,), aup, gup, (k_k, k_a, r_k) = take(2), take(n_w), take(1), take(n_w), take(n_w), take(3)
    return (mu, w0, dup, a0, aup, gup, k_k, k_a, r_k), tuple(it)


def _store_rwkv_operands(ops, pk_ref, v_ref, g_ref, bonus_ref):
    r, decay, k_mod, v, kk, a, g, bonus = ops
    W = RWKV_WIDTH
    pk_ref[:, PK_W * W:(PK_W + 1) * W] = decay
    pk_ref[:, PK_NKK * W:(PK_NKK + 1) * W] = -kk
    pk_ref[:, PK_BB * W:(PK_BB + 1) * W] = kk * a
    pk_ref[:, PK_K * W:(PK_K + 1) * W] = k_mod
    pk_ref[:, PK_R * W:(PK_R + 1) * W] = r
    v_ref[...] = v
    g_ref[...] = g
    bonus_ref[...] = bonus


def _inproj_prompt_kernel(n_w, x_ref, cc_ref, ss_ref, lead_ref, rt_ref, gc_ref, n1_ref, *rest):
    w_in = rest[:n_w]
    prm, rest = _take_prep_params(rest[n_w:], n_w)
    (pk_ref, g_ref, bonus_ref, gates_ref, ob_ref, shift_ref, ret_ref, prev_buf, ret_state) = rest
    t = pl.program_id(1)
    nb, tm, _ = x_ref.shape
    rows = [slice(b * tm, (b + 1) * tm) for b in range(nb)]

    @pl.when(t == 0)
    def _():
        prev_buf[:, 0:SUBLANES, :] = jnp.zeros((nb, SUBLANES, SHIFT_WIDTH), F32)
        ret_state[...] = jnp.zeros(ret_state.shape, F32)

    x = jnp.where(t == 0, lead_ref[...][None], x_ref[...])
    u = _rmsnorm(x.reshape(nb * tm, D_MODEL), n1_ref[...])
    u_b = u.astype(BF16) if n_w == 1 else u

    s = _mm(u_b, w_in, (0, SHIFT_WIDTH))
    prevs = []
    for b in range(nb):
        s_b = s[rows[b]]
        prev_buf[b, SUBLANES:SUBLANES + tm, :] = s_b
        prevs.append(prev_buf[b, SUBLANES - 1:SUBLANES - 1 + tm, :])
        prev_buf[b, SUBLANES - 1:SUBLANES, :] = s_b[tm - 1:tm, :]
        shift_ref[b] = s_b[tm - 1:tm, :]
    prev = jnp.concatenate(prevs, axis=0)

    r, decay, k_mod, v, kk, a, g, bonus = _rwkv_prep(s, prev, prm)
    W = RWKV_WIDTH
    for b in range(nb):
        rb = rows[b]
        pk_ref[b, :, PK_W * W:(PK_W + 1) * W] = decay[rb]
        pk_ref[b, :, PK_NKK * W:(PK_NKK + 1) * W] = -kk[rb]
        pk_ref[b, :, PK_BB * W:(PK_BB + 1) * W] = (kk * a)[rb]
        pk_ref[b, :, PK_K * W:(PK_K + 1) * W] = k_mod[rb]
        pk_ref[b, :, PK_R * W:(PK_R + 1) * W] = r[rb]
        pk_ref[b, :, PK_V * W:(PK_V + 1) * W] = v[rb]
        g_ref[b] = g[rb]
        bonus_ref[b] = bonus[rb]

    gates = _mm(u_b, w_in, (GATE_COL0, IN_WIDTH))
    for b in range(nb):
        gates_ref[b] = gates[rows[b]]

    ret = _mm(u_b, w_in, (RET_COL0, GATE_COL0))
    cc = cc_ref[...]
    ss = ss_ref[...]
    for b in range(nb):
        rb = rows[b]
        for h in range(RET_HEADS):
            c0 = h * RET_DIM
            qc = _rope(ret[rb, c0:c0 + RET_DIM], cc, ss).astype(BF16)
            kc = _rope(ret[rb, RET_WIDTH + c0:RET_WIDTH + c0 + RET_DIM], cc, ss) * (RET_DIM ** -0.5)
            vc = ret[rb, 2 * RET_WIDTH + c0:2 * RET_WIDTH + c0 + RET_DIM].astype(BF16)
            gr = ret[rb, 3 * RET_WIDTH + c0:3 * RET_WIDTH + c0 + RET_DIM]
            state = ret_state[b, h]
            scores = lax.dot_general(qc, kc.astype(BF16), (((1,), (1,)), ((), ())),
                                     preferred_element_type=F32) * rt_ref[h, 0]
            intra = jnp.dot(scores.astype(BF16), vc, preferred_element_type=F32)
            cross = jnp.dot(qc, state.astype(BF16), preferred_element_type=F32) * rt_ref[h, 1]
            kt = (kc * rt_ref[h, 2]).T.astype(BF16)
            ret_state[b, h] = gc_ref[h, 0:1, :] * state + jnp.dot(kt, vc, preferred_element_type=F32)
            o = intra + cross
            o = o * lax.rsqrt(jnp.mean(o * o, axis=-1, keepdims=True) + RMS_EPS)
            ob_ref[b, :, c0:c0 + RET_DIM] = (gr * _sigmoid(gr) * o).astype(ob_ref.dtype)

    @pl.when(t == pl.num_programs(1) - 1)
    def _():
        ret_ref[...] = ret_state[...]


def _inproj_sample_kernel(n_w, x_ref, prev_ref, cc_ref, ss_ref, n1_ref, *rest):
    w_in = rest[:n_w]
    prm, rest = _take_prep_params(rest[n_w:], n_w)
    (pk_ref, v_ref, g_ref, bonus_ref, gates_ref, retq_ref, shift_ref) = rest
    u = _rmsnorm(x_ref[...], n1_ref[...])
    s = _mm(u, w_in, (0, SHIFT_WIDTH))
    shift_ref[...] = s
    _store_rwkv_operands(_rwkv_prep(s, prev_ref[...], prm), pk_ref, v_ref, g_ref, bonus_ref)
    gates_ref[...] = _mm(u, w_in, (GATE_COL0, IN_WIDTH))
    ret = _mm(u, w_in, (RET_COL0, GATE_COL0))
    cc = cc_ref[...]
    ss = ss_ref[...]
    for h in range(RET_HEADS):
        c0 = h * RET_DIM
        retq_ref[:, c0:c0 + RET_DIM] = _rope(ret[:, c0:c0 + RET_DIM], cc, ss)
        retq_ref[:, RET_WIDTH + c0:RET_WIDTH + c0 + RET_DIM] = (
            _rope(ret[:, RET_WIDTH + c0:RET_WIDTH + c0 + RET_DIM], cc, ss) * (RET_DIM ** -0.5))
    retq_ref[:, 2 * RET_WIDTH:] = ret[:, 2 * RET_WIDTH:]


def _const_spec(arr):
    nd = arr.ndim
    return pl.BlockSpec(arr.shape, lambda *_: (0,) * nd, pipeline_mode=pl.Buffered(1))


def _inproj_prompt(x, lead, cc, ss, rt_tab, gc_tab, n1, w_in, prm, nb):
    B, T, _ = x.shape
    tm = RET_CHUNK
    assert lead.shape[0] == tm and T % tm == 0 and B % nb == 0
    TP = tm + T
    n_w = len(w_in)
    consts = (lead, rt_tab, gc_tab, n1) + tuple(w_in) + tuple(prm)
    prow = lambda width: pl.BlockSpec((nb, tm, width), lambda b, t: (b, jnp.maximum(t - 1, 0), 0))
    in_specs = [prow(D_MODEL),
                pl.BlockSpec((tm, LANES), lambda b, t: (t, 0)),
                pl.BlockSpec((tm, LANES), lambda b, t: (t, 0))] + [_const_spec(c) for c in consts]
    out_shape = (
        jax.ShapeDtypeStruct((B, TP, 6 * RWKV_WIDTH), F32),
        jax.ShapeDtypeStruct((B, T, RWKV_WIDTH), F32),
        jax.ShapeDtypeStruct((B, T, RWKV_WIDTH), F32),
        jax.ShapeDtypeStruct((B, T, 2 * D_MODEL), F32),
        jax.ShapeDtypeStruct((B, T, RET_WIDTH), BF16),
        jax.ShapeDtypeStruct((B, 1, SHIFT_WIDTH), F32),
        jax.ShapeDtypeStruct((B, RET_HEADS, RET_DIM, RET_DIM), F32),
    )
    out_specs = (
        pl.BlockSpec((nb, tm, 6 * RWKV_WIDTH), lambda b, t: (b, t, 0)),
        prow(RWKV_WIDTH), prow(RWKV_WIDTH), prow(2 * D_MODEL), prow(RET_WIDTH),
        pl.BlockSpec((nb, 1, SHIFT_WIDTH), lambda b, t: (b, 0, 0)),
        pl.BlockSpec((nb, RET_HEADS, RET_DIM, RET_DIM), lambda b, t: (b, 0, 0, 0)),
    )
    return pl.pallas_call(
        functools.partial(_inproj_prompt_kernel, n_w),
        grid=(B // nb, TP // tm),
        in_specs=in_specs,
        out_specs=out_specs,
        out_shape=out_shape,
        scratch_shapes=[pltpu.VMEM((nb, tm + SUBLANES, SHIFT_WIDTH), F32),
                        pltpu.VMEM((nb, RET_HEADS, RET_DIM, RET_DIM), F32)],
        compiler_params=pltpu.CompilerParams(
            dimension_semantics=("arbitrary", "arbitrary"), vmem_limit_bytes=VMEM_LIMIT),
        name="inproj_prompt",
    )(x, cc, ss, *consts)


def _inproj_sample(xs, prev, cc, ss, n1, w_in, prm):
    N = xs.shape[0]
    n_w = len(w_in)
    args = (xs, prev, cc, ss, n1) + tuple(w_in) + tuple(prm)
    out_shape = (
        jax.ShapeDtypeStruct((N, 5 * RWKV_WIDTH), F32),
        jax.ShapeDtypeStruct((N, RWKV_WIDTH), F32),
        jax.ShapeDtypeStruct((N, RWKV_WIDTH), F32),
        jax.ShapeDtypeStruct((N, RWKV_WIDTH), F32),
        jax.ShapeDtypeStruct((N, 2 * D_MODEL), F32),
        jax.ShapeDtypeStruct((N, 4 * RET_WIDTH), F32),
        jax.ShapeDtypeStruct((N, SHIFT_WIDTH), F32),
    )
    return pl.pallas_call(
        functools.partial(_inproj_sample_kernel, n_w),
        out_shape=out_shape,
        compiler_params=pltpu.CompilerParams(vmem_limit_bytes=VMEM_LIMIT),
        name="inproj_sample",
    )(*args)


def _rwkv_prompt_kernel(pk_ref, y_ref, sout_ref, state):
    i = pl.program_id(0)
    HD = RWKV_HEAD_DIM
    VH = HD // 2

    @pl.when(i == 0)
    def _():
        state[...] = jnp.zeros(state.shape, F32)

    lo = lax.broadcasted_iota(jnp.int32, (VH, LANES), 1) < HD

    def step(t, carry):
        vt = pk_ref[t, PK_V]
        vt_sw = pltpu.roll(vt, HD, 1)
        ys = []
        for vh in range(2):
            rows = [slice(kl * HD + vh * VH, kl * HD + (vh + 1) * VH) for kl in range(K_LO)]
            sa = jnp.zeros((VH, LANES), F32)
            for kl in range(K_LO):
                sa = sa + state[rows[kl], :] * pk_ref[t, PK_NKK, kl:kl + 1, :]
            sa = sa + pltpu.roll(sa, HD, 1)
            vv = jnp.where(lo, vt, vt_sw) if vh == 0 else jnp.where(lo, vt_sw, vt)
            yacc = jnp.zeros((VH, LANES), F32)
            for kl in range(K_LO):
                new = (state[rows[kl], :] * pk_ref[t, PK_W, kl:kl + 1, :]
                       + sa * pk_ref[t, PK_BB, kl:kl + 1, :]
                       + vv * pk_ref[t, PK_K, kl:kl + 1, :])
                state[rows[kl], :] = new
                yacc = yacc + new * pk_ref[t, PK_R, kl:kl + 1, :]
            ys.append(yacc + pltpu.roll(yacc, HD, 1))
        y_ref[t] = jnp.where(lo, ys[0], ys[1])
        return carry

    lax.fori_loop(0, pk_ref.shape[0], step, 0)

    @pl.when(i == pl.num_programs(0) - 1)
    def _():
        sout_ref[...] = state[...]


def _rwkv_prompt(pk_t, tb):
    TP = pk_t.shape[0]
    HD = RWKV_HEAD_DIM
    assert (LEAD - N_META) % tb == 0 and LEAD % tb == 0
    skip = (LEAD - N_META) // tb
    meta_blocks = N_META // tb
    return pl.pallas_call(
        _rwkv_prompt_kernel,
        grid=(TP // tb - skip,),
        in_specs=[pl.BlockSpec((tb, 6, K_LO, LANES), lambda i: (i + skip, 0, 0, 0))],
        out_specs=(pl.BlockSpec((tb, K_LO, LANES), lambda i: (jnp.maximum(i - meta_blocks, 0), 0, 0)),
                   pl.BlockSpec((K_LO * HD, LANES), lambda i: (0, 0))),
        out_shape=(jax.ShapeDtypeStruct((TP - LEAD, K_LO, LANES), F32),
                   jax.ShapeDtypeStruct((K_LO * HD, LANES), F32)),
        scratch_shapes=[pltpu.VMEM((K_LO * HD, LANES), F32)],
        compiler_params=pltpu.CompilerParams(dimension_semantics=("arbitrary",)),
        name="rwkv_prompt",
    )(pk_t)


def _lane_col(tile, h):
    lane = lax.broadcasted_iota(jnp.int32, tile.shape, 1)
    return jnp.sum(jnp.where(lane == h, tile, 0.0), axis=-1, keepdims=True)


def _rwkv_sample_kernel(s_ref, pk_ref, vt_ref, sout_ref, yt_ref):
    for b in range(s_ref.shape[0]):
        v_tile = vt_ref[b]
        lane = lax.broadcasted_iota(jnp.int32, v_tile.shape, 1)
        y_tile = jnp.zeros(v_tile.shape, F32)
        for h in range(RWKV_HEADS):
            S = s_ref[b, h]
            sa = jnp.sum(S * pk_ref[b, PK_NKK, h:h + 1, :], axis=-1, keepdims=True)
            new = (S * pk_ref[b, PK_W, h:h + 1, :]
                   + sa * pk_ref[b, PK_BB, h:h + 1, :]
                   + _lane_col(v_tile, h) * pk_ref[b, PK_K, h:h + 1, :])
            sout_ref[b, h] = new
            y = jnp.sum(new * pk_ref[b, PK_R, h:h + 1, :], axis=-1, keepdims=True)
            y_tile = jnp.where(lane == h, y, y_tile)
        yt_ref[b] = y_tile


def _rwkv_sample(state, pk_h, v_t, bt):
    N = state.shape[0]
    H, HD = RWKV_HEADS, RWKV_HEAD_DIM
    return pl.pallas_call(
        _rwkv_sample_kernel,
        grid=(N // bt,),
        in_specs=[pl.BlockSpec((bt, H, HD, HD), lambda i: (i, 0, 0, 0)),
                  pl.BlockSpec((bt, 5, H, HD), lambda i: (i, 0, 0, 0)),
                  pl.BlockSpec((bt, HD, H), lambda i: (i, 0, 0))],
        out_specs=(pl.BlockSpec((bt, H, HD, HD), lambda i: (i, 0, 0, 0)),
                   pl.BlockSpec((bt, HD, H), lambda i: (i, 0, 0))),
        out_shape=(jax.ShapeDtypeStruct(state.shape, F32),
                   jax.ShapeDtypeStruct((N, HD, H), F32)),
        compiler_params=pltpu.CompilerParams(dimension_semantics=("arbitrary",)),
        name="rwkv_sample",
    )(state, pk_h, v_t)


def _ret_sample_kernel(s_ref, qt_ref, kt_ref, v_ref, g_ref, gam_ref, sout_ref, ob_ref):
    for b in range(s_ref.shape[0]):
        q_tile = qt_ref[b]
        k_tile = kt_ref[b]
        for h in range(RET_HEADS):
            S = s_ref[b, h]
            qc = _lane_col(q_tile, h)
            kc = _lane_col(k_tile, h)
            vrow = v_ref[b, h:h + 1, :]
            gam = gam_ref[h:h + 1, :]
            qk = jnp.sum(qc * kc, axis=0, keepdims=True)
            cross = jnp.sum(qc * S, axis=0, keepdims=True)
            o = qk * vrow + cross * gam
            sout_ref[b, h] = gam * S + kc * vrow
            o = o * lax.rsqrt(jnp.mean(o * o, axis=-1, keepdims=True) + RMS_EPS)
            gr = g_ref[b, h:h + 1, :]
            ob_ref[b, h:h + 1, :] = gr * _sigmoid(gr) * o


def _ret_sample(state, q_t, k_t, v_h, g_h, gam, bt):
    N = state.shape[0]
    H, DK = RET_HEADS, RET_DIM
    col = pl.BlockSpec((bt, DK, H), lambda i: (i, 0, 0))
    rowspec = pl.BlockSpec((bt, H, DK), lambda i: (i, 0, 0))
    st = pl.BlockSpec((bt, H, DK, DK), lambda i: (i, 0, 0, 0))
    return pl.pallas_call(
        _ret_sample_kernel,
        grid=(N // bt,),
        in_specs=[st, col, col, rowspec, rowspec, pl.BlockSpec((H, DK), lambda i: (0, 0))],
        out_specs=(st, rowspec),
        out_shape=(jax.ShapeDtypeStruct(state.shape, F32), jax.ShapeDtypeStruct((N, H, DK), F32)),
        compiler_params=pltpu.CompilerParams(dimension_semantics=("arbitrary",)),
        name="ret_sample",
    )(state, q_t, k_t, v_h, g_h, gam)


def _mixtail_kernel(n_w, x_ref, y_ref, g_ref, bonus_ref, ob_ref, gates_ref,
                    lnw_ref, lnb_ref, n2_ref, br_ref, *rest):
    wa = rest[0:n_w]
    wb = rest[n_w:2 * n_w]
    wo = rest[2 * n_w:3 * n_w]
    wr = rest[3 * n_w:3 * n_w + 2]
    x1_ref, u2_ref, comb_ref = rest[3 * n_w + 2:]

    y = y_ref[...]
    inv_n = 1.0 / RWKV_HEAD_DIM
    mu = _head64_sum(y) * inv_n
    d = y - mu
    var = _head64_sum(d * d) * inv_n
    yn = d * lax.rsqrt(var + LNX_EPS) * lnw_ref[...] + lnb_ref[...]
    ya = (yn + bonus_ref[...]) * g_ref[...]
    branch_a = _mm(ya, wa)
    branch_b = _mm(ob_ref[...], wb)
    gates = gates_ref[...]
    merged = _sigmoid(gates[:, :D_MODEL]) * branch_a + _sigmoid(gates[:, D_MODEL:]) * branch_b
    x1 = x_ref[...] + _mm(merged, wo)
    x1_ref[...] = x1
    u2 = _rmsnorm(x1, n2_ref[...])
    u2_ref[...] = u2.astype(u2_ref.dtype)

    logits = _mm(u2, wr) + br_ref[...]
    lane = lax.broadcasted_iota(jnp.int32, logits.shape, 1)
    lanef = lane.astype(F32)
    neg = jnp.float32(-jnp.inf)
    big = jnp.float32(1e9)
    is_grp = (lane >= N_EXPERTS) & (lane < N_EXPERTS + N_GROUPS)
    lg = jnp.where(is_grp, logits, neg)
    lg_max = jnp.max(lg, axis=-1, keepdims=True)
    p_grp = 1.0 / jnp.sum(jnp.exp(lg - lg_max), axis=-1, keepdims=True)
    grp = jnp.min(jnp.where(lg == lg_max, lanef, big), axis=-1, keepdims=True) - N_EXPERTS
    in_grp = (lanef >= grp * EXPERTS_PER_GROUP) & (lanef < (grp + 1.0) * EXPERTS_PER_GROUP)
    le = jnp.where(in_grp, logits, neg)
    v1 = jnp.max(le, axis=-1, keepdims=True)
    i1 = jnp.min(jnp.where(le == v1, lanef, big), axis=-1, keepdims=True)
    le2 = jnp.where(lanef == i1, neg, le)
    v2 = jnp.max(le2, axis=-1, keepdims=True)
    i2 = jnp.min(jnp.where(le2 == v2, lanef, big), axis=-1, keepdims=True)
    e2 = jnp.exp(v2 - v1)
    w1 = 1.0 / (1.0 + e2)
    w2 = e2 / (1.0 + e2)
    comb_ref[...] = p_grp * (jnp.where(lanef == i1, w1, 0.0) + jnp.where(lanef == i2, w2, 0.0))


def _mixtail(x, y, g, bonus, ob, gates, lnw, lnb, n2, br, wa, wb, wo, wr, tm):
    N = x.shape[0]
    n_w = len(wa)
    consts = (lnw, lnb, n2, br) + tuple(wa) + tuple(wb) + tuple(wo) + tuple(wr)
    row = lambda width: pl.BlockSpec((tm, width), lambda i: (i, 0))
    in_specs = [row(D_MODEL), row(RWKV_WIDTH), row(RWKV_WIDTH), row(RWKV_WIDTH), row(RET_WIDTH),
                row(2 * D_MODEL)] + [_const_spec(c) for c in consts]
    return pl.pallas_call(
        functools.partial(_mixtail_kernel, n_w),
        grid=(N // tm,),
        in_specs=in_specs,
        out_specs=(row(D_MODEL), row(D_MODEL), row(LANES)),
        out_shape=(jax.ShapeDtypeStruct((N, D_MODEL), F32),
                   jax.ShapeDtypeStruct((N, D_MODEL), BF16),
                   jax.ShapeDtypeStruct((N, LANES), F32)),
        compiler_params=pltpu.CompilerParams(dimension_semantics=("arbitrary",), vmem_limit_bytes=VMEM_LIMIT),
        name="mixtail",
    )(x, y, g, bonus, ob, gates, *consts)


def _moe_kernel(x1_ref, u2_ref, comb_ref, wg_ref, wu_ref, wd_ref, fg_ref, o_ref, acc):
    grp = pl.program_id(1)

    @pl.when(grp == 0)
    def _():
        acc[...] = jnp.zeros(acc.shape, F32)

    u2 = u2_ref[...]
    comb = comb_ref[...]
    lane = lax.broadcasted_iota(jnp.int32, comb.shape, 1)
    total = acc[...]
    for j in range(EXPERTS_PER_GROUP):
        c = jnp.sum(jnp.where(lane == grp * EXPERTS_PER_GROUP + j, comb, 0.0), axis=-1, keepdims=True)
        hg = jnp.dot(u2, wg_ref[j], preferred_element_type=F32)
        hu = jnp.dot(u2, wu_ref[j], preferred_element_type=F32)
        h = hg * _sigmoid(hg) * hu * c
        total = total + jnp.dot(h.astype(BF16), wd_ref[j], preferred_element_type=F32)
    acc[...] = total

    @pl.when(grp == pl.num_programs(1) - 1)
    def _():
        o_ref[...] = _rmsnorm(x1_ref[...] + acc[...], fg_ref[...])


def _moe(x1, u2, comb, wg, wu, wd, fg, tm):
    N = x1.shape[0]
    E = EXPERTS_PER_GROUP
    return pl.pallas_call(
        _moe_kernel,
        grid=(N // tm, N_GROUPS),
        in_specs=[pl.BlockSpec((tm, D_MODEL), lambda i, g: (i, 0)),
                  pl.BlockSpec((tm, D_MODEL), lambda i, g: (i, 0)),
                  pl.BlockSpec((tm, LANES), lambda i, g: (i, 0)),
                  pl.BlockSpec((E, D_MODEL, EXPERT_FF), lambda i, g: (g, 0, 0)),
                  pl.BlockSpec((E, D_MODEL, EXPERT_FF), lambda i, g: (g, 0, 0)),
                  pl.BlockSpec((E, EXPERT_FF, D_MODEL), lambda i, g: (g, 0, 0)),
                  pl.BlockSpec((1, D_MODEL), lambda i, g: (0, 0))],
        out_specs=pl.BlockSpec((tm, D_MODEL), lambda i, g: (i, 0)),
        out_shape=jax.ShapeDtypeStruct((N, D_MODEL), F32),
        scratch_shapes=[pltpu.VMEM((tm, D_MODEL), F32)],
        compiler_params=pltpu.CompilerParams(
            dimension_semantics=("arbitrary", "arbitrary"), vmem_limit_bytes=VMEM_LIMIT),
        name="moe",
    )(x1, u2, comb, wg, wu, wd, fg)


def _pieces(w, n):
    hi = w.astype(BF16)
    if n == 1:
        return (hi,)
    return (hi, (w - hi.astype(F32)).astype(BF16))


def _rope_tables(pos):
    half = RET_DIM // 2
    inv = ROPE_BASE ** (-jnp.arange(half, dtype=F32) / half)
    ang = pos.astype(F32)[:, None] * inv[None, :]
    c = jnp.cos(ang)
    s = jnp.sin(ang)
    cc = jnp.repeat(c, 2, axis=-1)
    ss = jnp.stack([-s, s], axis=-1).reshape(pos.shape[0], RET_DIM)
    return cc, ss


def _retention_tables():
    C = RET_CHUNK
    log_gamma = jnp.log(1.0 - 2.0 ** (-5.0 - jnp.arange(RET_HEADS, dtype=F32)))
    idx = jnp.arange(C, dtype=F32)
    diff = idx[:, None] - idx[None, :]
    lg = log_gamma[:, None, None]
    dmask = jnp.where(diff[None] >= 0, jnp.exp(jnp.maximum(diff, 0.0)[None] * lg), 0.0)
    q_dec = jnp.exp((idx[None, :] + 1.0) * log_gamma[:, None])
    k_dec = jnp.exp((C - 1.0 - idx[None, :]) * log_gamma[:, None])
    bc = lambda a: jnp.broadcast_to(a[:, :, None], (RET_HEADS, C, C))
    rt_tab = jnp.stack([dmask, bc(q_dec), bc(k_dec)], axis=1)
    gc_tab = jnp.broadcast_to(jnp.exp(C * log_gamma)[:, None, None], (RET_HEADS, SUBLANES, RET_DIM))
    gam1 = jnp.broadcast_to(jnp.exp(log_gamma)[:, None], (RET_HEADS, RET_DIM))
    return rt_tab, gc_tab, gam1


def kernel(x_prompt, x_sample, state_wkv, state_shift, state_ret, meta_tokens, norm1_g, w_in, shift_mu, decay_w0, decay_up, aaa_a0, aaa_up, gate_up, k_k, k_a, r_k, lnx_w, lnx_b, w_branch_a, w_branch_b, w_out, norm2_g, router_group_w, router_group_b, router_expert_w, router_expert_b, expert_w_gate, expert_w_up, expert_w_down, final_norm_g):
    B, T, D = x_prompt.shape
    NS = x_sample.shape[0]
    TP = LEAD + T
    H, HD = RWKV_HEADS, RWKV_HEAD_DIM
    l = 0
    row = lambda a: a.reshape(1, -1)

    zpad = jnp.zeros((LORA_PAD - 64, RWKV_WIDTH), F32)
    dup_pad = jnp.concatenate([decay_up[l], zpad], axis=0)
    aup_pad = jnp.concatenate([zpad, aaa_up[l]], axis=0)
    wr_full = jnp.concatenate([router_expert_w[l], router_group_w[l]], axis=1)
    wr_full = jnp.pad(wr_full, ((0, 0), (0, LANES - wr_full.shape[1])))
    br_full = jnp.pad(jnp.concatenate([router_expert_b[l], router_group_b[l]]), (0, LANES - N_EXPERTS - N_GROUPS))
    wr = _pieces(wr_full, 2)
    wg = expert_w_gate[l].astype(BF16)
    wu = expert_w_up[l].astype(BF16)
    wd = expert_w_down[l].astype(BF16)
    rt_tab, gc_tab, gam1 = _retention_tables()

    def prep_params(n):
        return (row(shift_mu[l]), row(decay_w0[l]), _pieces(dup_pad, n), row(aaa_a0[l]), _pieces(aup_pad, n),
                _pieces(gate_up[l], n), row(k_k[l]), row(k_a[l]), row(r_k[l]))

    def flat_params(p):
        out = []
        for a in p:
            out.extend(a if isinstance(a, tuple) else (a,))
        return tuple(out)

    def tail(n, x, y, g, bonus, ob, gates, tm_tail, tm_moe):
        x1, u2, comb = _mixtail(x, y, g, bonus, ob, gates, row(lnx_w[l]), row(lnx_b[l]), row(norm2_g[l]),
                                row(br_full), _pieces(w_branch_a[l], n), _pieces(w_branch_b[l], n),
                                _pieces(w_out[l], n), wr, tm_tail)
        return _moe(x1, u2, comb, wg, wu, wd, row(final_norm_g), tm_moe)

    lead = jnp.concatenate([jnp.zeros((LEAD - N_META, D), F32), meta_tokens.astype(F32)], axis=0)
    cc_p, ss_p = _rope_tables(jnp.arange(TP, dtype=jnp.int32) - (LEAD - N_META))
    pk, g, bonus, gates, ob, shift_p, ret_p = _inproj_prompt(
        x_prompt, lead, cc_p, ss_p, rt_tab, gc_tab, row(norm1_g[l]), _pieces(w_in[l], 1),
        flat_params(prep_params(1)), nb=2)

    pk_t = pk.reshape(B, TP, 6, H, 2, K_LO).transpose(1, 2, 5, 4, 0, 3).reshape(TP, 6, K_LO, LANES)
    y_t, s_fin = _rwkv_prompt(pk_t, tb=16)
    y = y_t.reshape(T, K_LO, 2, B, H).transpose(3, 0, 4, 2, 1).reshape(B * T, RWKV_WIDTH)
    wkv_p = s_fin.reshape(K_LO, HD, 2, B, H).transpose(3, 4, 1, 2, 0).reshape(B, H, HD, HD)

    flat = lambda a: a.reshape(B * T, a.shape[-1])
    y_prompt = tail(1, flat(x_prompt), y, flat(g), flat(bonus), flat(ob), flat(gates), 256, 1024).reshape(B, T, D)

    xs = x_sample.reshape(NS, D)
    cc_s, ss_s = _rope_tables(jnp.full((NS,), PAST_LEN, jnp.int32))
    pk_s, v_s, g_s, bonus_s, gates_s, retq, shift_s = _inproj_sample(
        xs, state_shift[l], cc_s, ss_s, row(norm1_g[l]), _pieces(w_in[l], 2), flat_params(prep_params(2)))
    wkv_s, yt_s = _rwkv_sample(state_wkv[l], pk_s.reshape(NS, 5, H, HD),
                               v_s.reshape(NS, H, HD).transpose(0, 2, 1), bt=8)
    y_s = yt_s.transpose(0, 2, 1).reshape(NS, RWKV_WIDTH)
    heads = lambda a: a.reshape(NS, RET_HEADS, RET_DIM)
    q_s, k_s, v_r, g_r = (retq[:, i * RET_WIDTH:(i + 1) * RET_WIDTH] for i in range(4))
    ret_s, ob_s = _ret_sample(state_ret[l], heads(q_s).transpose(0, 2, 1), heads(k_s).transpose(0, 2, 1),
                              heads(v_r), heads(g_r), gam1, bt=8)
    y_sample = tail(2, xs, y_s, g_s, bonus_s, ob_s.reshape(NS, RET_WIDTH), gates_s, NS, NS)

    return (y_prompt, y_sample.reshape(NS, 1, D),
            wkv_p[None], shift_p.reshape(1, B, SHIFT_WIDTH), ret_p[None],
            wkv_s[None], shift_s[None], ret_s[None])
```

```python
import functools

import jax
import jax.numpy as jnp
from jax import lax
from jax.experimental import pallas as pl
from jax.experimental.pallas import tpu as pltpu

F32 = jnp.float32
BF16 = jnp.bfloat16

D_MODEL = 1024
N_META = 16
RWKV_WIDTH = 512
RWKV_HEADS = 8
RWKV_HEAD_DIM = 64
LORA_PAD = 128
GATE_LORA = 128
SHIFT_WIDTH = 3 * RWKV_WIDTH + 64 + 64 + GATE_LORA
LNX_EPS = 64e-5
RET_WIDTH = 512
RET_HEADS = 4
RET_DIM = 128
RET_CHUNK = 128
ROPE_BASE = 10000.0
RET_COL0 = SHIFT_WIDTH
GATE_COL0 = SHIFT_WIDTH + 4 * RET_WIDTH
IN_WIDTH = GATE_COL0 + 2 * D_MODEL
N_GROUPS = 4
EXPERTS_PER_GROUP = 4
N_EXPERTS = 16
EXPERT_FF = 256
RMS_EPS = 1e-6
PAST_LEN = 16384

LANES = 128
SUBLANES = 8
VMEM_LIMIT = 56 * 1024 * 1024

LEAD = 128
PK_W, PK_NKK, PK_BB, PK_K, PK_R, PK_V = range(6)
K_LO = 32


def _split_bf16(a, nterms):
    pieces = []
    rem = a
    for _ in range(nterms - 1):
        c = rem * 65537.0
        hi = c - (c - rem)
        pieces.append(hi.astype(BF16))
        rem = rem - hi
    pieces.append(rem.astype(BF16))
    return pieces


def _mm(a, w_refs, cols=None):
    def w(i):
        r = w_refs[i]
        return r[...] if cols is None else r[:, cols[0]:cols[1]]

    if a.dtype == BF16:
        acc = jnp.dot(a, w(0), preferred_element_type=F32)
        if len(w_refs) > 1:
            acc = acc + jnp.dot(a, w(1), preferred_element_type=F32)
        return acc
    if len(w_refs) == 1:
        return jnp.dot(a.astype(BF16), w(0), preferred_element_type=F32)
    a_hi, a_lo = _split_bf16(a, 2)
    acc = jnp.dot(a_hi, w(0), preferred_element_type=F32)
    acc = acc + jnp.dot(a_lo, w(0), preferred_element_type=F32)
    acc = acc + jnp.dot(a_hi, w(1), preferred_element_type=F32)
    return acc


def _rmsnorm(x, g):
    return x * lax.rsqrt(jnp.mean(x * x, axis=-1, keepdims=True) + RMS_EPS) * g


def _sigmoid(x):
    return 1.0 / (1.0 + jnp.exp(-x))


def _head64_sum(x):
    lane = lax.broadcasted_iota(jnp.int32, (x.shape[0], LANES), 1)
    lo = lane < RWKV_HEAD_DIM
    outs = []
    for j in range(x.shape[1] // LANES):
        xj = x[:, j * LANES:(j + 1) * LANES]
        s_lo = jnp.sum(jnp.where(lo, xj, 0.0), axis=-1, keepdims=True)
        s_hi = jnp.sum(jnp.where(lo, 0.0, xj), axis=-1, keepdims=True)
        outs.append(jnp.where(lo, s_lo, s_hi))
    return jnp.concatenate(outs, axis=-1)


def _rope(x, cc, ss):
    lane = lax.broadcasted_iota(jnp.int32, x.shape, 1)
    even = (lane & 1) == 0
    swapped = jnp.where(even, pltpu.roll(x, LANES - 1, 1), pltpu.roll(x, 1, 1))
    return x * cc + swapped * ss


def _rwkv_prep(s, prev, prm):
    (mu, w0, dup, a0, aup, gup, k_k, k_a, r_k) = prm
    xs = s + mu[...] * (prev - s)
    W = RWKV_WIDTH
    r = xs[:, 0:W]
    k = xs[:, W:2 * W]
    v = xs[:, 2 * W:3 * W]
    wa = xs[:, 3 * W:3 * W + LORA_PAD]
    gd = xs[:, 3 * W + LORA_PAD:]
    dec = _mm(jnp.tanh(wa), dup)
    z = -(w0[...] + dec)
    softplus = jnp.maximum(z, 0.0) + jnp.log1p(jnp.exp(-jnp.abs(z)))
    decay = jnp.exp(-jnp.exp(-softplus - 0.5))
    a = _sigmoid(a0[...] + _mm(wa, aup))
    g = _mm(_sigmoid(gd), gup)
    kk = k * k_k[...]
    kk = kk / jnp.maximum(jnp.sqrt(_head64_sum(kk * kk)), 1e-12)
    k_mod = k * (1.0 + (a - 1.0) * k_a[...])
    bonus = _head64_sum(r * k_mod * r_k[...]) * v
    return r, decay, k_mod, v, kk, a, g, bonus


def _take_prep_params(refs, n_w):
    it = iter(refs)
    take = lambda n: tuple(next(it) for _ in range(n))
    (mu, w0), dup, (a0,), aup, gup, (k_k, k_a, r_k) = take(2), take(n_w), take(1), take(n_w), take(n_w), take(3)
    return (mu, w0, dup, a0, aup, gup, k_k, k_a, r_k), tuple(it)


def _store_rwkv_operands(ops, pk_ref, v_ref, g_ref, bonus_ref):
    r, decay, k_mod, v, kk, a, g, bonus = ops
    W = RWKV_WIDTH
    pk_ref[:, PK_W * W:(PK_W + 1) * W] = decay
    pk_ref[:, PK_NKK * W:(PK_NKK + 1) * W] = -kk
    pk_ref[:, PK_BB * W:(PK_BB + 1) * W] = kk * a
    pk_ref[:, PK_K * W:(PK_K + 1) * W] = k_mod
    pk_ref[:, PK_R * W:(PK_R + 1) * W] = r
    v_ref[...] = v
    g_ref[...] = g
    bonus_ref[...] = bonus


def _inproj_prompt_kernel(n_w, x_ref, cc_ref, ss_ref, lead_ref, rt_ref, gc_ref, n1_ref, *rest):
    w_in = rest[:n_w]
    prm, rest = _take_prep_params(rest[n_w:], n_w)
    (pk_ref, g_ref, bonus_ref, gates_ref, ob_ref, shift_ref, ret_ref, prev_buf, ret_state) = rest
    t = pl.program_id(1)
    nb, tm, _ = x_ref.shape
    rows = [slice(b * tm, (b + 1) * tm) for b in range(nb)]

    @pl.when(t == 0)
    def _():
        prev_buf[:, 0:SUBLANES, :] = jnp.zeros((nb, SUBLANES, SHIFT_WIDTH), F32)
        ret_state[...] = jnp.zeros(ret_state.shape, F32)

    x = jnp.where(t == 0, lead_ref[...][None], x_ref[...])
    u = _rmsnorm(x.reshape(nb * tm, D_MODEL), n1_ref[...])
    u_b = u.astype(BF16) if n_w == 1 else u

    s = _mm(u_b, w_in, (0, SHIFT_WIDTH))
    prevs = []
    for b in range(nb):
        s_b = s[rows[b]]
        prev_buf[b, SUBLANES:SUBLANES + tm, :] = s_b
        prevs.append(prev_buf[b, SUBLANES - 1:SUBLANES - 1 + tm, :])
        prev_buf[b, SUBLANES - 1:SUBLANES, :] = s_b[tm - 1:tm, :]
        shift_ref[b] = s_b[tm - 1:tm, :]
    prev = jnp.concatenate(prevs, axis=0)

    r, decay, k_mod, v, kk, a, g, bonus = _rwkv_prep(s, prev, prm)
    W = RWKV_WIDTH
    for b in range(nb):
        rb = rows[b]
        pk_ref[PK_W, b] = decay[rb].T
        pk_ref[PK_NKK, b] = (-kk)[rb].T
        pk_ref[PK_BB, b] = (kk * a)[rb].T
        pk_ref[PK_K, b] = k_mod[rb].T
        pk_ref[PK_R, b] = r[rb].T
        pk_ref[PK_V, b] = v[rb].T
        g_ref[b] = g[rb]
        bonus_ref[b] = bonus[rb]

    gates = _mm(u_b, w_in, (GATE_COL0, IN_WIDTH))
    for b in range(nb):
        gates_ref[b] = gates[rows[b]]

    ret = _mm(u_b, w_in, (RET_COL0, GATE_COL0))
    cc = cc_ref[...]
    ss = ss_ref[...]
    for b in range(nb):
        rb = rows[b]
        for h in range(RET_HEADS):
            c0 = h * RET_DIM
            qc = _rope(ret[rb, c0:c0 + RET_DIM], cc, ss).astype(BF16)
            kc = _rope(ret[rb, RET_WIDTH + c0:RET_WIDTH + c0 + RET_DIM], cc, ss) * (RET_DIM ** -0.5)
            vc = ret[rb, 2 * RET_WIDTH + c0:2 * RET_WIDTH + c0 + RET_DIM].astype(BF16)
            gr = ret[rb, 3 * RET_WIDTH + c0:3 * RET_WIDTH + c0 + RET_DIM]
            state = ret_state[b, h]
            scores = lax.dot_general(qc, kc.astype(BF16), (((1,), (1,)), ((), ())),
                                     preferred_element_type=F32) * rt_ref[h, 0]
            intra = jnp.dot(scores.astype(BF16), vc, preferred_element_type=F32)
            cross = jnp.dot(qc, state.astype(BF16), preferred_element_type=F32) * rt_ref[h, 1]
            kt = (kc * rt_ref[h, 2]).T.astype(BF16)
            ret_state[b, h] = gc_ref[h, 0:1, :] * state + jnp.dot(kt, vc, preferred_element_type=F32)
            o = intra + cross
            o = o * lax.rsqrt(jnp.mean(o * o, axis=-1, keepdims=True) + RMS_EPS)
            ob_ref[b, :, c0:c0 + RET_DIM] = (gr * _sigmoid(gr) * o).astype(ob_ref.dtype)

    @pl.when(t == pl.num_programs(1) - 1)
    def _():
        ret_ref[...] = ret_state[...]


def _inproj_sample_kernel(n_w, x_ref, prev_ref, cc_ref, ss_ref, n1_ref, *rest):
    w_in = rest[:n_w]
    prm, rest = _take_prep_params(rest[n_w:], n_w)
    (pk_ref, v_ref, g_ref, bonus_ref, gates_ref, retq_ref, shift_ref) = rest
    u = _rmsnorm(x_ref[...], n1_ref[...])
    s = _mm(u, w_in, (0, SHIFT_WIDTH))
    shift_ref[...] = s
    _store_rwkv_operands(_rwkv_prep(s, prev_ref[...], prm), pk_ref, v_ref, g_ref, bonus_ref)
    gates_ref[...] = _mm(u, w_in, (GATE_COL0, IN_WIDTH))
    ret = _mm(u, w_in, (RET_COL0, GATE_COL0))
    cc = cc_ref[...]
    ss = ss_ref[...]
    for h in range(RET_HEADS):
        c0 = h * RET_DIM
        retq_ref[:, c0:c0 + RET_DIM] = _rope(ret[:, c0:c0 + RET_DIM], cc, ss)
        retq_ref[:, RET_WIDTH + c0:RET_WIDTH + c0 + RET_DIM] = (
            _rope(ret[:, RET_WIDTH + c0:RET_WIDTH + c0 + RET_DIM], cc, ss) * (RET_DIM ** -0.5))
    retq_ref[:, 2 * RET_WIDTH:] = ret[:, 2 * RET_WIDTH:]


def _const_spec(arr):
    nd = arr.ndim
    return pl.BlockSpec(arr.shape, lambda *_: (0,) * nd, pipeline_mode=pl.Buffered(1))


def _inproj_prompt(x, lead, cc, ss, rt_tab, gc_tab, n1, w_in, prm, nb):
    B, T, _ = x.shape
    tm = RET_CHUNK
    assert lead.shape[0] == tm and T % tm == 0 and B % nb == 0
    TP = tm + T
    n_w = len(w_in)
    consts = (lead, rt_tab, gc_tab, n1) + tuple(w_in) + tuple(prm)
    prow = lambda width: pl.BlockSpec((nb, tm, width), lambda b, t: (b, jnp.maximum(t - 1, 0), 0))
    in_specs = [prow(D_MODEL),
                pl.BlockSpec((tm, LANES), lambda b, t: (t, 0)),
                pl.BlockSpec((tm, LANES), lambda b, t: (t, 0))] + [_const_spec(c) for c in consts]
    out_shape = (
        jax.ShapeDtypeStruct((6, B, RWKV_WIDTH, TP), F32),
        jax.ShapeDtypeStruct((B, T, RWKV_WIDTH), F32),
        jax.ShapeDtypeStruct((B, T, RWKV_WIDTH), F32),
        jax.ShapeDtypeStruct((B, T, 2 * D_MODEL), F32),
        jax.ShapeDtypeStruct((B, T, RET_WIDTH), BF16),
        jax.ShapeDtypeStruct((B, 1, SHIFT_WIDTH), F32),
        jax.ShapeDtypeStruct((B, RET_HEADS, RET_DIM, RET_DIM), F32),
    )
    out_specs = (
        pl.BlockSpec((6, nb, RWKV_WIDTH, tm), lambda b, t: (0, b, 0, t)),
        prow(RWKV_WIDTH), prow(RWKV_WIDTH), prow(2 * D_MODEL), prow(RET_WIDTH),
        pl.BlockSpec((nb, 1, SHIFT_WIDTH), lambda b, t: (b, 0, 0)),
        pl.BlockSpec((nb, RET_HEADS, RET_DIM, RET_DIM), lambda b, t: (b, 0, 0, 0)),
    )
    return pl.pallas_call(
        functools.partial(_inproj_prompt_kernel, n_w),
        grid=(B // nb, TP // tm),
        in_specs=in_specs,
        out_specs=out_specs,
        out_shape=out_shape,
        scratch_shapes=[pltpu.VMEM((nb, tm + SUBLANES, SHIFT_WIDTH), F32),
                        pltpu.VMEM((nb, RET_HEADS, RET_DIM, RET_DIM), F32)],
        compiler_params=pltpu.CompilerParams(
            dimension_semantics=("arbitrary", "arbitrary"), vmem_limit_bytes=VMEM_LIMIT),
        name="inproj_prompt",
    )(x, cc, ss, *consts)


def _inproj_sample(xs, prev, cc, ss, n1, w_in, prm):
    N = xs.shape[0]
    n_w = len(w_in)
    args = (xs, prev, cc, ss, n1) + tuple(w_in) + tuple(prm)
    out_shape = (
        jax.ShapeDtypeStruct((N, 5 * RWKV_WIDTH), F32),
        jax.ShapeDtypeStruct((N, RWKV_WIDTH), F32),
        jax.ShapeDtypeStruct((N, RWKV_WIDTH), F32),
        jax.ShapeDtypeStruct((N, RWKV_WIDTH), F32),
        jax.ShapeDtypeStruct((N, 2 * D_MODEL), F32),
        jax.ShapeDtypeStruct((N, 4 * RET_WIDTH), F32),
        jax.ShapeDtypeStruct((N, SHIFT_WIDTH), F32),
    )
    return pl.pallas_call(
        functools.partial(_inproj_sample_kernel, n_w),
        out_shape=out_shape,
        compiler_params=pltpu.CompilerParams(vmem_limit_bytes=VMEM_LIMIT),
        name="inproj_sample",
    )(*args)


def _rwkv_prompt_kernel(pk_ref, y_ref, sout_ref, state):
    i = pl.program_id(0)
    HD = RWKV_HEAD_DIM
    VH = HD // 2

    @pl.when(i == 0)
    def _():
        state[...] = jnp.zeros(state.shape, F32)

    even = (lax.broadcasted_iota(jnp.int32, (VH, LANES), 1) & 1) == 0

    def pair_sum(x):
        return x + jnp.where(even, pltpu.roll(x, LANES - 1, 1), pltpu.roll(x, 1, 1))

    def step(t, carry):
        vt = pk_ref[t, PK_V]
        ys = []
        for vh in range(2):
            rows = [slice(kl * HD + vh * VH, kl * HD + (vh + 1) * VH) for kl in range(K_LO)]
            sa = jnp.zeros((VH, LANES), F32)
            for kl in range(K_LO):
                sa = sa + state[rows[kl], :] * pk_ref[t, PK_NKK, kl:kl + 1, :]
            sa = pair_sum(sa)
            if vh == 0:
                vv = jnp.where(even, vt, pltpu.roll(vt, 1, 1))
            else:
                vv = jnp.where(even, pltpu.roll(vt, LANES - 1, 1), vt)
            yacc = jnp.zeros((VH, LANES), F32)
            for kl in range(K_LO):
                new = (state[rows[kl], :] * pk_ref[t, PK_W, kl:kl + 1, :]
                       + sa * pk_ref[t, PK_BB, kl:kl + 1, :]
                       + vv * pk_ref[t, PK_K, kl:kl + 1, :])
                state[rows[kl], :] = new
                yacc = yacc + new * pk_ref[t, PK_R, kl:kl + 1, :]
            ys.append(pair_sum(yacc))
        y_ref[t] = jnp.where(even, ys[0], ys[1])
        return carry

    lax.fori_loop(0, pk_ref.shape[0], step, 0)

    @pl.when(i == pl.num_programs(0) - 1)
    def _():
        sout_ref[...] = state[...]


def _rwkv_prompt(pk_t, tb):
    TP = pk_t.shape[0]
    HD = RWKV_HEAD_DIM
    assert (LEAD - N_META) % tb == 0 and LEAD % tb == 0
    skip = (LEAD - N_META) // tb
    meta_blocks = N_META // tb
    return pl.pallas_call(
        _rwkv_prompt_kernel,
        grid=(TP // tb - skip,),
        in_specs=[pl.BlockSpec((tb, 6, K_LO, LANES), lambda i: (i + skip, 0, 0, 0))],
        out_specs=(pl.BlockSpec((tb, K_LO, LANES), lambda i: (jnp.maximum(i - meta_blocks, 0), 0, 0)),
                   pl.BlockSpec((K_LO * HD, LANES), lambda i: (0, 0))),
        out_shape=(jax.ShapeDtypeStruct((TP - LEAD, K_LO, LANES), F32),
                   jax.ShapeDtypeStruct((K_LO * HD, LANES), F32)),
        scratch_shapes=[pltpu.VMEM((K_LO * HD, LANES), F32)],
        compiler_params=pltpu.CompilerParams(dimension_semantics=("arbitrary",)),
        name="rwkv_prompt",
    )(pk_t)


def _lane_col(tile, h):
    lane = lax.broadcasted_iota(jnp.int32, tile.shape, 1)
    return jnp.sum(jnp.where(lane == h, tile, 0.0), axis=-1, keepdims=True)


def _rwkv_sample_kernel(s_ref, pk_ref, vt_ref, sout_ref, yt_ref):
    for b in range(s_ref.shape[0]):
        v_tile = vt_ref[b]
        lane = lax.broadcasted_iota(jnp.int32, v_tile.shape, 1)
        y_tile = jnp.zeros(v_tile.shape, F32)
        for h in range(RWKV_HEADS):
            S = s_ref[b, h]
            sa = jnp.sum(S * pk_ref[b, PK_NKK, h:h + 1, :], axis=-1, keepdims=True)
            new = (S * pk_ref[b, PK_W, h:h + 1, :]
                   + sa * pk_ref[b, PK_BB, h:h + 1, :]
                   + _lane_col(v_tile, h) * pk_ref[b, PK_K, h:h + 1, :])
            sout_ref[b, h] = new
            y = jnp.sum(new * pk_ref[b, PK_R, h:h + 1, :], axis=-1, keepdims=True)
            y_tile = jnp.where(lane == h, y, y_tile)
        yt_ref[b] = y_tile


def _rwkv_sample(state, pk_h, v_t, bt):
    N = state.shape[0]
    H, HD = RWKV_HEADS, RWKV_HEAD_DIM
    return pl.pallas_call(
        _rwkv_sample_kernel,
        grid=(N // bt,),
        in_specs=[pl.BlockSpec((bt, H, HD, HD), lambda i: (i, 0, 0, 0)),
                  pl.BlockSpec((bt, 5, H, HD), lambda i: (i, 0, 0, 0)),
                  pl.BlockSpec((bt, HD, H), lambda i: (i, 0, 0))],
        out_specs=(pl.BlockSpec((bt, H, HD, HD), lambda i: (i, 0, 0, 0)),
                   pl.BlockSpec((bt, HD, H), lambda i: (i, 0, 0))),
        out_shape=(jax.ShapeDtypeStruct(state.shape, F32),
                   jax.ShapeDtypeStruct((N, HD, H), F32)),
        compiler_params=pltpu.CompilerParams(dimension_semantics=("arbitrary",)),
        name="rwkv_sample",
    )(state, pk_h, v_t)


def _ret_sample_kernel(s_ref, qt_ref, kt_ref, v_ref, g_ref, gam_ref, sout_ref, ob_ref):
    for b in range(s_ref.shape[0]):
        q_tile = qt_ref[b]
        k_tile = kt_ref[b]
        for h in range(RET_HEADS):
            S = s_ref[b, h]
            qc = _lane_col(q_tile, h)
            kc = _lane_col(k_tile, h)
            vrow = v_ref[b, h:h + 1, :]
            gam = gam_ref[h:h + 1, :]
            qk = jnp.sum(qc * kc, axis=0, keepdims=True)
            cross = jnp.sum(qc * S, axis=0, keepdims=True)
            o = qk * vrow + cross * gam
            sout_ref[b, h] = gam * S + kc * vrow
            o = o * lax.rsqrt(jnp.mean(o * o, axis=-1, keepdims=True) + RMS_EPS)
            gr = g_ref[b, h:h + 1, :]
            ob_ref[b, h:h + 1, :] = gr * _sigmoid(gr) * o


def _ret_sample(state, q_t, k_t, v_h, g_h, gam, bt):
    N = state.shape[0]
    H, DK = RET_HEADS, RET_DIM
    col = pl.BlockSpec((bt, DK, H), lambda i: (i, 0, 0))
    rowspec = pl.BlockSpec((bt, H, DK), lambda i: (i, 0, 0))
    st = pl.BlockSpec((bt, H, DK, DK), lambda i: (i, 0, 0, 0))
    return pl.pallas_call(
        _ret_sample_kernel,
        grid=(N // bt,),
        in_specs=[st, col, col, rowspec, rowspec, pl.BlockSpec((H, DK), lambda i: (0, 0))],
        out_specs=(st, rowspec),
        out_shape=(jax.ShapeDtypeStruct(state.shape, F32), jax.ShapeDtypeStruct((N, H, DK), F32)),
        compiler_params=pltpu.CompilerParams(dimension_semantics=("arbitrary",)),
        name="ret_sample",
    )(state, q_t, k_t, v_h, g_h, gam)


def _mixtail_kernel(n_w, y_feature_major, x_ref, y_ref, g_ref, bonus_ref, ob_ref, gates_ref,
                    lnw_ref, lnb_ref, n2_ref, br_ref, *rest):
    wa = rest[0:n_w]
    wb = rest[n_w:2 * n_w]
    wo = rest[2 * n_w:3 * n_w]
    wr = rest[3 * n_w:3 * n_w + 2]
    x1_ref, u2_ref, comb_ref = rest[3 * n_w + 2:]

    y = y_ref[...].T if y_feature_major else y_ref[...]
    inv_n = 1.0 / RWKV_HEAD_DIM
    mu = _head64_sum(y) * inv_n
    d = y - mu
    var = _head64_sum(d * d) * inv_n
    yn = d * lax.rsqrt(var + LNX_EPS) * lnw_ref[...] + lnb_ref[...]
    ya = (yn + bonus_ref[...]) * g_ref[...]
    branch_a = _mm(ya, wa)
    branch_b = _mm(ob_ref[...], wb)
    gates = gates_ref[...]
    merged = _sigmoid(gates[:, :D_MODEL]) * branch_a + _sigmoid(gates[:, D_MODEL:]) * branch_b
    x1 = x_ref[...] + _mm(merged, wo)
    x1_ref[...] = x1
    u2 = _rmsnorm(x1, n2_ref[...])
    u2_ref[...] = u2.astype(u2_ref.dtype)

    logits = _mm(u2, wr) + br_ref[...]
    lane = lax.broadcasted_iota(jnp.int32, logits.shape, 1)
    lanef = lane.astype(F32)
    neg = jnp.float32(-jnp.inf)
    big = jnp.float32(1e9)
    is_grp = (lane >= N_EXPERTS) & (lane < N_EXPERTS + N_GROUPS)
    lg = jnp.where(is_grp, logits, neg)
    lg_max = jnp.max(lg, axis=-1, keepdims=True)
    p_grp = 1.0 / jnp.sum(jnp.exp(lg - lg_max), axis=-1, keepdims=True)
    grp = jnp.min(jnp.where(lg == lg_max, lanef, big), axis=-1, keepdims=True) - N_EXPERTS
    in_grp = (lanef >= grp * EXPERTS_PER_GROUP) & (lanef < (grp + 1.0) * EXPERTS_PER_GROUP)
    le = jnp.where(in_grp, logits, neg)
    v1 = jnp.max(le, axis=-1, keepdims=True)
    i1 = jnp.min(jnp.where(le == v1, lanef, big), axis=-1, keepdims=True)
    le2 = jnp.where(lanef == i1, neg, le)
    v2 = jnp.max(le2, axis=-1, keepdims=True)
    i2 = jnp.min(jnp.where(le2 == v2, lanef, big), axis=-1, keepdims=True)
    e2 = jnp.exp(v2 - v1)
    w1 = 1.0 / (1.0 + e2)
    w2 = e2 / (1.0 + e2)
    comb_ref[...] = p_grp * (jnp.where(lanef == i1, w1, 0.0) + jnp.where(lanef == i2, w2, 0.0))


def _mixtail(x, y, g, bonus, ob, gates, lnw, lnb, n2, br, wa, wb, wo, wr, tm, y_feature_major):
    G, n, _ = x.shape
    assert n % tm == 0
    n_w = len(wa)
    consts = (lnw, lnb, n2, br) + tuple(wa) + tuple(wb) + tuple(wo) + tuple(wr)
    row = lambda width: pl.BlockSpec((None, tm, width), lambda b, i: (b, i, 0))
    y_spec = pl.BlockSpec((None, RWKV_WIDTH, tm), lambda b, i: (b, 0, i)) if y_feature_major else row(RWKV_WIDTH)
    in_specs = [row(D_MODEL), y_spec, row(RWKV_WIDTH), row(RWKV_WIDTH), row(RET_WIDTH),
                row(2 * D_MODEL)] + [_const_spec(c) for c in consts]
    return pl.pallas_call(
        functools.partial(_mixtail_kernel, n_w, y_feature_major),
        grid=(G, n // tm),
        in_specs=in_specs,
        out_specs=(row(D_MODEL), row(D_MODEL), row(LANES)),
        out_shape=(jax.ShapeDtypeStruct((G, n, D_MODEL), F32),
                   jax.ShapeDtypeStruct((G, n, D_MODEL), BF16),
                   jax.ShapeDtypeStruct((G, n, LANES), F32)),
        compiler_params=pltpu.CompilerParams(
            dimension_semantics=("arbitrary", "arbitrary"), vmem_limit_bytes=VMEM_LIMIT),
        name="mixtail",
    )(x, y, g, bonus, ob, gates, *consts)


def _moe_kernel(x1_ref, u2_ref, comb_ref, wg_ref, wu_ref, wd_ref, fg_ref, o_ref, acc):
    grp = pl.program_id(1)

    @pl.when(grp == 0)
    def _():
        acc[...] = jnp.zeros(acc.shape, F32)

    u2 = u2_ref[...]
    comb = comb_ref[...]
    lane = lax.broadcasted_iota(jnp.int32, comb.shape, 1)
    total = acc[...]
    for j in range(EXPERTS_PER_GROUP):
        c = jnp.sum(jnp.where(lane == grp * EXPERTS_PER_GROUP + j, comb, 0.0), axis=-1, keepdims=True)
        hg = jnp.dot(u2, wg_ref[j], preferred_element_type=F32)
        hu = jnp.dot(u2, wu_ref[j], preferred_element_type=F32)
        h = hg * _sigmoid(hg) * hu * c
        total = total + jnp.dot(h.astype(BF16), wd_ref[j], preferred_element_type=F32)
    acc[...] = total

    @pl.when(grp == pl.num_programs(1) - 1)
    def _():
        o_ref[...] = _rmsnorm(x1_ref[...] + acc[...], fg_ref[...])


def _moe(x1, u2, comb, wg, wu, wd, fg, tm):
    N = x1.shape[0]
    E = EXPERTS_PER_GROUP
    return pl.pallas_call(
        _moe_kernel,
        grid=(N // tm, N_GROUPS),
        in_specs=[pl.BlockSpec((tm, D_MODEL), lambda i, g: (i, 0)),
                  pl.BlockSpec((tm, D_MODEL), lambda i, g: (i, 0)),
                  pl.BlockSpec((tm, LANES), lambda i, g: (i, 0)),
                  pl.BlockSpec((E, D_MODEL, EXPERT_FF), lambda i, g: (g, 0, 0)),
                  pl.BlockSpec((E, D_MODEL, EXPERT_FF), lambda i, g: (g, 0, 0)),
                  pl.BlockSpec((E, EXPERT_FF, D_MODEL), lambda i, g: (g, 0, 0)),
                  pl.BlockSpec((1, D_MODEL), lambda i, g: (0, 0))],
        out_specs=pl.BlockSpec((tm, D_MODEL), lambda i, g: (i, 0)),
        out_shape=jax.ShapeDtypeStruct((N, D_MODEL), F32),
        scratch_shapes=[pltpu.VMEM((tm, D_MODEL), F32)],
        compiler_params=pltpu.CompilerParams(
            dimension_semantics=("arbitrary", "arbitrary"), vmem_limit_bytes=VMEM_LIMIT),
        name="moe",
    )(x1, u2, comb, wg, wu, wd, fg)


def _pieces(w, n):
    return tuple(_split_bf16(w.astype(F32), n))


def _rope_tables(pos):
    half = RET_DIM // 2
    inv = ROPE_BASE ** (-jnp.arange(half, dtype=F32) / half)
    ang = pos.astype(F32)[:, None] * inv[None, :]
    c = jnp.cos(ang)
    s = jnp.sin(ang)
    cc = jnp.repeat(c, 2, axis=-1)
    ss = jnp.stack([-s, s], axis=-1).reshape(pos.shape[0], RET_DIM)
    return cc, ss


def _retention_tables():
    C = RET_CHUNK
    log_gamma = jnp.log(1.0 - 2.0 ** (-5.0 - jnp.arange(RET_HEADS, dtype=F32)))
    idx = jnp.arange(C, dtype=F32)
    diff = idx[:, None] - idx[None, :]
    lg = log_gamma[:, None, None]
    dmask = jnp.where(diff[None] >= 0, jnp.exp(jnp.maximum(diff, 0.0)[None] * lg), 0.0)
    q_dec = jnp.exp((idx[None, :] + 1.0) * log_gamma[:, None])
    k_dec = jnp.exp((C - 1.0 - idx[None, :]) * log_gamma[:, None])
    bc = lambda a: jnp.broadcast_to(a[:, :, None], (RET_HEADS, C, C))
    rt_tab = jnp.stack([dmask, bc(q_dec), bc(k_dec)], axis=1)
    gc_tab = jnp.broadcast_to(jnp.exp(C * log_gamma)[:, None, None], (RET_HEADS, SUBLANES, RET_DIM))
    gam1 = jnp.broadcast_to(jnp.exp(log_gamma)[:, None], (RET_HEADS, RET_DIM))
    return rt_tab, gc_tab, gam1


def kernel(x_prompt, x_sample, state_wkv, state_shift, state_ret, meta_tokens, norm1_g, w_in, shift_mu, decay_w0, decay_up, aaa_a0, aaa_up, gate_up, k_k, k_a, r_k, lnx_w, lnx_b, w_branch_a, w_branch_b, w_out, norm2_g, router_group_w, router_group_b, router_expert_w, router_expert_b, expert_w_gate, expert_w_up, expert_w_down, final_norm_g):
    B, T, D = x_prompt.shape
    NS = x_sample.shape[0]
    TP = LEAD + T
    H, HD = RWKV_HEADS, RWKV_HEAD_DIM
    l = 0
    row = lambda a: a.reshape(1, -1)

    zpad = jnp.zeros((LORA_PAD - 64, RWKV_WIDTH), F32)
    dup_pad = jnp.concatenate([decay_up[l], zpad], axis=0)
    aup_pad = jnp.concatenate([zpad, aaa_up[l]], axis=0)
    wr_full = jnp.concatenate([router_expert_w[l], router_group_w[l]], axis=1)
    wr_full = jnp.pad(wr_full, ((0, 0), (0, LANES - wr_full.shape[1])))
    br_full = jnp.pad(jnp.concatenate([router_expert_b[l], router_group_b[l]]), (0, LANES - N_EXPERTS - N_GROUPS))
    wr = _pieces(wr_full, 2)
    wg = expert_w_gate[l].astype(BF16)
    wu = expert_w_up[l].astype(BF16)
    wd = expert_w_down[l].astype(BF16)
    rt_tab, gc_tab, gam1 = _retention_tables()

    def prep_params(n):
        return (row(shift_mu[l]), row(decay_w0[l]), _pieces(dup_pad, n), row(aaa_a0[l]), _pieces(aup_pad, n),
                _pieces(gate_up[l], n), row(k_k[l]), row(k_a[l]), row(r_k[l]))

    def flat_params(p):
        out = []
        for a in p:
            out.extend(a if isinstance(a, tuple) else (a,))
        return tuple(out)

    def tail(n, x, y, g, bonus, ob, gates, tm_tail, tm_moe, y_feature_major):
        x1, u2, comb = _mixtail(x, y, g, bonus, ob, gates, row(lnx_w[l]), row(lnx_b[l]), row(norm2_g[l]),
                                row(br_full), _pieces(w_branch_a[l], n), _pieces(w_branch_b[l], n),
                                _pieces(w_out[l], n), wr, tm_tail, y_feature_major)
        flat = lambda a: a.reshape(-1, a.shape[-1])
        return _moe(flat(x1), flat(u2), flat(comb), wg, wu, wd, row(final_norm_g), tm_moe)

    lead = jnp.concatenate([jnp.zeros((LEAD - N_META, D), F32), meta_tokens.astype(F32)], axis=0)
    cc_p, ss_p = _rope_tables(jnp.arange(TP, dtype=jnp.int32) - (LEAD - N_META))
    pk, g, bonus, gates, ob, shift_p, ret_p = _inproj_prompt(
        x_prompt, lead, cc_p, ss_p, rt_tab, gc_tab, row(norm1_g[l]), _pieces(w_in[l], 1),
        flat_params(prep_params(1)), nb=2)

    pk_t = pk.reshape(6, B * H * 2, K_LO, TP).transpose(3, 0, 2, 1)
    y_t, s_fin = _rwkv_prompt(pk_t, tb=16)
    y_fm = y_t.transpose(2, 1, 0).reshape(B, RWKV_WIDTH, T)
    wkv_p = s_fin.reshape(K_LO, HD, B, H, 2).transpose(2, 3, 1, 4, 0).reshape(B, H, HD, HD)

    y_prompt = tail(1, x_prompt, y_fm, g, bonus, ob, gates, 256, 1024, True).reshape(B, T, D)

    xs = x_sample.reshape(NS, D)
    cc_s, ss_s = _rope_tables(jnp.full((NS,), PAST_LEN, jnp.int32))
    pk_s, v_s, g_s, bonus_s, gates_s, retq, shift_s = _inproj_sample(
        xs, state_shift[l], cc_s, ss_s, row(norm1_g[l]), _pieces(w_in[l], 2), flat_params(prep_params(2)))
    wkv_s, yt_s = _rwkv_sample(state_wkv[l], pk_s.reshape(NS, 5, H, HD),
                               v_s.reshape(NS, H, HD).transpose(0, 2, 1), bt=8)
    y_s = yt_s.transpose(0, 2, 1).reshape(NS, RWKV_WIDTH)
    heads = lambda a: a.reshape(NS, RET_HEADS, RET_DIM)
    q_s, k_s, v_r, g_r = (retq[:, i * RET_WIDTH:(i + 1) * RET_WIDTH] for i in range(4))
    ret_s, ob_s = _ret_sample(state_ret[l], heads(q_s).transpose(0, 2, 1), heads(k_s).transpose(0, 2, 1),
                              heads(v_r), heads(g_r), gam1, bt=8)
    y_sample = tail(2, xs[None], y_s[None], g_s[None], bonus_s[None], ob_s.reshape(1, NS, RET_WIDTH), gates_s[None],
                    NS, NS, False)

    return (y_prompt, y_sample.reshape(NS, 1, D),
            wkv_p[None], shift_p.reshape(1, B, SHIFT_WIDTH), ret_p[None],
            wkv_s[None], shift_s[None], ret_s[None])
```

```python
import functools

import jax
import jax.numpy as jnp
from jax import lax
from jax.experimental import pallas as pl
from jax.experimental.pallas import tpu as pltpu

F32 = jnp.float32
BF16 = jnp.bfloat16

D_MODEL = 1024
N_META = 16
RWKV_WIDTH = 512
RWKV_HEADS = 8
RWKV_HEAD_DIM = 64
LORA_PAD = 128
GATE_LORA = 128
SHIFT_WIDTH = 3 * RWKV_WIDTH + 64 + 64 + GATE_LORA
LNX_EPS = 64e-5
RET_WIDTH = 512
RET_HEADS = 4
RET_DIM = 128
RET_CHUNK = 128
ROPE_BASE = 10000.0
RET_COL0 = SHIFT_WIDTH
GATE_COL0 = SHIFT_WIDTH + 4 * RET_WIDTH
IN_WIDTH = GATE_COL0 + 2 * D_MODEL
N_GROUPS = 4
EXPERTS_PER_GROUP = 4
N_EXPERTS = 16
EXPERT_FF = 256
RMS_EPS = 1e-6
PAST_LEN = 16384

LANES = 128
SUBLANES = 8
VMEM_LIMIT = 56 * 1024 * 1024

LEAD = 128
PK_W, PK_NKK, PK_BB, PK_K, PK_R, PK_V = range(6)
K_LO = 32
KL_UNROLL = 8


def _split_bf16(a, nterms):
    pieces = []
    rem = a
    for _ in range(nterms - 1):
        c = rem * 65537.0
        hi = c - (c - rem)
        pieces.append(hi.astype(BF16))
        rem = rem - hi
    pieces.append(rem.astype(BF16))
    return pieces


def _mm(a, w_refs, cols=None):
    def w(i):
        r = w_refs[i]
        return r[...] if cols is None else r[:, cols[0]:cols[1]]

    if a.dtype == BF16:
        acc = jnp.dot(a, w(0), preferred_element_type=F32)
        if len(w_refs) > 1:
            acc = acc + jnp.dot(a, w(1), preferred_element_type=F32)
        return acc
    if len(w_refs) == 1:
        return jnp.dot(a.astype(BF16), w(0), preferred_element_type=F32)
    a_hi, a_lo = _split_bf16(a, 2)
    acc = jnp.dot(a_hi, w(0), preferred_element_type=F32)
    acc = acc + jnp.dot(a_lo, w(0), preferred_element_type=F32)
    acc = acc + jnp.dot(a_hi, w(1), preferred_element_type=F32)
    return acc


def _rmsnorm(x, g):
    return x * lax.rsqrt(jnp.mean(x * x, axis=-1, keepdims=True) + RMS_EPS) * g


def _sigmoid(x):
    return 1.0 / (1.0 + jnp.exp(-x))


def _head64_sum(x):
    lane = lax.broadcasted_iota(jnp.int32, (x.shape[0], LANES), 1)
    lo = lane < RWKV_HEAD_DIM
    outs = []
    for j in range(x.shape[1] // LANES):
        xj = x[:, j * LANES:(j + 1) * LANES]
        s_lo = jnp.sum(jnp.where(lo, xj, 0.0), axis=-1, keepdims=True)
        s_hi = jnp.sum(jnp.where(lo, 0.0, xj), axis=-1, keepdims=True)
        outs.append(jnp.where(lo, s_lo, s_hi))
    return jnp.concatenate(outs, axis=-1)


def _rope(x, cc, ss):
    lane = lax.broadcasted_iota(jnp.int32, x.shape, 1)
    even = (lane & 1) == 0
    swapped = jnp.where(even, pltpu.roll(x, LANES - 1, 1), pltpu.roll(x, 1, 1))
    return x * cc + swapped * ss


def _rwkv_prep(s, prev, prm):
    (mu, w0, dup, a0, aup, gup, k_k, k_a, r_k) = prm
    xs = s + mu[...] * (prev - s)
    W = RWKV_WIDTH
    r = xs[:, 0:W]
    k = xs[:, W:2 * W]
    v = xs[:, 2 * W:3 * W]
    wa = xs[:, 3 * W:3 * W + LORA_PAD]
    gd = xs[:, 3 * W + LORA_PAD:]
    dec = _mm(jnp.tanh(wa), dup)
    z = -(w0[...] + dec)
    softplus = jnp.maximum(z, 0.0) + jnp.log1p(jnp.exp(-jnp.abs(z)))
    decay = jnp.exp(-jnp.exp(-softplus - 0.5))
    a = _sigmoid(a0[...] + _mm(wa, aup))
    g = _mm(_sigmoid(gd), gup)
    kk = k * k_k[...]
    kk = kk / jnp.maximum(jnp.sqrt(_head64_sum(kk * kk)), 1e-12)
    k_mod = k * (1.0 + (a - 1.0) * k_a[...])
    bonus = _head64_sum(r * k_mod * r_k[...]) * v
    return r, decay, k_mod, v, kk, a, g, bonus


def _take_prep_params(refs, n_w):
    it = iter(refs)
    take = lambda n: tuple(next(it) for _ in range(n))
    (mu, w0), dup, (a0,), aup, gup, (k_k, k_a, r_k) = take(2), take(n_w), take(1), take(n_w), take(n_w), take(3)
    return (mu, w0, dup, a0, aup, gup, k_k, k_a, r_k), tuple(it)


def _store_rwkv_operands(ops, pk_ref, v_ref, g_ref, bonus_ref):
    r, decay, k_mod, v, kk, a, g, bonus = ops
    W = RWKV_WIDTH
    pk_ref[:, PK_W * W:(PK_W + 1) * W] = decay
    pk_ref[:, PK_NKK * W:(PK_NKK + 1) * W] = -kk
    pk_ref[:, PK_BB * W:(PK_BB + 1) * W] = kk * a
    pk_ref[:, PK_K * W:(PK_K + 1) * W] = k_mod
    pk_ref[:, PK_R * W:(PK_R + 1) * W] = r
    v_ref[...] = v
    g_ref[...] = g
    bonus_ref[...] = bonus


def _inproj_prompt_kernel(n_w, x_ref, cc_ref, ss_ref, lead_ref, rt_ref, gc_ref, n1_ref, *rest):
    w_in = rest[:n_w]
    prm, rest = _take_prep_params(rest[n_w:], n_w)
    (pk_ref, g_ref, bonus_ref, gates_ref, ob_ref, shift_ref, ret_ref, prev_buf, ret_state) = rest
    t = pl.program_id(1)
    nb, tm, _ = x_ref.shape
    rows = [slice(b * tm, (b + 1) * tm) for b in range(nb)]

    @pl.when(t == 0)
    def _():
        prev_buf[:, 0:SUBLANES, :] = jnp.zeros((nb, SUBLANES, SHIFT_WIDTH), F32)
        ret_state[...] = jnp.zeros(ret_state.shape, F32)

    x = jnp.where(t == 0, lead_ref[...][None], x_ref[...])
    u = _rmsnorm(x.reshape(nb * tm, D_MODEL), n1_ref[...])
    u_b = u.astype(BF16) if n_w == 1 else u

    s = _mm(u_b, w_in, (0, SHIFT_WIDTH))
    prevs = []
    for b in range(nb):
        s_b = s[rows[b]]
        prev_buf[b, SUBLANES:SUBLANES + tm, :] = s_b
        prevs.append(prev_buf[b, SUBLANES - 1:SUBLANES - 1 + tm, :])
        prev_buf[b, SUBLANES - 1:SUBLANES, :] = s_b[tm - 1:tm, :]
        shift_ref[b] = s_b[tm - 1:tm, :]
    prev = jnp.concatenate(prevs, axis=0)

    r, decay, k_mod, v, kk, a, g, bonus = _rwkv_prep(s, prev, prm)
    W = RWKV_WIDTH
    for b in range(nb):
        rb = rows[b]
        for q, val in ((PK_W, decay), (PK_NKK, -kk), (PK_BB, kk * a), (PK_K, k_mod), (PK_R, r), (PK_V, v)):
            fm = val[rb].T
            for hi in range(2):
                pk_ref[q, hi, b] = jnp.concatenate(
                    [fm[h * RWKV_HEAD_DIM + hi * K_LO:h * RWKV_HEAD_DIM + (hi + 1) * K_LO] for h in range(RWKV_HEADS)],
                    axis=0)
        g_ref[b] = g[rb]
        bonus_ref[b] = bonus[rb]

    gates = _mm(u_b, w_in, (GATE_COL0, IN_WIDTH))
    for b in range(nb):
        gates_ref[b] = gates[rows[b]]

    ret = _mm(u_b, w_in, (RET_COL0, GATE_COL0))
    cc = cc_ref[...]
    ss = ss_ref[...]
    for b in range(nb):
        rb = rows[b]
        for h in range(RET_HEADS):
            c0 = h * RET_DIM
            qc = _rope(ret[rb, c0:c0 + RET_DIM], cc, ss).astype(BF16)
            kc = _rope(ret[rb, RET_WIDTH + c0:RET_WIDTH + c0 + RET_DIM], cc, ss) * (RET_DIM ** -0.5)
            vc = ret[rb, 2 * RET_WIDTH + c0:2 * RET_WIDTH + c0 + RET_DIM].astype(BF16)
            gr = ret[rb, 3 * RET_WIDTH + c0:3 * RET_WIDTH + c0 + RET_DIM]
            state = ret_state[b, h]
            scores = lax.dot_general(qc, kc.astype(BF16), (((1,), (1,)), ((), ())),
                                     preferred_element_type=F32) * rt_ref[h, 0]
            intra = jnp.dot(scores.astype(BF16), vc, preferred_element_type=F32)
            cross = jnp.dot(qc, state.astype(BF16), preferred_element_type=F32) * rt_ref[h, 1]
            kt = (kc * rt_ref[h, 2]).T.astype(BF16)
            ret_state[b, h] = gc_ref[h, 0:1, :] * state + jnp.dot(kt, vc, preferred_element_type=F32)
            o = intra + cross
            o = o * lax.rsqrt(jnp.mean(o * o, axis=-1, keepdims=True) + RMS_EPS)
            ob_ref[b, :, c0:c0 + RET_DIM] = (gr * _sigmoid(gr) * o).astype(ob_ref.dtype)

    @pl.when(t == pl.num_programs(1) - 1)
    def _():
        ret_ref[...] = ret_state[...]


def _inproj_sample_kernel(n_w, x_ref, prev_ref, cc_ref, ss_ref, n1_ref, *rest):
    w_in = rest[:n_w]
    prm, rest = _take_prep_params(rest[n_w:], n_w)
    (pk_ref, v_ref, g_ref, bonus_ref, gates_ref, retq_ref, shift_ref) = rest
    u = _rmsnorm(x_ref[...], n1_ref[...])
    s = _mm(u, w_in, (0, SHIFT_WIDTH))
    shift_ref[...] = s
    _store_rwkv_operands(_rwkv_prep(s, prev_ref[...], prm), pk_ref, v_ref, g_ref, bonus_ref)
    gates_ref[...] = _mm(u, w_in, (GATE_COL0, IN_WIDTH))
    ret = _mm(u, w_in, (RET_COL0, GATE_COL0))
    cc = cc_ref[...]
    ss = ss_ref[...]
    for h in range(RET_HEADS):
        c0 = h * RET_DIM
        retq_ref[:, c0:c0 + RET_DIM] = _rope(ret[:, c0:c0 + RET_DIM], cc, ss)
        retq_ref[:, RET_WIDTH + c0:RET_WIDTH + c0 + RET_DIM] = (
            _rope(ret[:, RET_WIDTH + c0:RET_WIDTH + c0 + RET_DIM], cc, ss) * (RET_DIM ** -0.5))
    retq_ref[:, 2 * RET_WIDTH:] = ret[:, 2 * RET_WIDTH:]


def _const_spec(arr):
    nd = arr.ndim
    return pl.BlockSpec(arr.shape, lambda *_: (0,) * nd, pipeline_mode=pl.Buffered(1))


def _inproj_prompt(x, lead, cc, ss, rt_tab, gc_tab, n1, w_in, prm, nb):
    B, T, _ = x.shape
    tm = RET_CHUNK
    assert lead.shape[0] == tm and T % tm == 0 and B % nb == 0
    TP = tm + T
    n_w = len(w_in)
    consts = (lead, rt_tab, gc_tab, n1) + tuple(w_in) + tuple(prm)
    prow = lambda width: pl.BlockSpec((nb, tm, width), lambda b, t: (b, jnp.maximum(t - 1, 0), 0))
    in_specs = [prow(D_MODEL),
                pl.BlockSpec((tm, LANES), lambda b, t: (t, 0)),
                pl.BlockSpec((tm, LANES), lambda b, t: (t, 0))] + [_const_spec(c) for c in consts]
    out_shape = (
        jax.ShapeDtypeStruct((6, 2, B, RWKV_WIDTH // 2, TP), F32),
        jax.ShapeDtypeStruct((B, T, RWKV_WIDTH), F32),
        jax.ShapeDtypeStruct((B, T, RWKV_WIDTH), F32),
        jax.ShapeDtypeStruct((B, T, 2 * D_MODEL), F32),
        jax.ShapeDtypeStruct((B, T, RET_WIDTH), BF16),
        jax.ShapeDtypeStruct((B, 1, SHIFT_WIDTH), F32),
        jax.ShapeDtypeStruct((B, RET_HEADS, RET_DIM, RET_DIM), F32),
    )
    out_specs = (
        pl.BlockSpec((6, 2, nb, RWKV_WIDTH // 2, tm), lambda b, t: (0, 0, b, 0, t)),
        prow(RWKV_WIDTH), prow(RWKV_WIDTH), prow(2 * D_MODEL), prow(RET_WIDTH),
        pl.BlockSpec((nb, 1, SHIFT_WIDTH), lambda b, t: (b, 0, 0)),
        pl.BlockSpec((nb, RET_HEADS, RET_DIM, RET_DIM), lambda b, t: (b, 0, 0, 0)),
    )
    return pl.pallas_call(
        functools.partial(_inproj_prompt_kernel, n_w),
        grid=(B // nb, TP // tm),
        in_specs=in_specs,
        out_specs=out_specs,
        out_shape=out_shape,
        scratch_shapes=[pltpu.VMEM((nb, tm + SUBLANES, SHIFT_WIDTH), F32),
                        pltpu.VMEM((nb, RET_HEADS, RET_DIM, RET_DIM), F32)],
        compiler_params=pltpu.CompilerParams(
            dimension_semantics=("arbitrary", "arbitrary"), vmem_limit_bytes=VMEM_LIMIT),
        name="inproj_prompt",
    )(x, cc, ss, *consts)


def _inproj_sample(xs, prev, cc, ss, n1, w_in, prm):
    N = xs.shape[0]
    n_w = len(w_in)
    args = (xs, prev, cc, ss, n1) + tuple(w_in) + tuple(prm)
    out_shape = (
        jax.ShapeDtypeStruct((N, 5 * RWKV_WIDTH), F32),
        jax.ShapeDtypeStruct((N, RWKV_WIDTH), F32),
        jax.ShapeDtypeStruct((N, RWKV_WIDTH), F32),
        jax.ShapeDtypeStruct((N, RWKV_WIDTH), F32),
        jax.ShapeDtypeStruct((N, 2 * D_MODEL), F32),
        jax.ShapeDtypeStruct((N, 4 * RET_WIDTH), F32),
        jax.ShapeDtypeStruct((N, SHIFT_WIDTH), F32),
    )
    return pl.pallas_call(
        functools.partial(_inproj_sample_kernel, n_w),
        out_shape=out_shape,
        compiler_params=pltpu.CompilerParams(vmem_limit_bytes=VMEM_LIMIT),
        name="inproj_sample",
    )(*args)


def _rwkv_prompt_kernel(pk_ref, nxt_ref, y_ref, sout_ref, state, sa_buf, kd):
    i = pl.program_id(0)
    tb = pk_ref.shape[0]
    HD = RWKV_HEAD_DIM
    VL = K_LO

    @pl.when(i == 0)
    def _():
        state[...] = jnp.zeros(state.shape, F32)
        sa_buf[...] = jnp.zeros(sa_buf.shape, F32)

    first_half = lax.broadcasted_iota(jnp.int32, (K_LO, LANES), 1) < LANES // 2

    def spread(x, t, q):
        swapped = pltpu.roll(x, LANES // 2, 1)
        kd[t, q, 0:K_LO, :] = jnp.where(first_half, x, swapped)
        kd[t, q, K_LO:HD, :] = jnp.where(first_half, swapped, x)

    for t in range(tb):
        for q in (PK_W, PK_NKK, PK_BB, PK_K, PK_R):
            spread(pk_ref[t, q], t, q)
    spread(nxt_ref[0, 0], tb, PK_NKK)

    def step(t, sa):
        vv = pk_ref[t, PK_V]

        def key_rows(c, accs):
            yacc, sacc = accs
            for j in range(KL_UNROLL):
                k = c * KL_UNROLL + j
                row = lambda tt, q: kd[tt, q, pl.ds(k, 1), :]
                rows = pl.ds(pl.multiple_of(k * VL, VL), VL)
                new = state[rows, :] * row(t, PK_W) + sa * row(t, PK_BB) + vv * row(t, PK_K)
                state[rows, :] = new
                yacc = yacc + new * row(t, PK_R)
                sacc = sacc + new * row(t + 1, PK_NKK)
            return yacc, sacc

        zero = jnp.zeros((VL, LANES), F32)
        yacc, sacc = lax.fori_loop(0, HD // KL_UNROLL, key_rows, (zero, zero))
        y_ref[t] = yacc
        return sacc

    sa_buf[...] = lax.fori_loop(0, tb, step, sa_buf[...])

    @pl.when(i == pl.num_programs(0) - 1)
    def _():
        sout_ref[...] = state[...]


def _rwkv_prompt(pk_t, tb):
    TP = pk_t.shape[0]
    HD = RWKV_HEAD_DIM
    assert (LEAD - N_META) % tb == 0 and LEAD % tb == 0
    skip = (LEAD - N_META) // tb
    meta_blocks = N_META // tb
    return pl.pallas_call(
        _rwkv_prompt_kernel,
        grid=(TP // tb - skip,),
        in_specs=[pl.BlockSpec((tb, 6, K_LO, LANES), lambda i: (i + skip, 0, 0, 0)),
                  pl.BlockSpec((1, 1, K_LO, LANES),
                               lambda i: (jnp.minimum((i + skip + 1) * tb, TP - 1), PK_NKK, 0, 0))],
        out_specs=(pl.BlockSpec((tb, K_LO, LANES), lambda i: (jnp.maximum(i - meta_blocks, 0), 0, 0)),
                   pl.BlockSpec((K_LO * HD, LANES), lambda i: (0, 0))),
        out_shape=(jax.ShapeDtypeStruct((TP - LEAD, K_LO, LANES), F32),
                   jax.ShapeDtypeStruct((K_LO * HD, LANES), F32)),
        scratch_shapes=[pltpu.VMEM((HD * K_LO, LANES), F32), pltpu.VMEM((K_LO, LANES), F32),
                        pltpu.VMEM((tb + 1, 5, HD, LANES), F32)],
        compiler_params=pltpu.CompilerParams(dimension_semantics=("arbitrary",)),
        name="rwkv_prompt",
    )(pk_t, pk_t)


def _lane_col(tile, h):
    lane = lax.broadcasted_iota(jnp.int32, tile.shape, 1)
    return jnp.sum(jnp.where(lane == h, tile, 0.0), axis=-1, keepdims=True)


def _rwkv_sample_kernel(s_ref, pk_ref, vt_ref, sout_ref, yt_ref):
    for b in range(s_ref.shape[0]):
        v_tile = vt_ref[b]
        lane = lax.broadcasted_iota(jnp.int32, v_tile.shape, 1)
        y_tile = jnp.zeros(v_tile.shape, F32)
        for h in range(RWKV_HEADS):
            S = s_ref[b, h]
            sa = jnp.sum(S * pk_ref[b, PK_NKK, h:h + 1, :], axis=-1, keepdims=True)
            new = (S * pk_ref[b, PK_W, h:h + 1, :]
                   + sa * pk_ref[b, PK_BB, h:h + 1, :]
                   + _lane_col(v_tile, h) * pk_ref[b, PK_K, h:h + 1, :])
            sout_ref[b, h] = new
            y = jnp.sum(new * pk_ref[b, PK_R, h:h + 1, :], axis=-1, keepdims=True)
            y_tile = jnp.where(lane == h, y, y_tile)
        yt_ref[b] = y_tile


def _rwkv_sample(state, pk_h, v_t, bt):
    N = state.shape[0]
    H, HD = RWKV_HEADS, RWKV_HEAD_DIM
    return pl.pallas_call(
        _rwkv_sample_kernel,
        grid=(N // bt,),
        in_specs=[pl.BlockSpec((bt, H, HD, HD), lambda i: (i, 0, 0, 0)),
                  pl.BlockSpec((bt, 5, H, HD), lambda i: (i, 0, 0, 0)),
                  pl.BlockSpec((bt, HD, H), lambda i: (i, 0, 0))],
        out_specs=(pl.BlockSpec((bt, H, HD, HD), lambda i: (i, 0, 0, 0)),
                   pl.BlockSpec((bt, HD, H), lambda i: (i, 0, 0))),
        out_shape=(jax.ShapeDtypeStruct(state.shape, F32),
                   jax.ShapeDtypeStruct((N, HD, H), F32)),
        compiler_params=pltpu.CompilerParams(dimension_semantics=("arbitrary",)),
        name="rwkv_sample",
    )(state, pk_h, v_t)


def _ret_sample_kernel(s_ref, qt_ref, kt_ref, v_ref, g_ref, gam_ref, sout_ref, ob_ref):
    for b in range(s_ref.shape[0]):
        q_tile = qt_ref[b]
        k_tile = kt_ref[b]
        for h in range(RET_HEADS):
            S = s_ref[b, h]
            qc = _lane_col(q_tile, h)
            kc = _lane_col(k_tile, h)
            vrow = v_ref[b, h:h + 1, :]
            gam = gam_ref[h:h + 1, :]
            qk = jnp.sum(qc * kc, axis=0, keepdims=True)
            cross = jnp.sum(qc * S, axis=0, keepdims=True)
            o = qk * vrow + cross * gam
            sout_ref[b, h] = gam * S + kc * vrow
            o = o * lax.rsqrt(jnp.mean(o * o, axis=-1, keepdims=True) + RMS_EPS)
            gr = g_ref[b, h:h + 1, :]
            ob_ref[b, h:h + 1, :] = gr * _sigmoid(gr) * o


def _ret_sample(state, q_t, k_t, v_h, g_h, gam, bt):
    N = state.shape[0]
    H, DK = RET_HEADS, RET_DIM
    col = pl.BlockSpec((bt, DK, H), lambda i: (i, 0, 0))
    rowspec = pl.BlockSpec((bt, H, DK), lambda i: (i, 0, 0))
    st = pl.BlockSpec((bt, H, DK, DK), lambda i: (i, 0, 0, 0))
    return pl.pallas_call(
        _ret_sample_kernel,
        grid=(N // bt,),
        in_specs=[st, col, col, rowspec, rowspec, pl.BlockSpec((H, DK), lambda i: (0, 0))],
        out_specs=(st, rowspec),
        out_shape=(jax.ShapeDtypeStruct(state.shape, F32), jax.ShapeDtypeStruct((N, H, DK), F32)),
        compiler_params=pltpu.CompilerParams(dimension_semantics=("arbitrary",)),
        name="ret_sample",
    )(state, q_t, k_t, v_h, g_h, gam)


def _mixtail_kernel(n_w, y_feature_major, x_ref, y_ref, g_ref, bonus_ref, ob_ref, gates_ref,
                    lnw_ref, lnb_ref, n2_ref, br_ref, *rest):
    wa = rest[0:n_w]
    wb = rest[n_w:2 * n_w]
    wo = rest[2 * n_w:3 * n_w]
    wr = rest[3 * n_w:3 * n_w + 2]
    x1_ref, u2_ref, comb_ref = rest[3 * n_w + 2:]

    if y_feature_major:
        y = jnp.concatenate([y_ref[hi, h * K_LO:(h + 1) * K_LO, :] for h in range(RWKV_HEADS) for hi in range(2)],
                            axis=0).T
    else:
        y = y_ref[...]
    inv_n = 1.0 / RWKV_HEAD_DIM
    mu = _head64_sum(y) * inv_n
    d = y - mu
    var = _head64_sum(d * d) * inv_n
    yn = d * lax.rsqrt(var + LNX_EPS) * lnw_ref[...] + lnb_ref[...]
    ya = (yn + bonus_ref[...]) * g_ref[...]
    branch_a = _mm(ya, wa)
    branch_b = _mm(ob_ref[...], wb)
    gates = gates_ref[...]
    merged = _sigmoid(gates[:, :D_MODEL]) * branch_a + _sigmoid(gates[:, D_MODEL:]) * branch_b
    x1 = x_ref[...] + _mm(merged, wo)
    x1_ref[...] = x1
    u2 = _rmsnorm(x1, n2_ref[...])
    u2_ref[...] = u2.astype(u2_ref.dtype)

    logits = _mm(u2, wr) + br_ref[...]
    lane = lax.broadcasted_iota(jnp.int32, logits.shape, 1)
    lanef = lane.astype(F32)
    neg = jnp.float32(-jnp.inf)
    big = jnp.float32(1e9)
    is_grp = (lane >= N_EXPERTS) & (lane < N_EXPERTS + N_GROUPS)
    lg = jnp.where(is_grp, logits, neg)
    lg_max = jnp.max(lg, axis=-1, keepdims=True)
    p_grp = 1.0 / jnp.sum(jnp.exp(lg - lg_max), axis=-1, keepdims=True)
    grp = jnp.min(jnp.where(lg == lg_max, lanef, big), axis=-1, keepdims=True) - N_EXPERTS
    in_grp = (lanef >= grp * EXPERTS_PER_GROUP) & (lanef < (grp + 1.0) * EXPERTS_PER_GROUP)
    le = jnp.where(in_grp, logits, neg)
    v1 = jnp.max(le, axis=-1, keepdims=True)
    i1 = jnp.min(jnp.where(le == v1, lanef, big), axis=-1, keepdims=True)
    le2 = jnp.where(lanef == i1, neg, le)
    v2 = jnp.max(le2, axis=-1, keepdims=True)
    i2 = jnp.min(jnp.where(le2 == v2, lanef, big), axis=-1, keepdims=True)
    e2 = jnp.exp(v2 - v1)
    w1 = 1.0 / (1.0 + e2)
    w2 = e2 / (1.0 + e2)
    comb_ref[...] = p_grp * (jnp.where(lanef == i1, w1, 0.0) + jnp.where(lanef == i2, w2, 0.0))


def _mixtail(x, y, g, bonus, ob, gates, lnw, lnb, n2, br, wa, wb, wo, wr, tm, y_feature_major):
    G, n, _ = x.shape
    assert n % tm == 0
    n_w = len(wa)
    consts = (lnw, lnb, n2, br) + tuple(wa) + tuple(wb) + tuple(wo) + tuple(wr)
    row = lambda width: pl.BlockSpec((None, tm, width), lambda b, i: (b, i, 0))
    y_spec = (pl.BlockSpec((2, None, RWKV_WIDTH // 2, tm), lambda b, i: (0, b, 0, i)) if y_feature_major
              else row(RWKV_WIDTH))
    in_specs = [row(D_MODEL), y_spec, row(RWKV_WIDTH), row(RWKV_WIDTH), row(RET_WIDTH),
                row(2 * D_MODEL)] + [_const_spec(c) for c in consts]
    return pl.pallas_call(
        functools.partial(_mixtail_kernel, n_w, y_feature_major),
        grid=(G, n // tm),
        in_specs=in_specs,
        out_specs=(row(D_MODEL), row(D_MODEL), row(LANES)),
        out_shape=(jax.ShapeDtypeStruct((G, n, D_MODEL), F32),
                   jax.ShapeDtypeStruct((G, n, D_MODEL), BF16),
                   jax.ShapeDtypeStruct((G, n, LANES), F32)),
        compiler_params=pltpu.CompilerParams(
            dimension_semantics=("arbitrary", "arbitrary"), vmem_limit_bytes=VMEM_LIMIT),
        name="mixtail",
    )(x, y, g, bonus, ob, gates, *consts)


def _moe_kernel(x1_ref, u2_ref, comb_ref, wg_ref, wu_ref, wd_ref, fg_ref, o_ref, acc):
    grp = pl.program_id(1)

    @pl.when(grp == 0)
    def _():
        acc[...] = jnp.zeros(acc.shape, F32)

    u2 = u2_ref[...]
    comb = comb_ref[...]
    lane = lax.broadcasted_iota(jnp.int32, comb.shape, 1)
    total = acc[...]
    for j in range(EXPERTS_PER_GROUP):
        c = jnp.sum(jnp.where(lane == grp * EXPERTS_PER_GROUP + j, comb, 0.0), axis=-1, keepdims=True)
        hg = jnp.dot(u2, wg_ref[j], preferred_element_type=F32)
        hu = jnp.dot(u2, wu_ref[j], preferred_element_type=F32)
        h = hg * _sigmoid(hg) * hu * c
        total = total + jnp.dot(h.astype(BF16), wd_ref[j], preferred_element_type=F32)
    acc[...] = total

    @pl.when(grp == pl.num_programs(1) - 1)
    def _():
        o_ref[...] = _rmsnorm(x1_ref[...] + acc[...], fg_ref[...])


def _moe(x1, u2, comb, wg, wu, wd, fg, tm):
    N = x1.shape[0]
    E = EXPERTS_PER_GROUP
    return pl.pallas_call(
        _moe_kernel,
        grid=(N // tm, N_GROUPS),
        in_specs=[pl.BlockSpec((tm, D_MODEL), lambda i, g: (i, 0)),
                  pl.BlockSpec((tm, D_MODEL), lambda i, g: (i, 0)),
                  pl.BlockSpec((tm, LANES), lambda i, g: (i, 0)),
                  pl.BlockSpec((E, D_MODEL, EXPERT_FF), lambda i, g: (g, 0, 0)),
                  pl.BlockSpec((E, D_MODEL, EXPERT_FF), lambda i, g: (g, 0, 0)),
                  pl.BlockSpec((E, EXPERT_FF, D_MODEL), lambda i, g: (g, 0, 0)),
                  pl.BlockSpec((1, D_MODEL), lambda i, g: (0, 0))],
        out_specs=pl.BlockSpec((tm, D_MODEL), lambda i, g: (i, 0)),
        out_shape=jax.ShapeDtypeStruct((N, D_MODEL), F32),
        scratch_shapes=[pltpu.VMEM((tm, D_MODEL), F32)],
        compiler_params=pltpu.CompilerParams(
            dimension_semantics=("arbitrary", "arbitrary"), vmem_limit_bytes=VMEM_LIMIT),
        name="moe",
    )(x1, u2, comb, wg, wu, wd, fg)


def _pieces(w, n):
    return tuple(_split_bf16(w.astype(F32), n))


def _rope_tables(pos):
    half = RET_DIM // 2
    inv = ROPE_BASE ** (-jnp.arange(half, dtype=F32) / half)
    ang = pos.astype(F32)[:, None] * inv[None, :]
    c = jnp.cos(ang)
    s = jnp.sin(ang)
    cc = jnp.repeat(c, 2, axis=-1)
    ss = jnp.stack([-s, s], axis=-1).reshape(pos.shape[0], RET_DIM)
    return cc, ss


def _retention_tables():
    C = RET_CHUNK
    log_gamma = jnp.log(1.0 - 2.0 ** (-5.0 - jnp.arange(RET_HEADS, dtype=F32)))
    idx = jnp.arange(C, dtype=F32)
    diff = idx[:, None] - idx[None, :]
    lg = log_gamma[:, None, None]
    dmask = jnp.where(diff[None] >= 0, jnp.exp(jnp.maximum(diff, 0.0)[None] * lg), 0.0)
    q_dec = jnp.exp((idx[None, :] + 1.0) * log_gamma[:, None])
    k_dec = jnp.exp((C - 1.0 - idx[None, :]) * log_gamma[:, None])
    bc = lambda a: jnp.broadcast_to(a[:, :, None], (RET_HEADS, C, C))
    rt_tab = jnp.stack([dmask, bc(q_dec), bc(k_dec)], axis=1)
    gc_tab = jnp.broadcast_to(jnp.exp(C * log_gamma)[:, None, None], (RET_HEADS, SUBLANES, RET_DIM))
    gam1 = jnp.broadcast_to(jnp.exp(log_gamma)[:, None], (RET_HEADS, RET_DIM))
    return rt_tab, gc_tab, gam1


def kernel(x_prompt, x_sample, state_wkv, state_shift, state_ret, meta_tokens, norm1_g, w_in, shift_mu, decay_w0, decay_up, aaa_a0, aaa_up, gate_up, k_k, k_a, r_k, lnx_w, lnx_b, w_branch_a, w_branch_b, w_out, norm2_g, router_group_w, router_group_b, router_expert_w, router_expert_b, expert_w_gate, expert_w_up, expert_w_down, final_norm_g):
    B, T, D = x_prompt.shape
    NS = x_sample.shape[0]
    TP = LEAD + T
    H, HD = RWKV_HEADS, RWKV_HEAD_DIM
    l = 0
    row = lambda a: a.reshape(1, -1)

    zpad = jnp.zeros((LORA_PAD - 64, RWKV_WIDTH), F32)
    dup_pad = jnp.concatenate([decay_up[l], zpad], axis=0)
    aup_pad = jnp.concatenate([zpad, aaa_up[l]], axis=0)
    wr_full = jnp.concatenate([router_expert_w[l], router_group_w[l]], axis=1)
    wr_full = jnp.pad(wr_full, ((0, 0), (0, LANES - wr_full.shape[1])))
    br_full = jnp.pad(jnp.concatenate([router_expert_b[l], router_group_b[l]]), (0, LANES - N_EXPERTS - N_GROUPS))
    wr = _pieces(wr_full, 2)
    wg = expert_w_gate[l].astype(BF16)
    wu = expert_w_up[l].astype(BF16)
    wd = expert_w_down[l].astype(BF16)
    rt_tab, gc_tab, gam1 = _retention_tables()

    def prep_params(n):
        return (row(shift_mu[l]), row(decay_w0[l]), _pieces(dup_pad, n), row(aaa_a0[l]), _pieces(aup_pad, n),
                _pieces(gate_up[l], n), row(k_k[l]), row(k_a[l]), row(r_k[l]))

    def flat_params(p):
        out = []
        for a in p:
            out.extend(a if isinstance(a, tuple) else (a,))
        return tuple(out)

    def tail(n, x, y, g, bonus, ob, gates, tm_tail, tm_moe, y_feature_major):
        x1, u2, comb = _mixtail(x, y, g, bonus, ob, gates, row(lnx_w[l]), row(lnx_b[l]), row(norm2_g[l]),
                                row(br_full), _pieces(w_branch_a[l], n), _pieces(w_branch_b[l], n),
                                _pieces(w_out[l], n), wr, tm_tail, y_feature_major)
        flat = lambda a: a.reshape(-1, a.shape[-1])
        return _moe(flat(x1), flat(u2), flat(comb), wg, wu, wd, row(final_norm_g), tm_moe)

    lead = jnp.concatenate([jnp.zeros((LEAD - N_META, D), F32), meta_tokens.astype(F32)], axis=0)
    cc_p, ss_p = _rope_tables(jnp.arange(TP, dtype=jnp.int32) - (LEAD - N_META))
    pk, g, bonus, gates, ob, shift_p, ret_p = _inproj_prompt(
        x_prompt, lead, cc_p, ss_p, rt_tab, gc_tab, row(norm1_g[l]), _pieces(w_in[l], 1),
        flat_params(prep_params(1)), nb=2)

    pk_t = pk.reshape(6, 2 * B * H, K_LO, TP).transpose(3, 0, 2, 1)
    y_t, s_fin = _rwkv_prompt(pk_t, tb=16)
    y_fm = y_t.transpose(2, 1, 0).reshape(2, B, RWKV_WIDTH // 2, T)
    wkv_p = s_fin.reshape(HD, K_LO, 2, B, H).transpose(3, 4, 2, 1, 0).reshape(B, H, HD, HD)

    y_prompt = tail(1, x_prompt, y_fm, g, bonus, ob, gates, 256, 1024, True).reshape(B, T, D)

    xs = x_sample.reshape(NS, D)
    cc_s, ss_s = _rope_tables(jnp.full((NS,), PAST_LEN, jnp.int32))
    pk_s, v_s, g_s, bonus_s, gates_s, retq, shift_s = _inproj_sample(
        xs, state_shift[l], cc_s, ss_s, row(norm1_g[l]), _pieces(w_in[l], 2), flat_params(prep_params(2)))
    wkv_s, yt_s = _rwkv_sample(state_wkv[l], pk_s.reshape(NS, 5, H, HD),
                               v_s.reshape(NS, H, HD).transpose(0, 2, 1), bt=8)
    y_s = yt_s.transpose(0, 2, 1).reshape(NS, RWKV_WIDTH)
    heads = lambda a: a.reshape(NS, RET_HEADS, RET_DIM)
    q_s, k_s, v_r, g_r = (retq[:, i * RET_WIDTH:(i + 1) * RET_WIDTH] for i in range(4))
    ret_s, ob_s = _ret_sample(state_ret[l], heads(q_s).transpose(0, 2, 1), heads(k_s).transpose(0, 2, 1),
                              heads(v_r), heads(g_r), gam1, bt=8)
    y_sample = tail(2, xs[None], y_s[None], g_s[None], bonus_s[None], ob_s.reshape(1, NS, RET_WIDTH), gates_s[None],
                    NS, NS, False)

    return (y_prompt, y_sample.reshape(NS, 1, D),
            wkv_p[None], shift_p.reshape(1, B, SHIFT_WIDTH), ret_p[None],
            wkv_s[None], shift_s[None], ret_s[None])
```

```python
import functools

import jax
import jax.numpy as jnp
from jax import lax
from jax.experimental import pallas as pl
from jax.experimental.pallas import tpu as pltpu

F32 = jnp.float32
BF16 = jnp.bfloat16

D_MODEL = 1024
N_META = 16
RWKV_WIDTH = 512
RWKV_HEADS = 8
RWKV_HEAD_DIM = 64
LORA_PAD = 128
GATE_LORA = 128
SHIFT_WIDTH = 3 * RWKV_WIDTH + 64 + 64 + GATE_LORA
LNX_EPS = 64e-5
RET_WIDTH = 512
RET_HEADS = 4
RET_DIM = 128
RET_CHUNK = 128
ROPE_BASE = 10000.0
RET_COL0 = SHIFT_WIDTH
GATE_COL0 = SHIFT_WIDTH + 4 * RET_WIDTH
IN_WIDTH = GATE_COL0 + 2 * D_MODEL
N_GROUPS = 4
EXPERTS_PER_GROUP = 4
N_EXPERTS = 16
EXPERT_FF = 256
RMS_EPS = 1e-6
PAST_LEN = 16384

LANES = 128
SUBLANES = 8
VMEM_LIMIT = 60 * 1024 * 1024

LEAD = 128
PK_W, PK_NKK, PK_BB, PK_K, PK_R, PK_V = range(6)
K_LO = 32
KL_UNROLL = 8


def _split_bf16(a, nterms):
    pieces = []
    rem = a
    for _ in range(nterms - 1):
        c = rem * 65537.0
        hi = c - (c - rem)
        pieces.append(hi.astype(BF16))
        rem = rem - hi
    pieces.append(rem.astype(BF16))
    return pieces


def _mm(a, w_refs, cols=None):
    def w(i):
        r = w_refs[i]
        return r[...] if cols is None else r[:, cols[0]:cols[1]]

    if a.dtype == BF16:
        acc = jnp.dot(a, w(0), preferred_element_type=F32)
        if len(w_refs) > 1:
            acc = acc + jnp.dot(a, w(1), preferred_element_type=F32)
        return acc
    if len(w_refs) == 1:
        return jnp.dot(a.astype(BF16), w(0), preferred_element_type=F32)
    a_hi, a_lo = _split_bf16(a, 2)
    acc = jnp.dot(a_hi, w(0), preferred_element_type=F32)
    acc = acc + jnp.dot(a_lo, w(0), preferred_element_type=F32)
    acc = acc + jnp.dot(a_hi, w(1), preferred_element_type=F32)
    return acc


def _rmsnorm(x, g):
    return x * lax.rsqrt(jnp.mean(x * x, axis=-1, keepdims=True) + RMS_EPS) * g


def _sigmoid(x):
    return 1.0 / (1.0 + jnp.exp(-x))


def _head64_sum(x):
    lane = lax.broadcasted_iota(jnp.int32, (x.shape[0], LANES), 1)
    lo = lane < RWKV_HEAD_DIM
    outs = []
    for j in range(x.shape[1] // LANES):
        xj = x[:, j * LANES:(j + 1) * LANES]
        s_lo = jnp.sum(jnp.where(lo, xj, 0.0), axis=-1, keepdims=True)
        s_hi = jnp.sum(jnp.where(lo, 0.0, xj), axis=-1, keepdims=True)
        outs.append(jnp.where(lo, s_lo, s_hi))
    return jnp.concatenate(outs, axis=-1)


def _rope(x, cc, ss):
    lane = lax.broadcasted_iota(jnp.int32, x.shape, 1)
    even = (lane & 1) == 0
    swapped = jnp.where(even, pltpu.roll(x, LANES - 1, 1), pltpu.roll(x, 1, 1))
    return x * cc + swapped * ss


def _rwkv_prep(s, prev, prm):
    (mu, w0, dup, a0, aup, gup, k_k, k_a, r_k) = prm
    xs = s + mu[...] * (prev - s)
    W = RWKV_WIDTH
    r = xs[:, 0:W]
    k = xs[:, W:2 * W]
    v = xs[:, 2 * W:3 * W]
    wa = xs[:, 3 * W:3 * W + LORA_PAD]
    gd = xs[:, 3 * W + LORA_PAD:]
    dec = _mm(jnp.tanh(wa), dup)
    z = -(w0[...] + dec)
    softplus = jnp.maximum(z, 0.0) + jnp.log1p(jnp.exp(-jnp.abs(z)))
    decay = jnp.exp(-jnp.exp(-softplus - 0.5))
    a = _sigmoid(a0[...] + _mm(wa, aup))
    g = _mm(_sigmoid(gd), gup)
    kk = k * k_k[...]
    kk = kk / jnp.maximum(jnp.sqrt(_head64_sum(kk * kk)), 1e-12)
    k_mod = k * (1.0 + (a - 1.0) * k_a[...])
    bonus = _head64_sum(r * k_mod * r_k[...]) * v
    return r, decay, k_mod, v, kk, a, g, bonus


def _take_prep_params(refs, n_w):
    it = iter(refs)
    take = lambda n: tuple(next(it) for _ in range(n))
    (mu, w0), dup, (a0,), aup, gup, (k_k, k_a, r_k) = take(2), take(n_w), take(1), take(n_w), take(n_w), take(3)
    return (mu, w0, dup, a0, aup, gup, k_k, k_a, r_k), tuple(it)


def _store_rwkv_operands(ops, pk_ref, v_ref, g_ref, bonus_ref):
    r, decay, k_mod, v, kk, a, g, bonus = ops
    W = RWKV_WIDTH
    pk_ref[:, PK_W * W:(PK_W + 1) * W] = decay
    pk_ref[:, PK_NKK * W:(PK_NKK + 1) * W] = -kk
    pk_ref[:, PK_BB * W:(PK_BB + 1) * W] = kk * a
    pk_ref[:, PK_K * W:(PK_K + 1) * W] = k_mod
    pk_ref[:, PK_R * W:(PK_R + 1) * W] = r
    v_ref[...] = v
    g_ref[...] = g
    bonus_ref[...] = bonus


def _inproj_prompt_kernel(n_w, x_ref, cc_ref, ss_ref, lead_ref, rt_ref, gc_ref, n1_ref, *rest):
    w_in = rest[:n_w]
    prm, rest = _take_prep_params(rest[n_w:], n_w)
    (pk_ref, g_ref, bonus_ref, gates_ref, ob_ref, shift_ref, ret_ref, prev_buf, ret_state) = rest
    t = pl.program_id(1)
    nb, tm, _ = x_ref.shape
    rows = [slice(b * tm, (b + 1) * tm) for b in range(nb)]

    @pl.when(t == 0)
    def _():
        prev_buf[:, 0:SUBLANES, :] = jnp.zeros((nb, SUBLANES, SHIFT_WIDTH), F32)
        ret_state[...] = jnp.zeros(ret_state.shape, F32)

    x = jnp.where(t == 0, lead_ref[...][None], x_ref[...])
    u = _rmsnorm(x.reshape(nb * tm, D_MODEL), n1_ref[...])
    u_b = u.astype(BF16) if n_w == 1 else u

    s = _mm(u_b, w_in, (0, SHIFT_WIDTH))
    prevs = []
    for b in range(nb):
        s_b = s[rows[b]]
        prev_buf[b, SUBLANES:SUBLANES + tm, :] = s_b
        prevs.append(prev_buf[b, SUBLANES - 1:SUBLANES - 1 + tm, :])
        prev_buf[b, SUBLANES - 1:SUBLANES, :] = s_b[tm - 1:tm, :]
        shift_ref[b] = s_b[tm - 1:tm, :]
    prev = jnp.concatenate(prevs, axis=0)

    r, decay, k_mod, v, kk, a, g, bonus = _rwkv_prep(s, prev, prm)
    W = RWKV_WIDTH
    for b in range(nb):
        rb = rows[b]
        for q, val in ((PK_W, decay), (PK_NKK, -kk), (PK_BB, kk * a), (PK_K, k_mod), (PK_R, r), (PK_V, v)):
            fm = val[rb].T
            for hi in range(2):
                pk_ref[q, hi, b] = jnp.concatenate(
                    [fm[h * RWKV_HEAD_DIM + hi * K_LO:h * RWKV_HEAD_DIM + (hi + 1) * K_LO] for h in range(RWKV_HEADS)],
                    axis=0)
        g_ref[b] = g[rb]
        bonus_ref[b] = bonus[rb]

    gates = _mm(u_b, w_in, (GATE_COL0, IN_WIDTH))
    for b in range(nb):
        gates_ref[b] = gates[rows[b]]

    ret = _mm(u_b, w_in, (RET_COL0, GATE_COL0))
    cc = cc_ref[...]
    ss = ss_ref[...]
    for b in range(nb):
        rb = rows[b]
        for h in range(RET_HEADS):
            c0 = h * RET_DIM
            qc = _rope(ret[rb, c0:c0 + RET_DIM], cc, ss).astype(BF16)
            kc = _rope(ret[rb, RET_WIDTH + c0:RET_WIDTH + c0 + RET_DIM], cc, ss) * (RET_DIM ** -0.5)
            vc = ret[rb, 2 * RET_WIDTH + c0:2 * RET_WIDTH + c0 + RET_DIM].astype(BF16)
            gr = ret[rb, 3 * RET_WIDTH + c0:3 * RET_WIDTH + c0 + RET_DIM]
            state = ret_state[b, h]
            scores = lax.dot_general(qc, kc.astype(BF16), (((1,), (1,)), ((), ())),
                                     preferred_element_type=F32) * rt_ref[h, 0]
            intra = jnp.dot(scores.astype(BF16), vc, preferred_element_type=F32)
            cross = jnp.dot(qc, state.astype(BF16), preferred_element_type=F32) * rt_ref[h, 1]
            kt = (kc * rt_ref[h, 2]).T.astype(BF16)
            ret_state[b, h] = gc_ref[h, 0:1, :] * state + jnp.dot(kt, vc, preferred_element_type=F32)
            o = intra + cross
            o = o * lax.rsqrt(jnp.mean(o * o, axis=-1, keepdims=True) + RMS_EPS)
            ob_ref[b, :, c0:c0 + RET_DIM] = (gr * _sigmoid(gr) * o).astype(ob_ref.dtype)

    @pl.when(t == pl.num_programs(1) - 1)
    def _():
        ret_ref[...] = ret_state[...]


def _inproj_sample_kernel(n_w, x_ref, prev_ref, cc_ref, ss_ref, n1_ref, *rest):
    w_in = rest[:n_w]
    prm, rest = _take_prep_params(rest[n_w:], n_w)
    (pk_ref, v_ref, g_ref, bonus_ref, gates_ref, retq_ref, shift_ref) = rest
    u = _rmsnorm(x_ref[...], n1_ref[...])
    s = _mm(u, w_in, (0, SHIFT_WIDTH))
    shift_ref[...] = s
    _store_rwkv_operands(_rwkv_prep(s, prev_ref[...], prm), pk_ref, v_ref, g_ref, bonus_ref)
    gates_ref[...] = _mm(u, w_in, (GATE_COL0, IN_WIDTH))
    ret = _mm(u, w_in, (RET_COL0, GATE_COL0))
    cc = cc_ref[...]
    ss = ss_ref[...]
    for h in range(RET_HEADS):
        c0 = h * RET_DIM
        retq_ref[:, c0:c0 + RET_DIM] = _rope(ret[:, c0:c0 + RET_DIM], cc, ss)
        retq_ref[:, RET_WIDTH + c0:RET_WIDTH + c0 + RET_DIM] = (
            _rope(ret[:, RET_WIDTH + c0:RET_WIDTH + c0 + RET_DIM], cc, ss) * (RET_DIM ** -0.5))
    retq_ref[:, 2 * RET_WIDTH:] = ret[:, 2 * RET_WIDTH:]


def _const_spec(arr):
    nd = arr.ndim
    return pl.BlockSpec(arr.shape, lambda *_: (0,) * nd, pipeline_mode=pl.Buffered(1))


def _inproj_prompt(x, lead, cc, ss, rt_tab, gc_tab, n1, w_in, prm, nb):
    B, T, _ = x.shape
    tm = RET_CHUNK
    assert lead.shape[0] == tm and T % tm == 0 and B % nb == 0
    TP = tm + T
    n_w = len(w_in)
    consts = (lead, rt_tab, gc_tab, n1) + tuple(w_in) + tuple(prm)
    prow = lambda width: pl.BlockSpec((nb, tm, width), lambda b, t: (b, jnp.maximum(t - 1, 0), 0))
    in_specs = [prow(D_MODEL),
                pl.BlockSpec((tm, LANES), lambda b, t: (t, 0)),
                pl.BlockSpec((tm, LANES), lambda b, t: (t, 0))] + [_const_spec(c) for c in consts]
    out_shape = (
        jax.ShapeDtypeStruct((6, 2, B, RWKV_WIDTH // 2, TP), F32),
        jax.ShapeDtypeStruct((B, T, RWKV_WIDTH), F32),
        jax.ShapeDtypeStruct((B, T, RWKV_WIDTH), F32),
        jax.ShapeDtypeStruct((B, T, 2 * D_MODEL), F32),
        jax.ShapeDtypeStruct((B, T, RET_WIDTH), BF16),
        jax.ShapeDtypeStruct((B, 1, SHIFT_WIDTH), F32),
        jax.ShapeDtypeStruct((B, RET_HEADS, RET_DIM, RET_DIM), F32),
    )
    out_specs = (
        pl.BlockSpec((6, 2, nb, RWKV_WIDTH // 2, tm), lambda b, t: (0, 0, b, 0, t)),
        prow(RWKV_WIDTH), prow(RWKV_WIDTH), prow(2 * D_MODEL), prow(RET_WIDTH),
        pl.BlockSpec((nb, 1, SHIFT_WIDTH), lambda b, t: (b, 0, 0)),
        pl.BlockSpec((nb, RET_HEADS, RET_DIM, RET_DIM), lambda b, t: (b, 0, 0, 0)),
    )
    return pl.pallas_call(
        functools.partial(_inproj_prompt_kernel, n_w),
        grid=(B // nb, TP // tm),
        in_specs=in_specs,
        out_specs=out_specs,
        out_shape=out_shape,
        scratch_shapes=[pltpu.VMEM((nb, tm + SUBLANES, SHIFT_WIDTH), F32),
                        pltpu.VMEM((nb, RET_HEADS, RET_DIM, RET_DIM), F32)],
        compiler_params=pltpu.CompilerParams(
            dimension_semantics=("arbitrary", "arbitrary"), vmem_limit_bytes=VMEM_LIMIT),
        name="inproj_prompt",
    )(x, cc, ss, *consts)


def _inproj_sample(xs, prev, cc, ss, n1, w_in, prm):
    N = xs.shape[0]
    n_w = len(w_in)
    args = (xs, prev, cc, ss, n1) + tuple(w_in) + tuple(prm)
    out_shape = (
        jax.ShapeDtypeStruct((N, 5 * RWKV_WIDTH), F32),
        jax.ShapeDtypeStruct((N, RWKV_WIDTH), F32),
        jax.ShapeDtypeStruct((N, RWKV_WIDTH), F32),
        jax.ShapeDtypeStruct((N, RWKV_WIDTH), F32),
        jax.ShapeDtypeStruct((N, 2 * D_MODEL), F32),
        jax.ShapeDtypeStruct((N, 4 * RET_WIDTH), F32),
        jax.ShapeDtypeStruct((N, SHIFT_WIDTH), F32),
    )
    return pl.pallas_call(
        functools.partial(_inproj_sample_kernel, n_w),
        out_shape=out_shape,
        compiler_params=pltpu.CompilerParams(vmem_limit_bytes=VMEM_LIMIT),
        name="inproj_sample",
    )(*args)


def _rwkv_prompt_kernel(pk_ref, nxt_ref, y_ref, sout_ref, state, sa_buf, kd):
    i = pl.program_id(0)
    tb = pk_ref.shape[0]
    HD = RWKV_HEAD_DIM
    VL = K_LO

    @pl.when(i == 0)
    def _():
        state[...] = jnp.zeros(state.shape, F32)
        sa_buf[...] = jnp.zeros(sa_buf.shape, F32)

    first_half = lax.broadcasted_iota(jnp.int32, (K_LO, LANES), 1) < LANES // 2

    def spread(x, t, q):
        swapped = pltpu.roll(x, LANES // 2, 1)
        kd[t, q, 0:K_LO, :] = jnp.where(first_half, x, swapped)
        kd[t, q, K_LO:HD, :] = jnp.where(first_half, swapped, x)

    for t in range(tb):
        for q in (PK_W, PK_NKK, PK_BB, PK_K, PK_R):
            spread(pk_ref[t, q], t, q)
    spread(nxt_ref[0, 0], tb, PK_NKK)

    def step(t, sa):
        vv = pk_ref[t, PK_V]

        def key_rows(c, accs):
            yacc, sacc = accs
            for j in range(KL_UNROLL):
                k = c * KL_UNROLL + j
                row = lambda tt, q: kd[tt, q, pl.ds(k, 1), :]
                rows = pl.ds(pl.multiple_of(k * VL, VL), VL)
                new = state[rows, :] * row(t, PK_W) + sa * row(t, PK_BB) + vv * row(t, PK_K)
                state[rows, :] = new
                yacc = yacc + new * row(t, PK_R)
                sacc = sacc + new * row(t + 1, PK_NKK)
            return yacc, sacc

        zero = jnp.zeros((VL, LANES), F32)
        yacc, sacc = lax.fori_loop(0, HD // KL_UNROLL, key_rows, (zero, zero))
        y_ref[t] = yacc
        return sacc

    sa_buf[...] = lax.fori_loop(0, tb, step, sa_buf[...])

    @pl.when(i == pl.num_programs(0) - 1)
    def _():
        sout_ref[...] = state[...]


def _rwkv_prompt(pk_t, tb):
    TP = pk_t.shape[0]
    HD = RWKV_HEAD_DIM
    assert (LEAD - N_META) % tb == 0 and LEAD % tb == 0
    skip = (LEAD - N_META) // tb
    meta_blocks = N_META // tb
    return pl.pallas_call(
        _rwkv_prompt_kernel,
        grid=(TP // tb - skip,),
        in_specs=[pl.BlockSpec((tb, 6, K_LO, LANES), lambda i: (i + skip, 0, 0, 0)),
                  pl.BlockSpec((1, 1, K_LO, LANES),
                               lambda i: (jnp.minimum((i + skip + 1) * tb, TP - 1), PK_NKK, 0, 0))],
        out_specs=(pl.BlockSpec((tb, K_LO, LANES), lambda i: (jnp.maximum(i - meta_blocks, 0), 0, 0)),
                   pl.BlockSpec((K_LO * HD, LANES), lambda i: (0, 0))),
        out_shape=(jax.ShapeDtypeStruct((TP - LEAD, K_LO, LANES), F32),
                   jax.ShapeDtypeStruct((K_LO * HD, LANES), F32)),
        scratch_shapes=[pltpu.VMEM((HD * K_LO, LANES), F32), pltpu.VMEM((K_LO, LANES), F32),
                        pltpu.VMEM((tb + 1, 5, HD, LANES), F32)],
        compiler_params=pltpu.CompilerParams(dimension_semantics=("arbitrary",)),
        name="rwkv_prompt",
    )(pk_t, pk_t)


def _lane_col(tile, h):
    lane = lax.broadcasted_iota(jnp.int32, tile.shape, 1)
    return jnp.sum(jnp.where(lane == h, tile, 0.0), axis=-1, keepdims=True)


def _rwkv_sample_kernel(s_ref, pk_ref, vt_ref, sout_ref, yt_ref):
    for b in range(s_ref.shape[0]):
        v_tile = vt_ref[b]
        lane = lax.broadcasted_iota(jnp.int32, v_tile.shape, 1)
        y_tile = jnp.zeros(v_tile.shape, F32)
        for h in range(RWKV_HEADS):
            S = s_ref[b, h]
            sa = jnp.sum(S * pk_ref[b, PK_NKK, h:h + 1, :], axis=-1, keepdims=True)
            new = (S * pk_ref[b, PK_W, h:h + 1, :]
                   + sa * pk_ref[b, PK_BB, h:h + 1, :]
                   + _lane_col(v_tile, h) * pk_ref[b, PK_K, h:h + 1, :])
            sout_ref[b, h] = new
            y = jnp.sum(new * pk_ref[b, PK_R, h:h + 1, :], axis=-1, keepdims=True)
            y_tile = jnp.where(lane == h, y, y_tile)
        yt_ref[b] = y_tile


def _rwkv_sample(state, pk_h, v_t, bt):
    N = state.shape[0]
    H, HD = RWKV_HEADS, RWKV_HEAD_DIM
    return pl.pallas_call(
        _rwkv_sample_kernel,
        grid=(N // bt,),
        in_specs=[pl.BlockSpec((bt, H, HD, HD), lambda i: (i, 0, 0, 0)),
                  pl.BlockSpec((bt, 5, H, HD), lambda i: (i, 0, 0, 0)),
                  pl.BlockSpec((bt, HD, H), lambda i: (i, 0, 0))],
        out_specs=(pl.BlockSpec((bt, H, HD, HD), lambda i: (i, 0, 0, 0)),
                   pl.BlockSpec((bt, HD, H), lambda i: (i, 0, 0))),
        out_shape=(jax.ShapeDtypeStruct(state.shape, F32),
                   jax.ShapeDtypeStruct((N, HD, H), F32)),
        compiler_params=pltpu.CompilerParams(dimension_semantics=("arbitrary",)),
        name="rwkv_sample",
    )(state, pk_h, v_t)


def _ret_sample_kernel(s_ref, qt_ref, kt_ref, v_ref, g_ref, gam_ref, sout_ref, ob_ref):
    for b in range(s_ref.shape[0]):
        q_tile = qt_ref[b]
        k_tile = kt_ref[b]
        for h in range(RET_HEADS):
            S = s_ref[b, h]
            qc = _lane_col(q_tile, h)
            kc = _lane_col(k_tile, h)
            vrow = v_ref[b, h:h + 1, :]
            gam = gam_ref[h:h + 1, :]
            qk = jnp.sum(qc * kc, axis=0, keepdims=True)
            cross = jnp.sum(qc * S, axis=0, keepdims=True)
            o = qk * vrow + cross * gam
            sout_ref[b, h] = gam * S + kc * vrow
            o = o * lax.rsqrt(jnp.mean(o * o, axis=-1, keepdims=True) + RMS_EPS)
            gr = g_ref[b, h:h + 1, :]
            ob_ref[b, h:h + 1, :] = gr * _sigmoid(gr) * o


def _ret_sample(state, q_t, k_t, v_h, g_h, gam, bt):
    N = state.shape[0]
    H, DK = RET_HEADS, RET_DIM
    col = pl.BlockSpec((bt, DK, H), lambda i: (i, 0, 0))
    rowspec = pl.BlockSpec((bt, H, DK), lambda i: (i, 0, 0))
    st = pl.BlockSpec((bt, H, DK, DK), lambda i: (i, 0, 0, 0))
    return pl.pallas_call(
        _ret_sample_kernel,
        grid=(N // bt,),
        in_specs=[st, col, col, rowspec, rowspec, pl.BlockSpec((H, DK), lambda i: (0, 0))],
        out_specs=(st, rowspec),
        out_shape=(jax.ShapeDtypeStruct(state.shape, F32), jax.ShapeDtypeStruct((N, H, DK), F32)),
        compiler_params=pltpu.CompilerParams(dimension_semantics=("arbitrary",)),
        name="ret_sample",
    )(state, q_t, k_t, v_h, g_h, gam)


def _mixtail_kernel(n_w, y_feature_major, x_ref, y_ref, g_ref, bonus_ref, ob_ref, gates_ref,
                    lnw_ref, lnb_ref, n2_ref, br_ref, *rest):
    wa = rest[0:n_w]
    wb = rest[n_w:2 * n_w]
    wo = rest[2 * n_w:3 * n_w]
    wr = rest[3 * n_w:3 * n_w + 2]
    x1_ref, u2_ref, comb_ref = rest[3 * n_w + 2:]

    if y_feature_major:
        y = jnp.concatenate([y_ref[hi, h * K_LO:(h + 1) * K_LO, :] for h in range(RWKV_HEADS) for hi in range(2)],
                            axis=0).T
    else:
        y = y_ref[...]
    inv_n = 1.0 / RWKV_HEAD_DIM
    mu = _head64_sum(y) * inv_n
    d = y - mu
    var = _head64_sum(d * d) * inv_n
    yn = d * lax.rsqrt(var + LNX_EPS) * lnw_ref[...] + lnb_ref[...]
    ya = (yn + bonus_ref[...]) * g_ref[...]
    branch_a = _mm(ya, wa)
    branch_b = _mm(ob_ref[...], wb)
    gates = gates_ref[...]
    merged = _sigmoid(gates[:, :D_MODEL]) * branch_a + _sigmoid(gates[:, D_MODEL:]) * branch_b
    x1 = x_ref[...] + _mm(merged, wo)
    x1_ref[...] = x1
    u2 = _rmsnorm(x1, n2_ref[...])
    u2_ref[...] = u2.astype(u2_ref.dtype)

    logits = _mm(u2, wr) + br_ref[...]
    lane = lax.broadcasted_iota(jnp.int32, logits.shape, 1)
    lanef = lane.astype(F32)
    neg = jnp.float32(-jnp.inf)
    big = jnp.float32(1e9)
    is_grp = (lane >= N_EXPERTS) & (lane < N_EXPERTS + N_GROUPS)
    lg = jnp.where(is_grp, logits, neg)
    lg_max = jnp.max(lg, axis=-1, keepdims=True)
    p_grp = 1.0 / jnp.sum(jnp.exp(lg - lg_max), axis=-1, keepdims=True)
    grp = jnp.min(jnp.where(lg == lg_max, lanef, big), axis=-1, keepdims=True) - N_EXPERTS
    in_grp = (lanef >= grp * EXPERTS_PER_GROUP) & (lanef < (grp + 1.0) * EXPERTS_PER_GROUP)
    le = jnp.where(in_grp, logits, neg)
    v1 = jnp.max(le, axis=-1, keepdims=True)
    i1 = jnp.min(jnp.where(le == v1, lanef, big), axis=-1, keepdims=True)
    le2 = jnp.where(lanef == i1, neg, le)
    v2 = jnp.max(le2, axis=-1, keepdims=True)
    i2 = jnp.min(jnp.where(le2 == v2, lanef, big), axis=-1, keepdims=True)
    e2 = jnp.exp(v2 - v1)
    w1 = 1.0 / (1.0 + e2)
    w2 = e2 / (1.0 + e2)
    comb_ref[...] = p_grp * (jnp.where(lanef == i1, w1, 0.0) + jnp.where(lanef == i2, w2, 0.0))


def _mixtail(x, y, g, bonus, ob, gates, lnw, lnb, n2, br, wa, wb, wo, wr, tm, y_feature_major):
    G, n, _ = x.shape
    assert n % tm == 0
    n_w = len(wa)
    consts = (lnw, lnb, n2, br) + tuple(wa) + tuple(wb) + tuple(wo) + tuple(wr)
    row = lambda width: pl.BlockSpec((None, tm, width), lambda b, i: (b, i, 0))
    y_spec = (pl.BlockSpec((2, None, RWKV_WIDTH // 2, tm), lambda b, i: (0, b, 0, i)) if y_feature_major
              else row(RWKV_WIDTH))
    in_specs = [row(D_MODEL), y_spec, row(RWKV_WIDTH), row(RWKV_WIDTH), row(RET_WIDTH),
                row(2 * D_MODEL)] + [_const_spec(c) for c in consts]
    return pl.pallas_call(
        functools.partial(_mixtail_kernel, n_w, y_feature_major),
        grid=(G, n // tm),
        in_specs=in_specs,
        out_specs=(row(D_MODEL), row(D_MODEL), row(LANES)),
        out_shape=(jax.ShapeDtypeStruct((G, n, D_MODEL), F32),
                   jax.ShapeDtypeStruct((G, n, D_MODEL), BF16),
                   jax.ShapeDtypeStruct((G, n, LANES), F32)),
        compiler_params=pltpu.CompilerParams(
            dimension_semantics=("arbitrary", "arbitrary"), vmem_limit_bytes=VMEM_LIMIT),
        name="mixtail",
    )(x, y, g, bonus, ob, gates, *consts)


def _moe_kernel(x1_ref, u2_ref, comb_ref, wg_ref, wu_ref, wd_ref, fg_ref, o_ref, acc):
    grp = pl.program_id(1)

    @pl.when(grp == 0)
    def _():
        acc[...] = jnp.zeros(acc.shape, F32)

    u2 = u2_ref[...]
    comb = comb_ref[...]
    lane = lax.broadcasted_iota(jnp.int32, comb.shape, 1)
    total = acc[...]
    for j in range(EXPERTS_PER_GROUP):
        c = jnp.sum(jnp.where(lane == grp * EXPERTS_PER_GROUP + j, comb, 0.0), axis=-1, keepdims=True)
        hg = jnp.dot(u2, wg_ref[j], preferred_element_type=F32)
        hu = jnp.dot(u2, wu_ref[j], preferred_element_type=F32)
        h = hg * _sigmoid(hg) * hu * c
        total = total + jnp.dot(h.astype(BF16), wd_ref[j], preferred_element_type=F32)
    acc[...] = total

    @pl.when(grp == pl.num_programs(1) - 1)
    def _():
        o_ref[...] = _rmsnorm(x1_ref[...] + acc[...], fg_ref[...])


def _moe(x1, u2, comb, wg, wu, wd, fg, tm):
    N = x1.shape[0]
    E = EXPERTS_PER_GROUP
    return pl.pallas_call(
        _moe_kernel,
        grid=(N // tm, N_GROUPS),
        in_specs=[pl.BlockSpec((tm, D_MODEL), lambda i, g: (i, 0)),
                  pl.BlockSpec((tm, D_MODEL), lambda i, g: (i, 0)),
                  pl.BlockSpec((tm, LANES), lambda i, g: (i, 0)),
                  pl.BlockSpec((E, D_MODEL, EXPERT_FF), lambda i, g: (g, 0, 0)),
                  pl.BlockSpec((E, D_MODEL, EXPERT_FF), lambda i, g: (g, 0, 0)),
                  pl.BlockSpec((E, EXPERT_FF, D_MODEL), lambda i, g: (g, 0, 0)),
                  pl.BlockSpec((1, D_MODEL), lambda i, g: (0, 0))],
        out_specs=pl.BlockSpec((tm, D_MODEL), lambda i, g: (i, 0)),
        out_shape=jax.ShapeDtypeStruct((N, D_MODEL), F32),
        scratch_shapes=[pltpu.VMEM((tm, D_MODEL), F32)],
        compiler_params=pltpu.CompilerParams(
            dimension_semantics=("arbitrary", "arbitrary"), vmem_limit_bytes=VMEM_LIMIT),
        name="moe",
    )(x1, u2, comb, wg, wu, wd, fg)


def _pieces(w, n):
    return tuple(_split_bf16(w.astype(F32), n))


def _rope_tables(pos):
    half = RET_DIM // 2
    inv = ROPE_BASE ** (-jnp.arange(half, dtype=F32) / half)
    ang = pos.astype(F32)[:, None] * inv[None, :]
    c = jnp.cos(ang)
    s = jnp.sin(ang)
    cc = jnp.repeat(c, 2, axis=-1)
    ss = jnp.stack([-s, s], axis=-1).reshape(pos.shape[0], RET_DIM)
    return cc, ss


def _retention_tables():
    C = RET_CHUNK
    log_gamma = jnp.log(1.0 - 2.0 ** (-5.0 - jnp.arange(RET_HEADS, dtype=F32)))
    idx = jnp.arange(C, dtype=F32)
    diff = idx[:, None] - idx[None, :]
    lg = log_gamma[:, None, None]
    dmask = jnp.where(diff[None] >= 0, jnp.exp(jnp.maximum(diff, 0.0)[None] * lg), 0.0)
    q_dec = jnp.exp((idx[None, :] + 1.0) * log_gamma[:, None])
    k_dec = jnp.exp((C - 1.0 - idx[None, :]) * log_gamma[:, None])
    bc = lambda a: jnp.broadcast_to(a[:, :, None], (RET_HEADS, C, C))
    rt_tab = jnp.stack([dmask, bc(q_dec), bc(k_dec)], axis=1)
    gc_tab = jnp.broadcast_to(jnp.exp(C * log_gamma)[:, None, None], (RET_HEADS, SUBLANES, RET_DIM))
    gam1 = jnp.broadcast_to(jnp.exp(log_gamma)[:, None], (RET_HEADS, RET_DIM))
    return rt_tab, gc_tab, gam1


def kernel(x_prompt, x_sample, state_wkv, state_shift, state_ret, meta_tokens, norm1_g, w_in, shift_mu, decay_w0, decay_up, aaa_a0, aaa_up, gate_up, k_k, k_a, r_k, lnx_w, lnx_b, w_branch_a, w_branch_b, w_out, norm2_g, router_group_w, router_group_b, router_expert_w, router_expert_b, expert_w_gate, expert_w_up, expert_w_down, final_norm_g):
    B, T, D = x_prompt.shape
    NS = x_sample.shape[0]
    TP = LEAD + T
    H, HD = RWKV_HEADS, RWKV_HEAD_DIM
    l = 0
    row = lambda a: a.reshape(1, -1)

    zpad = jnp.zeros((LORA_PAD - 64, RWKV_WIDTH), F32)
    dup_pad = jnp.concatenate([decay_up[l], zpad], axis=0)
    aup_pad = jnp.concatenate([zpad, aaa_up[l]], axis=0)
    wr_full = jnp.concatenate([router_expert_w[l], router_group_w[l]], axis=1)
    wr_full = jnp.pad(wr_full, ((0, 0), (0, LANES - wr_full.shape[1])))
    br_full = jnp.pad(jnp.concatenate([router_expert_b[l], router_group_b[l]]), (0, LANES - N_EXPERTS - N_GROUPS))
    wr = _pieces(wr_full, 2)
    wg = expert_w_gate[l].astype(BF16)
    wu = expert_w_up[l].astype(BF16)
    wd = expert_w_down[l].astype(BF16)
    rt_tab, gc_tab, gam1 = _retention_tables()

    def prep_params(n):
        return (row(shift_mu[l]), row(decay_w0[l]), _pieces(dup_pad, n), row(aaa_a0[l]), _pieces(aup_pad, n),
                _pieces(gate_up[l], n), row(k_k[l]), row(k_a[l]), row(r_k[l]))

    def flat_params(p):
        out = []
        for a in p:
            out.extend(a if isinstance(a, tuple) else (a,))
        return tuple(out)

    def tail(n, x, y, g, bonus, ob, gates, tm_tail, tm_moe, y_feature_major):
        x1, u2, comb = _mixtail(x, y, g, bonus, ob, gates, row(lnx_w[l]), row(lnx_b[l]), row(norm2_g[l]),
                                row(br_full), _pieces(w_branch_a[l], n), _pieces(w_branch_b[l], n),
                                _pieces(w_out[l], n), wr, tm_tail, y_feature_major)
        flat = lambda a: a.reshape(-1, a.shape[-1])
        return _moe(flat(x1), flat(u2), flat(comb), wg, wu, wd, row(final_norm_g), tm_moe)

    lead = jnp.concatenate([jnp.zeros((LEAD - N_META, D), F32), meta_tokens.astype(F32)], axis=0)
    cc_p, ss_p = _rope_tables(jnp.arange(TP, dtype=jnp.int32) - (LEAD - N_META))
    pk, g, bonus, gates, ob, shift_p, ret_p = _inproj_prompt(
        x_prompt, lead, cc_p, ss_p, rt_tab, gc_tab, row(norm1_g[l]), _pieces(w_in[l], 1),
        flat_params(prep_params(1)), nb=4)

    pk_t = pk.reshape(6, 2 * B * H, K_LO, TP).transpose(3, 0, 2, 1)
    y_t, s_fin = _rwkv_prompt(pk_t, tb=16)
    y_fm = y_t.transpose(2, 1, 0).reshape(2, B, RWKV_WIDTH // 2, T)
    wkv_p = s_fin.reshape(HD, K_LO, 2, B, H).transpose(3, 4, 2, 1, 0).reshape(B, H, HD, HD)

    y_prompt = tail(1, x_prompt, y_fm, g, bonus, ob, gates, 512, 1024, True).reshape(B, T, D)

    xs = x_sample.reshape(NS, D)
    cc_s, ss_s = _rope_tables(jnp.full((NS,), PAST_LEN, jnp.int32))
    pk_s, v_s, g_s, bonus_s, gates_s, retq, shift_s = _inproj_sample(
        xs, state_shift[l], cc_s, ss_s, row(norm1_g[l]), _pieces(w_in[l], 2), flat_params(prep_params(2)))
    wkv_s, yt_s = _rwkv_sample(state_wkv[l], pk_s.reshape(NS, 5, H, HD),
                               v_s.reshape(NS, H, HD).transpose(0, 2, 1), bt=8)
    y_s = yt_s.transpose(0, 2, 1).reshape(NS, RWKV_WIDTH)
    heads = lambda a: a.reshape(NS, RET_HEADS, RET_DIM)
    q_s, k_s, v_r, g_r = (retq[:, i * RET_WIDTH:(i + 1) * RET_WIDTH] for i in range(4))
    ret_s, ob_s = _ret_sample(state_ret[l], heads(q_s).transpose(0, 2, 1), heads(k_s).transpose(0, 2, 1),
                              heads(v_r), heads(g_r), gam1, bt=8)
    y_sample = tail(2, xs[None], y_s[None], g_s[None], bonus_s[None], ob_s.reshape(1, NS, RET_WIDTH), gates_s[None],
                    NS, NS, False)

    return (y_prompt, y_sample.reshape(NS, 1, D),
            wkv_p[None], shift_p.reshape(1, B, SHIFT_WIDTH), ret_p[None],
            wkv_s[None], shift_s[None], ret_s[None])
```

```python
import functools

import jax
import jax.numpy as jnp
from jax import lax
from jax.experimental import pallas as pl
from jax.experimental.pallas import tpu as pltpu

F32 = jnp.float32
BF16 = jnp.bfloat16

D_MODEL = 1024
N_META = 16
RWKV_WIDTH = 512
RWKV_HEADS = 8
RWKV_HEAD_DIM = 64
LORA_PAD = 128
GATE_LORA = 128
SHIFT_WIDTH = 3 * RWKV_WIDTH + 64 + 64 + GATE_LORA
LNX_EPS = 64e-5
RET_WIDTH = 512
RET_HEADS = 4
RET_DIM = 128
RET_CHUNK = 128
ROPE_BASE = 10000.0
RET_COL0 = SHIFT_WIDTH
GATE_COL0 = SHIFT_WIDTH + 4 * RET_WIDTH
IN_WIDTH = GATE_COL0 + 2 * D_MODEL
N_GROUPS = 4
EXPERTS_PER_GROUP = 4
N_EXPERTS = 16
EXPERT_FF = 256
RMS_EPS = 1e-6
PAST_LEN = 16384

LANES = 128
SUBLANES = 8
VMEM_LIMIT = 60 * 1024 * 1024

LEAD = 128
PK_W, PK_NKK, PK_BB, PK_K, PK_R, PK_V = range(6)
K_LO = 32
KL_UNROLL = 8


def _split_bf16(a, nterms):
    pieces = []
    rem = a
    for _ in range(nterms - 1):
        c = rem * 65537.0
        hi = c - (c - rem)
        pieces.append(hi.astype(BF16))
        rem = rem - hi
    pieces.append(rem.astype(BF16))
    return pieces


def _mm(a, w_refs, cols=None):
    def w(i):
        r = w_refs[i]
        return r[...] if cols is None else r[:, cols[0]:cols[1]]

    if a.dtype == BF16:
        acc = jnp.dot(a, w(0), preferred_element_type=F32)
        if len(w_refs) > 1:
            acc = acc + jnp.dot(a, w(1), preferred_element_type=F32)
        return acc
    if len(w_refs) == 1:
        return jnp.dot(a.astype(BF16), w(0), preferred_element_type=F32)
    a_hi, a_lo = _split_bf16(a, 2)
    acc = jnp.dot(a_hi, w(0), preferred_element_type=F32)
    acc = acc + jnp.dot(a_lo, w(0), preferred_element_type=F32)
    acc = acc + jnp.dot(a_hi, w(1), preferred_element_type=F32)
    return acc


def _rmsnorm(x, g):
    return x * lax.rsqrt(jnp.mean(x * x, axis=-1, keepdims=True) + RMS_EPS) * g


def _sigmoid(x):
    return 1.0 / (1.0 + jnp.exp(-x))


def _head64_sum(x):
    lane = lax.broadcasted_iota(jnp.int32, (x.shape[0], LANES), 1)
    lo = lane < RWKV_HEAD_DIM
    outs = []
    for j in range(x.shape[1] // LANES):
        xj = x[:, j * LANES:(j + 1) * LANES]
        s_lo = jnp.sum(jnp.where(lo, xj, 0.0), axis=-1, keepdims=True)
        s_hi = jnp.sum(jnp.where(lo, 0.0, xj), axis=-1, keepdims=True)
        outs.append(jnp.where(lo, s_lo, s_hi))
    return jnp.concatenate(outs, axis=-1)


def _rope(x, cc, ss):
    lane = lax.broadcasted_iota(jnp.int32, x.shape, 1)
    even = (lane & 1) == 0
    swapped = jnp.where(even, pltpu.roll(x, LANES - 1, 1), pltpu.roll(x, 1, 1))
    return x * cc + swapped * ss


def _rwkv_prep(s, prev, prm):
    (mu, w0, dup, a0, aup, gup, k_k, k_a, r_k) = prm
    xs = s + mu[...] * (prev - s)
    W = RWKV_WIDTH
    r = xs[:, 0:W]
    k = xs[:, W:2 * W]
    v = xs[:, 2 * W:3 * W]
    wa = xs[:, 3 * W:3 * W + LORA_PAD]
    gd = xs[:, 3 * W + LORA_PAD:]
    dec = _mm(jnp.tanh(wa), dup)
    z = -(w0[...] + dec)
    softplus = jnp.maximum(z, 0.0) + jnp.log1p(jnp.exp(-jnp.abs(z)))
    decay = jnp.exp(-jnp.exp(-softplus - 0.5))
    a = _sigmoid(a0[...] + _mm(wa, aup))
    g = _mm(_sigmoid(gd), gup)
    kk = k * k_k[...]
    kk = kk / jnp.maximum(jnp.sqrt(_head64_sum(kk * kk)), 1e-12)
    k_mod = k * (1.0 + (a - 1.0) * k_a[...])
    bonus = _head64_sum(r * k_mod * r_k[...]) * v
    return r, decay, k_mod, v, kk, a, g, bonus


def _take_prep_params(refs, n_w):
    it = iter(refs)
    take = lambda n: tuple(next(it) for _ in range(n))
    (mu, w0), dup, (a0,), aup, gup, (k_k, k_a, r_k) = take(2), take(n_w), take(1), take(n_w), take(n_w), take(3)
    return (mu, w0, dup, a0, aup, gup, k_k, k_a, r_k), tuple(it)


def _store_rwkv_operands(ops, pk_ref, v_ref, g_ref, bonus_ref):
    r, decay, k_mod, v, kk, a, g, bonus = ops
    W = RWKV_WIDTH
    pk_ref[:, PK_W * W:(PK_W + 1) * W] = decay
    pk_ref[:, PK_NKK * W:(PK_NKK + 1) * W] = -kk
    pk_ref[:, PK_BB * W:(PK_BB + 1) * W] = kk * a
    pk_ref[:, PK_K * W:(PK_K + 1) * W] = k_mod
    pk_ref[:, PK_R * W:(PK_R + 1) * W] = r
    v_ref[...] = v
    g_ref[...] = g
    bonus_ref[...] = bonus


def _inproj_prompt_kernel(n_w, x_ref, cc_ref, ss_ref, lead_ref, rt_ref, gc_ref, n1_ref, *rest):
    w_in = rest[:n_w]
    prm, rest = _take_prep_params(rest[n_w:], n_w)
    (pk_ref, g_ref, bonus_ref, gates_ref, ob_ref, shift_ref, ret_ref, prev_buf, ret_state) = rest
    t = pl.program_id(1)
    nb, tm, _ = x_ref.shape
    rows = [slice(b * tm, (b + 1) * tm) for b in range(nb)]

    @pl.when(t == 0)
    def _():
        prev_buf[:, 0:SUBLANES, :] = jnp.zeros((nb, SUBLANES, SHIFT_WIDTH), F32)
        ret_state[...] = jnp.zeros(ret_state.shape, F32)

    x = jnp.where(t == 0, lead_ref[...][None], x_ref[...])
    u = _rmsnorm(x.reshape(nb * tm, D_MODEL), n1_ref[...])
    u_b = u.astype(BF16) if n_w == 1 else u

    s = _mm(u_b, w_in, (0, SHIFT_WIDTH))
    prevs = []
    for b in range(nb):
        s_b = s[rows[b]]
        prev_buf[b, SUBLANES:SUBLANES + tm, :] = s_b
        prevs.append(prev_buf[b, SUBLANES - 1:SUBLANES - 1 + tm, :])
        prev_buf[b, SUBLANES - 1:SUBLANES, :] = s_b[tm - 1:tm, :]
        shift_ref[b] = s_b[tm - 1:tm, :]
    prev = jnp.concatenate(prevs, axis=0)

    r, decay, k_mod, v, kk, a, g, bonus = _rwkv_prep(s, prev, prm)
    W = RWKV_WIDTH
    for b in range(nb):
        rb = rows[b]
        for q, val in ((PK_W, decay), (PK_NKK, -kk), (PK_BB, kk * a), (PK_K, k_mod), (PK_R, r), (PK_V, v)):
            fm = val[rb].T
            for hi in range(2):
                pk_ref[q, hi, b] = jnp.concatenate(
                    [fm[h * RWKV_HEAD_DIM + hi * K_LO:h * RWKV_HEAD_DIM + (hi + 1) * K_LO] for h in range(RWKV_HEADS)],
                    axis=0)
        g_ref[b] = g[rb]
        bonus_ref[b] = bonus[rb]

    gates = _mm(u_b, w_in, (GATE_COL0, IN_WIDTH))
    for b in range(nb):
        gates_ref[b] = gates[rows[b]]

    ret = _mm(u_b, w_in, (RET_COL0, GATE_COL0))
    cc = cc_ref[...]
    ss = ss_ref[...]
    for b in range(nb):
        rb = rows[b]
        for h in range(RET_HEADS):
            c0 = h * RET_DIM
            qc = _rope(ret[rb, c0:c0 + RET_DIM], cc, ss).astype(BF16)
            kc = _rope(ret[rb, RET_WIDTH + c0:RET_WIDTH + c0 + RET_DIM], cc, ss) * (RET_DIM ** -0.5)
            vc = ret[rb, 2 * RET_WIDTH + c0:2 * RET_WIDTH + c0 + RET_DIM].astype(BF16)
            gr = ret[rb, 3 * RET_WIDTH + c0:3 * RET_WIDTH + c0 + RET_DIM]
            state = ret_state[b, h]
            scores = lax.dot_general(qc, kc.astype(BF16), (((1,), (1,)), ((), ())),
                                     preferred_element_type=F32) * rt_ref[h, 0]
            intra = jnp.dot(scores.astype(BF16), vc, preferred_element_type=F32)
            cross = jnp.dot(qc, state.astype(BF16), preferred_element_type=F32) * rt_ref[h, 1]
            kt = (kc * rt_ref[h, 2]).T.astype(BF16)
            ret_state[b, h] = gc_ref[h, 0:1, :] * state + jnp.dot(kt, vc, preferred_element_type=F32)
            o = intra + cross
            o = o * lax.rsqrt(jnp.mean(o * o, axis=-1, keepdims=True) + RMS_EPS)
            ob_ref[b, :, c0:c0 + RET_DIM] = (gr * _sigmoid(gr) * o).astype(ob_ref.dtype)

    @pl.when(t == pl.num_programs(1) - 1)
    def _():
        ret_ref[...] = ret_state[...]


def _inproj_sample_kernel(n_w, x_ref, prev_ref, cc_ref, ss_ref, n1_ref, *rest):
    w_in = rest[:n_w]
    prm, rest = _take_prep_params(rest[n_w:], n_w)
    (pk_ref, v_ref, g_ref, bonus_ref, gates_ref, retq_ref, shift_ref) = rest
    u = _rmsnorm(x_ref[...], n1_ref[...])
    s = _mm(u, w_in, (0, SHIFT_WIDTH))
    shift_ref[...] = s
    _store_rwkv_operands(_rwkv_prep(s, prev_ref[...], prm), pk_ref, v_ref, g_ref, bonus_ref)
    gates_ref[...] = _mm(u, w_in, (GATE_COL0, IN_WIDTH))
    ret = _mm(u, w_in, (RET_COL0, GATE_COL0))
    cc = cc_ref[...]
    ss = ss_ref[...]
    for h in range(RET_HEADS):
        c0 = h * RET_DIM
        retq_ref[:, c0:c0 + RET_DIM] = _rope(ret[:, c0:c0 + RET_DIM], cc, ss)
        retq_ref[:, RET_WIDTH + c0:RET_WIDTH + c0 + RET_DIM] = (
            _rope(ret[:, RET_WIDTH + c0:RET_WIDTH + c0 + RET_DIM], cc, ss) * (RET_DIM ** -0.5))
    retq_ref[:, 2 * RET_WIDTH:] = ret[:, 2 * RET_WIDTH:]


def _const_spec(arr):
    nd = arr.ndim
    return pl.BlockSpec(arr.shape, lambda *_: (0,) * nd, pipeline_mode=pl.Buffered(1))


def _inproj_prompt(x, lead, cc, ss, rt_tab, gc_tab, n1, w_in, prm, nb):
    B, T, _ = x.shape
    tm = RET_CHUNK
    assert lead.shape[0] == tm and T % tm == 0 and B % nb == 0
    TP = tm + T
    n_w = len(w_in)
    consts = (lead, rt_tab, gc_tab, n1) + tuple(w_in) + tuple(prm)
    prow = lambda width: pl.BlockSpec((nb, tm, width), lambda b, t: (b, jnp.maximum(t - 1, 0), 0))
    in_specs = [prow(D_MODEL),
                pl.BlockSpec((tm, LANES), lambda b, t: (t, 0)),
                pl.BlockSpec((tm, LANES), lambda b, t: (t, 0))] + [_const_spec(c) for c in consts]
    out_shape = (
        jax.ShapeDtypeStruct((6, 2, B, RWKV_WIDTH // 2, TP), F32),
        jax.ShapeDtypeStruct((B, T, RWKV_WIDTH), F32),
        jax.ShapeDtypeStruct((B, T, RWKV_WIDTH), F32),
        jax.ShapeDtypeStruct((B, T, 2 * D_MODEL), F32),
        jax.ShapeDtypeStruct((B, T, RET_WIDTH), BF16),
        jax.ShapeDtypeStruct((B, 1, SHIFT_WIDTH), F32),
        jax.ShapeDtypeStruct((B, RET_HEADS, RET_DIM, RET_DIM), F32),
    )
    out_specs = (
        pl.BlockSpec((6, 2, nb, RWKV_WIDTH // 2, tm), lambda b, t: (0, 0, b, 0, t)),
        prow(RWKV_WIDTH), prow(RWKV_WIDTH), prow(2 * D_MODEL), prow(RET_WIDTH),
        pl.BlockSpec((nb, 1, SHIFT_WIDTH), lambda b, t: (b, 0, 0)),
        pl.BlockSpec((nb, RET_HEADS, RET_DIM, RET_DIM), lambda b, t: (b, 0, 0, 0)),
    )
    return pl.pallas_call(
        functools.partial(_inproj_prompt_kernel, n_w),
        grid=(B // nb, TP // tm),
        in_specs=in_specs,
        out_specs=out_specs,
        out_shape=out_shape,
        scratch_shapes=[pltpu.VMEM((nb, tm + SUBLANES, SHIFT_WIDTH), F32),
                        pltpu.VMEM((nb, RET_HEADS, RET_DIM, RET_DIM), F32)],
        compiler_params=pltpu.CompilerParams(
            dimension_semantics=("arbitrary", "arbitrary"), vmem_limit_bytes=VMEM_LIMIT),
        name="inproj_prompt",
    )(x, cc, ss, *consts)


def _inproj_sample(xs, prev, cc, ss, n1, w_in, prm):
    N = xs.shape[0]
    n_w = len(w_in)
    args = (xs, prev, cc, ss, n1) + tuple(w_in) + tuple(prm)
    out_shape = (
        jax.ShapeDtypeStruct((N, 5 * RWKV_WIDTH), F32),
        jax.ShapeDtypeStruct((N, RWKV_WIDTH), F32),
        jax.ShapeDtypeStruct((N, RWKV_WIDTH), F32),
        jax.ShapeDtypeStruct((N, RWKV_WIDTH), F32),
        jax.ShapeDtypeStruct((N, 2 * D_MODEL), F32),
        jax.ShapeDtypeStruct((N, 4 * RET_WIDTH), F32),
        jax.ShapeDtypeStruct((N, SHIFT_WIDTH), F32),
    )
    return pl.pallas_call(
        functools.partial(_inproj_sample_kernel, n_w),
        out_shape=out_shape,
        compiler_params=pltpu.CompilerParams(vmem_limit_bytes=VMEM_LIMIT),
        name="inproj_sample",
    )(*args)


def _rwkv_prompt_kernel(pk_ref, nxt_ref, y_ref, sout_ref, state, sa_buf, kd):
    i = pl.program_id(0)
    tb = pk_ref.shape[0]
    HD = RWKV_HEAD_DIM
    VL = K_LO

    @pl.when(i == 0)
    def _():
        state[...] = jnp.zeros(state.shape, F32)
        sa_buf[...] = jnp.zeros(sa_buf.shape, F32)

    first_half = lax.broadcasted_iota(jnp.int32, (K_LO, LANES), 1) < LANES // 2

    def spread(x, t, q):
        swapped = pltpu.roll(x, LANES // 2, 1)
        kd[t, q, 0:K_LO, :] = jnp.where(first_half, x, swapped)
        kd[t, q, K_LO:HD, :] = jnp.where(first_half, swapped, x)

    for t in range(tb):
        for q in (PK_W, PK_NKK, PK_BB, PK_K, PK_R):
            spread(pk_ref[t, q], t, q)
    spread(nxt_ref[0, 0], tb, PK_NKK)

    def step(t, sa):
        vv = pk_ref[t, PK_V]

        def key_rows(c, accs):
            yacc, sacc = accs
            for j in range(KL_UNROLL):
                k = c * KL_UNROLL + j
                row = lambda tt, q: kd[tt, q, pl.ds(k, 1), :]
                rows = pl.ds(pl.multiple_of(k * VL, VL), VL)
                new = state[rows, :] * row(t, PK_W) + sa * row(t, PK_BB) + vv * row(t, PK_K)
                state[rows, :] = new
                yacc = yacc + new * row(t, PK_R)
                sacc = sacc + new * row(t + 1, PK_NKK)
            return yacc, sacc

        zero = jnp.zeros((VL, LANES), F32)
        yacc, sacc = lax.fori_loop(0, HD // KL_UNROLL, key_rows, (zero, zero))
        y_ref[t] = yacc
        return sacc

    sa_buf[...] = lax.fori_loop(0, tb, step, sa_buf[...])

    @pl.when(i == pl.num_programs(0) - 1)
    def _():
        sout_ref[...] = state[...]


def _rwkv_prompt(pk_t, tb):
    TP = pk_t.shape[0]
    HD = RWKV_HEAD_DIM
    assert (LEAD - N_META) % tb == 0 and LEAD % tb == 0
    skip = (LEAD - N_META) // tb
    meta_blocks = N_META // tb
    return pl.pallas_call(
        _rwkv_prompt_kernel,
        grid=(TP // tb - skip,),
        in_specs=[pl.BlockSpec((tb, 6, K_LO, LANES), lambda i: (i + skip, 0, 0, 0)),
                  pl.BlockSpec((1, 1, K_LO, LANES),
                               lambda i: (jnp.minimum((i + skip + 1) * tb, TP - 1), PK_NKK, 0, 0))],
        out_specs=(pl.BlockSpec((tb, K_LO, LANES), lambda i: (jnp.maximum(i - meta_blocks, 0), 0, 0)),
                   pl.BlockSpec((K_LO * HD, LANES), lambda i: (0, 0))),
        out_shape=(jax.ShapeDtypeStruct((TP - LEAD, K_LO, LANES), F32),
                   jax.ShapeDtypeStruct((K_LO * HD, LANES), F32)),
        scratch_shapes=[pltpu.VMEM((HD * K_LO, LANES), F32), pltpu.VMEM((K_LO, LANES), F32),
                        pltpu.VMEM((tb + 1, 5, HD, LANES), F32)],
        compiler_params=pltpu.CompilerParams(dimension_semantics=("arbitrary",)),
        name="rwkv_prompt",
    )(pk_t, pk_t)


def _lane_col(tile, h):
    lane = lax.broadcasted_iota(jnp.int32, tile.shape, 1)
    return jnp.sum(jnp.where(lane == h, tile, 0.0), axis=-1, keepdims=True)


def _rwkv_sample_kernel(s_ref, pk_ref, vt_ref, sout_ref, yt_ref):
    for b in range(s_ref.shape[0]):
        v_tile = vt_ref[b]
        lane = lax.broadcasted_iota(jnp.int32, v_tile.shape, 1)
        y_tile = jnp.zeros(v_tile.shape, F32)
        for h in range(RWKV_HEADS):
            S = s_ref[b, h]
            sa = jnp.sum(S * pk_ref[b, PK_NKK, h:h + 1, :], axis=-1, keepdims=True)
            new = (S * pk_ref[b, PK_W, h:h + 1, :]
                   + sa * pk_ref[b, PK_BB, h:h + 1, :]
                   + _lane_col(v_tile, h) * pk_ref[b, PK_K, h:h + 1, :])
            sout_ref[b, h] = new
            y = jnp.sum(new * pk_ref[b, PK_R, h:h + 1, :], axis=-1, keepdims=True)
            y_tile = jnp.where(lane == h, y, y_tile)
        yt_ref[b] = y_tile


def _rwkv_sample(state, pk_h, v_t, bt):
    N = state.shape[0]
    H, HD = RWKV_HEADS, RWKV_HEAD_DIM
    return pl.pallas_call(
        _rwkv_sample_kernel,
        grid=(N // bt,),
        in_specs=[pl.BlockSpec((bt, H, HD, HD), lambda i: (i, 0, 0, 0)),
                  pl.BlockSpec((bt, 5, H, HD), lambda i: (i, 0, 0, 0)),
                  pl.BlockSpec((bt, HD, H), lambda i: (i, 0, 0))],
        out_specs=(pl.BlockSpec((bt, H, HD, HD), lambda i: (i, 0, 0, 0)),
                   pl.BlockSpec((bt, HD, H), lambda i: (i, 0, 0))),
        out_shape=(jax.ShapeDtypeStruct(state.shape, F32),
                   jax.ShapeDtypeStruct((N, HD, H), F32)),
        compiler_params=pltpu.CompilerParams(dimension_semantics=("arbitrary",)),
        name="rwkv_sample",
    )(state, pk_h, v_t)


def _ret_sample_kernel(s_ref, qt_ref, kt_ref, v_ref, g_ref, gam_ref, sout_ref, ob_ref):
    for b in range(s_ref.shape[0]):
        q_tile = qt_ref[b]
        k_tile = kt_ref[b]
        for h in range(RET_HEADS):
            S = s_ref[b, h]
            qc = _lane_col(q_tile, h)
            kc = _lane_col(k_tile, h)
            vrow = v_ref[b, h:h + 1, :]
            gam = gam_ref[h:h + 1, :]
            qk = jnp.sum(qc * kc, axis=0, keepdims=True)
            cross = jnp.sum(qc * S, axis=0, keepdims=True)
            o = qk * vrow + cross * gam
            sout_ref[b, h] = gam * S + kc * vrow
            o = o * lax.rsqrt(jnp.mean(o * o, axis=-1, keepdims=True) + RMS_EPS)
            gr = g_ref[b, h:h + 1, :]
            ob_ref[b, h:h + 1, :] = gr * _sigmoid(gr) * o


def _ret_sample(state, q_t, k_t, v_h, g_h, gam, bt):
    N = state.shape[0]
    H, DK = RET_HEADS, RET_DIM
    col = pl.BlockSpec((bt, DK, H), lambda i: (i, 0, 0))
    rowspec = pl.BlockSpec((bt, H, DK), lambda i: (i, 0, 0))
    st = pl.BlockSpec((bt, H, DK, DK), lambda i: (i, 0, 0, 0))
    return pl.pallas_call(
        _ret_sample_kernel,
        grid=(N // bt,),
        in_specs=[st, col, col, rowspec, rowspec, pl.BlockSpec((H, DK), lambda i: (0, 0))],
        out_specs=(st, rowspec),
        out_shape=(jax.ShapeDtypeStruct(state.shape, F32), jax.ShapeDtypeStruct((N, H, DK), F32)),
        compiler_params=pltpu.CompilerParams(dimension_semantics=("arbitrary",)),
        name="ret_sample",
    )(state, q_t, k_t, v_h, g_h, gam)


def _mixtail_kernel(n_w, y_feature_major, x_ref, y_ref, g_ref, bonus_ref, ob_ref, gates_ref,
                    lnw_ref, lnb_ref, n2_ref, br_ref, *rest):
    wa = rest[0:n_w]
    wb = rest[n_w:2 * n_w]
    wo = rest[2 * n_w:3 * n_w]
    wr = rest[3 * n_w:3 * n_w + 2]
    x1_ref, u2_ref, comb_ref = rest[3 * n_w + 2:]

    if y_feature_major:
        y = jnp.concatenate([y_ref[hi, h * K_LO:(h + 1) * K_LO, :] for h in range(RWKV_HEADS) for hi in range(2)],
                            axis=0).T
    else:
        y = y_ref[...]
    inv_n = 1.0 / RWKV_HEAD_DIM
    mu = _head64_sum(y) * inv_n
    d = y - mu
    var = _head64_sum(d * d) * inv_n
    yn = d * lax.rsqrt(var + LNX_EPS) * lnw_ref[...] + lnb_ref[...]
    ya = (yn + bonus_ref[...]) * g_ref[...]
    branch_a = _mm(ya, wa)
    branch_b = _mm(ob_ref[...], wb)
    gates = gates_ref[...]
    merged = _sigmoid(gates[:, :D_MODEL]) * branch_a + _sigmoid(gates[:, D_MODEL:]) * branch_b
    x1 = x_ref[...] + _mm(merged, wo)
    x1_ref[...] = x1
    u2 = _rmsnorm(x1, n2_ref[...])
    u2_ref[...] = u2.astype(u2_ref.dtype)

    logits = _mm(u2, wr) + br_ref[...]
    lane = lax.broadcasted_iota(jnp.int32, logits.shape, 1)
    lanef = lane.astype(F32)
    neg = jnp.float32(-jnp.inf)
    big = jnp.float32(1e9)
    is_grp = (lane >= N_EXPERTS) & (lane < N_EXPERTS + N_GROUPS)
    lg = jnp.where(is_grp, logits, neg)
    lg_max = jnp.max(lg, axis=-1, keepdims=True)
    p_grp = 1.0 / jnp.sum(jnp.exp(lg - lg_max), axis=-1, keepdims=True)
    grp = jnp.min(jnp.where(lg == lg_max, lanef, big), axis=-1, keepdims=True) - N_EXPERTS
    in_grp = (lanef >= grp * EXPERTS_PER_GROUP) & (lanef < (grp + 1.0) * EXPERTS_PER_GROUP)
    le = jnp.where(in_grp, logits, neg)
    v1 = jnp.max(le, axis=-1, keepdims=True)
    i1 = jnp.min(jnp.where(le == v1, lanef, big), axis=-1, keepdims=True)
    le2 = jnp.where(lanef == i1, neg, le)
    v2 = jnp.max(le2, axis=-1, keepdims=True)
    i2 = jnp.min(jnp.where(le2 == v2, lanef, big), axis=-1, keepdims=True)
    e2 = jnp.exp(v2 - v1)
    w1 = 1.0 / (1.0 + e2)
    w2 = e2 / (1.0 + e2)
    comb_ref[...] = p_grp * (jnp.where(lanef == i1, w1, 0.0) + jnp.where(lanef == i2, w2, 0.0))


def _mixtail(x, y, g, bonus, ob, gates, lnw, lnb, n2, br, wa, wb, wo, wr, tm, y_feature_major):
    G, n, _ = x.shape
    assert n % tm == 0
    n_w = len(wa)
    consts = (lnw, lnb, n2, br) + tuple(wa) + tuple(wb) + tuple(wo) + tuple(wr)
    row = lambda width: pl.BlockSpec((None, tm, width), lambda b, i: (b, i, 0))
    y_spec = (pl.BlockSpec((2, None, RWKV_WIDTH // 2, tm), lambda b, i: (0, b, 0, i)) if y_feature_major
              else row(RWKV_WIDTH))
    in_specs = [row(D_MODEL), y_spec, row(RWKV_WIDTH), row(RWKV_WIDTH), row(RET_WIDTH),
                row(2 * D_MODEL)] + [_const_spec(c) for c in consts]
    return pl.pallas_call(
        functools.partial(_mixtail_kernel, n_w, y_feature_major),
        grid=(G, n // tm),
        in_specs=in_specs,
        out_specs=(row(D_MODEL), row(D_MODEL), row(LANES)),
        out_shape=(jax.ShapeDtypeStruct((G, n, D_MODEL), F32),
                   jax.ShapeDtypeStruct((G, n, D_MODEL), BF16),
                   jax.ShapeDtypeStruct((G, n, LANES), F32)),
        compiler_params=pltpu.CompilerParams(
            dimension_semantics=("arbitrary", "arbitrary"), vmem_limit_bytes=VMEM_LIMIT),
        name="mixtail",
    )(x, y, g, bonus, ob, gates, *consts)


def _moe_kernel(x1_ref, u2_ref, comb_ref, wg_ref, wu_ref, wd_ref, fg_ref, o_ref, acc):
    grp = pl.program_id(1)

    @pl.when(grp == 0)
    def _():
        acc[...] = jnp.zeros(acc.shape, F32)

    u2 = u2_ref[...]
    comb = comb_ref[...]
    lane = lax.broadcasted_iota(jnp.int32, comb.shape, 1)
    total = acc[...]
    for j in range(EXPERTS_PER_GROUP):
        c = jnp.sum(jnp.where(lane == grp * EXPERTS_PER_GROUP + j, comb, 0.0), axis=-1, keepdims=True)
        hg = jnp.dot(u2, wg_ref[j], preferred_element_type=F32)
        hu = jnp.dot(u2, wu_ref[j], preferred_element_type=F32)
        h = hg * _sigmoid(hg) * hu * c
        total = total + jnp.dot(h.astype(BF16), wd_ref[j], preferred_element_type=F32)
    acc[...] = total

    @pl.when(grp == pl.num_programs(1) - 1)
    def _():
        o_ref[...] = _rmsnorm(x1_ref[...] + acc[...], fg_ref[...])


def _moe(x1, u2, comb, wg, wu, wd, fg, tm):
    N = x1.shape[0]
    E = EXPERTS_PER_GROUP
    return pl.pallas_call(
        _moe_kernel,
        grid=(N // tm, N_GROUPS),
        in_specs=[pl.BlockSpec((tm, D_MODEL), lambda i, g: (i, 0)),
                  pl.BlockSpec((tm, D_MODEL), lambda i, g: (i, 0)),
                  pl.BlockSpec((tm, LANES), lambda i, g: (i, 0)),
                  pl.BlockSpec((E, D_MODEL, EXPERT_FF), lambda i, g: (g, 0, 0)),
                  pl.BlockSpec((E, D_MODEL, EXPERT_FF), lambda i, g: (g, 0, 0)),
                  pl.BlockSpec((E, EXPERT_FF, D_MODEL), lambda i, g: (g, 0, 0)),
                  pl.BlockSpec((1, D_MODEL), lambda i, g: (0, 0))],
        out_specs=pl.BlockSpec((tm, D_MODEL), lambda i, g: (i, 0)),
        out_shape=jax.ShapeDtypeStruct((N, D_MODEL), F32),
        scratch_shapes=[pltpu.VMEM((tm, D_MODEL), F32)],
        compiler_params=pltpu.CompilerParams(
            dimension_semantics=("arbitrary", "arbitrary"), vmem_limit_bytes=VMEM_LIMIT),
        name="moe",
    )(x1, u2, comb, wg, wu, wd, fg)


def _pieces(w, n):
    return tuple(_split_bf16(w.astype(F32), n))


def _rope_tables(pos):
    half = RET_DIM // 2
    inv = ROPE_BASE ** (-jnp.arange(half, dtype=F32) / half)
    ang = pos.astype(F32)[:, None] * inv[None, :]
    c = jnp.cos(ang)
    s = jnp.sin(ang)
    cc = jnp.repeat(c, 2, axis=-1)
    ss = jnp.stack([-s, s], axis=-1).reshape(pos.shape[0], RET_DIM)
    return cc, ss


def _retention_tables():
    C = RET_CHUNK
    log_gamma = jnp.log(1.0 - 2.0 ** (-5.0 - jnp.arange(RET_HEADS, dtype=F32)))
    idx = jnp.arange(C, dtype=F32)
    diff = idx[:, None] - idx[None, :]
    lg = log_gamma[:, None, None]
    dmask = jnp.where(diff[None] >= 0, jnp.exp(jnp.maximum(diff, 0.0)[None] * lg), 0.0)
    q_dec = jnp.exp((idx[None, :] + 1.0) * log_gamma[:, None])
    k_dec = jnp.exp((C - 1.0 - idx[None, :]) * log_gamma[:, None])
    bc = lambda a: jnp.broadcast_to(a[:, :, None], (RET_HEADS, C, C))
    rt_tab = jnp.stack([dmask, bc(q_dec), bc(k_dec)], axis=1)
    gc_tab = jnp.broadcast_to(jnp.exp(C * log_gamma)[:, None, None], (RET_HEADS, SUBLANES, RET_DIM))
    gam1 = jnp.broadcast_to(jnp.exp(log_gamma)[:, None], (RET_HEADS, RET_DIM))
    return rt_tab, gc_tab, gam1


def kernel(x_prompt, x_sample, state_wkv, state_shift, state_ret, meta_tokens, norm1_g, w_in, shift_mu, decay_w0, decay_up, aaa_a0, aaa_up, gate_up, k_k, k_a, r_k, lnx_w, lnx_b, w_branch_a, w_branch_b, w_out, norm2_g, router_group_w, router_group_b, router_expert_w, router_expert_b, expert_w_gate, expert_w_up, expert_w_down, final_norm_g):
    B, T, D = x_prompt.shape
    NS = x_sample.shape[0]
    TP = LEAD + T
    H, HD = RWKV_HEADS, RWKV_HEAD_DIM
    l = 0
    row = lambda a: a.reshape(1, -1)

    zpad = jnp.zeros((LORA_PAD - 64, RWKV_WIDTH), F32)
    dup_pad = jnp.concatenate([decay_up[l], zpad], axis=0)
    aup_pad = jnp.concatenate([zpad, aaa_up[l]], axis=0)
    wr_full = jnp.concatenate([router_expert_w[l], router_group_w[l]], axis=1)
    wr_full = jnp.pad(wr_full, ((0, 0), (0, LANES - wr_full.shape[1])))
    br_full = jnp.pad(jnp.concatenate([router_expert_b[l], router_group_b[l]]), (0, LANES - N_EXPERTS - N_GROUPS))
    wr = _pieces(wr_full, 2)
    wg = expert_w_gate[l].astype(BF16)
    wu = expert_w_up[l].astype(BF16)
    wd = expert_w_down[l].astype(BF16)
    rt_tab, gc_tab, gam1 = _retention_tables()

    def prep_params(n):
        return (row(shift_mu[l]), row(decay_w0[l]), _pieces(dup_pad, n), row(aaa_a0[l]), _pieces(aup_pad, n),
                _pieces(gate_up[l], n), row(k_k[l]), row(k_a[l]), row(r_k[l]))

    def flat_params(p):
        out = []
        for a in p:
            out.extend(a if isinstance(a, tuple) else (a,))
        return tuple(out)

    def tail(n, x, y, g, bonus, ob, gates, tm_tail, tm_moe, y_feature_major):
        x1, u2, comb = _mixtail(x, y, g, bonus, ob, gates, row(lnx_w[l]), row(lnx_b[l]), row(norm2_g[l]),
                                row(br_full), _pieces(w_branch_a[l], n), _pieces(w_branch_b[l], n),
                                _pieces(w_out[l], n), wr, tm_tail, y_feature_major)
        flat = lambda a: a.reshape(-1, a.shape[-1])
        return _moe(flat(x1), flat(u2), flat(comb), wg, wu, wd, row(final_norm_g), tm_moe)

    lead = jnp.concatenate([jnp.zeros((LEAD - N_META, D), F32), meta_tokens.astype(F32)], axis=0)
    cc_p, ss_p = _rope_tables(jnp.arange(TP, dtype=jnp.int32) - (LEAD - N_META))
    pk, g, bonus, gates, ob, shift_p, ret_p = _inproj_prompt(
        x_prompt, lead, cc_p, ss_p, rt_tab, gc_tab, row(norm1_g[l]), _pieces(w_in[l], 1),
        flat_params(prep_params(1)), nb=4)

    pk_t = pk.reshape(6, 2 * B * H, K_LO, TP).transpose(3, 0, 2, 1)

    pk, xs = lax.optimization_barrier((pk, x_sample.reshape(NS, D)))
    cc_s, ss_s = _rope_tables(jnp.full((NS,), PAST_LEN, jnp.int32))
    pk_s, v_s, g_s, bonus_s, gates_s, retq, shift_s = _inproj_sample(
        xs, state_shift[l], cc_s, ss_s, row(norm1_g[l]), _pieces(w_in[l], 2), flat_params(prep_params(2)))
    wkv_s, yt_s = _rwkv_sample(state_wkv[l], pk_s.reshape(NS, 5, H, HD),
                               v_s.reshape(NS, H, HD).transpose(0, 2, 1), bt=8)
    y_s = yt_s.transpose(0, 2, 1).reshape(NS, RWKV_WIDTH)
    heads = lambda a: a.reshape(NS, RET_HEADS, RET_DIM)
    q_s, k_s, v_r, g_r = (retq[:, i * RET_WIDTH:(i + 1) * RET_WIDTH] for i in range(4))
    ret_s, ob_s = _ret_sample(state_ret[l], heads(q_s).transpose(0, 2, 1), heads(k_s).transpose(0, 2, 1),
                              heads(v_r), heads(g_r), gam1, bt=8)
    y_sample = tail(2, xs[None], y_s[None], g_s[None], bonus_s[None], ob_s.reshape(1, NS, RET_WIDTH), gates_s[None],
                    NS, NS, False)

    pk_t, y_sample, wkv_s, ret_s = lax.optimization_barrier((pk_t, y_sample, wkv_s, ret_s))
    y_t, s_fin = _rwkv_prompt(pk_t, tb=16)
    y_fm = y_t.transpose(2, 1, 0).reshape(2, B, RWKV_WIDTH // 2, T)
    wkv_p = s_fin.reshape(HD, K_LO, 2, B, H).transpose(3, 4, 2, 1, 0).reshape(B, H, HD, HD)
    y_prompt = tail(1, x_prompt, y_fm, g, bonus, ob, gates, 512, 1024, True).reshape(B, T, D)

    return (y_prompt, y_sample.reshape(NS, 1, D),
            wkv_p[None], shift_p.reshape(1, B, SHIFT_WIDTH), ret_p[None],
            wkv_s[None], shift_s[None], ret_s[None])
```

```python
import functools

import jax
import jax.numpy as jnp
from jax import lax
from jax.experimental import pallas as pl
from jax.experimental.pallas import tpu as pltpu

F32 = jnp.float32
BF16 = jnp.bfloat16

D_MODEL = 1024
N_META = 16
RWKV_WIDTH = 512
RWKV_HEADS = 8
RWKV_HEAD_DIM = 64
LORA_PAD = 128
GATE_LORA = 128
SHIFT_WIDTH = 3 * RWKV_WIDTH + 64 + 64 + GATE_LORA
LNX_EPS = 64e-5
RET_WIDTH = 512
RET_HEADS = 4
RET_DIM = 128
RET_CHUNK = 128
ROPE_BASE = 10000.0
RET_COL0 = SHIFT_WIDTH
GATE_COL0 = SHIFT_WIDTH + 4 * RET_WIDTH
IN_WIDTH = GATE_COL0 + 2 * D_MODEL
N_GROUPS = 4
EXPERTS_PER_GROUP = 4
N_EXPERTS = 16
EXPERT_FF = 256
RMS_EPS = 1e-6
PAST_LEN = 16384

LANES = 128
SUBLANES = 8
VMEM_LIMIT = 60 * 1024 * 1024

LEAD = 128
PK_W, PK_NKK, PK_BB, PK_K, PK_R, PK_V = range(6)
K_LO = 32
KL_UNROLL = 8


def _split_bf16(a, nterms):
    pieces = []
    rem = a
    for _ in range(nterms - 1):
        c = rem * 65537.0
        hi = c - (c - rem)
        pieces.append(hi.astype(BF16))
        rem = rem - hi
    pieces.append(rem.astype(BF16))
    return pieces


def _mm(a, w_refs, cols=None):
    def w(i):
        r = w_refs[i]
        return r[...] if cols is None else r[:, cols[0]:cols[1]]

    if a.dtype == BF16:
        acc = jnp.dot(a, w(0), preferred_element_type=F32)
        if len(w_refs) > 1:
            acc = acc + jnp.dot(a, w(1), preferred_element_type=F32)
        return acc
    if len(w_refs) == 1:
        return jnp.dot(a.astype(BF16), w(0), preferred_element_type=F32)
    a_hi, a_lo = _split_bf16(a, 2)
    acc = jnp.dot(a_hi, w(0), preferred_element_type=F32)
    acc = acc + jnp.dot(a_lo, w(0), preferred_element_type=F32)
    acc = acc + jnp.dot(a_hi, w(1), preferred_element_type=F32)
    return acc


def _rmsnorm(x, g):
    return x * lax.rsqrt(jnp.mean(x * x, axis=-1, keepdims=True) + RMS_EPS) * g


def _sigmoid(x):
    return 1.0 / (1.0 + jnp.exp(-x))


def _head64_sum(x):
    lane = lax.broadcasted_iota(jnp.int32, (x.shape[0], LANES), 1)
    lo = lane < RWKV_HEAD_DIM
    outs = []
    for j in range(x.shape[1] // LANES):
        xj = x[:, j * LANES:(j + 1) * LANES]
        s_lo = jnp.sum(jnp.where(lo, xj, 0.0), axis=-1, keepdims=True)
        s_hi = jnp.sum(jnp.where(lo, 0.0, xj), axis=-1, keepdims=True)
        outs.append(jnp.where(lo, s_lo, s_hi))
    return jnp.concatenate(outs, axis=-1)


def _rope(x, cc, ss):
    lane = lax.broadcasted_iota(jnp.int32, x.shape, 1)
    even = (lane & 1) == 0
    swapped = jnp.where(even, pltpu.roll(x, LANES - 1, 1), pltpu.roll(x, 1, 1))
    return x * cc + swapped * ss


def _rwkv_prep(s, prev, prm):
    (mu, w0, dup, a0, aup, gup, k_k, k_a, r_k) = prm
    xs = s + mu[...] * (prev - s)
    W = RWKV_WIDTH
    r = xs[:, 0:W]
    k = xs[:, W:2 * W]
    v = xs[:, 2 * W:3 * W]
    wa = xs[:, 3 * W:3 * W + LORA_PAD]
    gd = xs[:, 3 * W + LORA_PAD:]
    dec = _mm(jnp.tanh(wa), dup)
    z = -(w0[...] + dec)
    softplus = jnp.maximum(z, 0.0) + jnp.log1p(jnp.exp(-jnp.abs(z)))
    decay = jnp.exp(-jnp.exp(-softplus - 0.5))
    a = _sigmoid(a0[...] + _mm(wa, aup))
    g = _mm(_sigmoid(gd), gup)
    kk = k * k_k[...]
    kk = kk / jnp.maximum(jnp.sqrt(_head64_sum(kk * kk)), 1e-12)
    k_mod = k * (1.0 + (a - 1.0) * k_a[...])
    bonus = _head64_sum(r * k_mod * r_k[...]) * v
    return r, decay, k_mod, v, kk, a, g, bonus


def _take_prep_params(refs, n_w):
    it = iter(refs)
    take = lambda n: tuple(next(it) for _ in range(n))
    (mu, w0), dup, (a0,), aup, gup, (k_k, k_a, r_k) = take(2), take(n_w), take(1), take(n_w), take(n_w), take(3)
    return (mu, w0, dup, a0, aup, gup, k_k, k_a, r_k), tuple(it)


def _store_rwkv_operands(ops, pk_ref, v_ref, g_ref, bonus_ref):
    r, decay, k_mod, v, kk, a, g, bonus = ops
    W = RWKV_WIDTH
    pk_ref[:, PK_W * W:(PK_W + 1) * W] = decay
    pk_ref[:, PK_NKK * W:(PK_NKK + 1) * W] = -kk
    pk_ref[:, PK_BB * W:(PK_BB + 1) * W] = kk * a
    pk_ref[:, PK_K * W:(PK_K + 1) * W] = k_mod
    pk_ref[:, PK_R * W:(PK_R + 1) * W] = r
    v_ref[...] = v
    g_ref[...] = g
    bonus_ref[...] = bonus


def _inproj_prompt_kernel(n_w, x_ref, cc_ref, ss_ref, lead_ref, rt_ref, gc_ref, n1_ref, *rest):
    w_in = rest[:n_w]
    prm, rest = _take_prep_params(rest[n_w:], n_w)
    (pk_ref, g_ref, bonus_ref, gates_ref, ob_ref, shift_ref, ret_ref, prev_buf, ret_state) = rest
    t = pl.program_id(1)
    nb, tm, _ = x_ref.shape
    rows = [slice(b * tm, (b + 1) * tm) for b in range(nb)]

    @pl.when(t == 0)
    def _():
        prev_buf[:, 0:SUBLANES, :] = jnp.zeros((nb, SUBLANES, SHIFT_WIDTH), F32)
        ret_state[...] = jnp.zeros(ret_state.shape, F32)

    x = jnp.where(t == 0, lead_ref[...][None], x_ref[...])
    u = _rmsnorm(x.reshape(nb * tm, D_MODEL), n1_ref[...])
    u_b = u.astype(BF16) if n_w == 1 else u

    s = _mm(u_b, w_in, (0, SHIFT_WIDTH))
    prevs = []
    for b in range(nb):
        s_b = s[rows[b]]
        prev_buf[b, SUBLANES:SUBLANES + tm, :] = s_b
        prevs.append(prev_buf[b, SUBLANES - 1:SUBLANES - 1 + tm, :])
        prev_buf[b, SUBLANES - 1:SUBLANES, :] = s_b[tm - 1:tm, :]
        shift_ref[b] = s_b[tm - 1:tm, :]
    prev = jnp.concatenate(prevs, axis=0)

    r, decay, k_mod, v, kk, a, g, bonus = _rwkv_prep(s, prev, prm)
    W = RWKV_WIDTH
    for b in range(nb):
        rb = rows[b]
        for q, val in ((PK_W, decay), (PK_NKK, -kk), (PK_BB, kk * a), (PK_K, k_mod), (PK_R, r), (PK_V, v)):
            fm = val[rb].T
            for hi in range(2):
                pk_ref[q, hi, b] = jnp.concatenate(
                    [fm[h * RWKV_HEAD_DIM + hi * K_LO:h * RWKV_HEAD_DIM + (hi + 1) * K_LO] for h in range(RWKV_HEADS)],
                    axis=0)
        g_ref[b] = g[rb]
        bonus_ref[b] = bonus[rb]

    gates = _mm(u_b, w_in, (GATE_COL0, IN_WIDTH))
    for b in range(nb):
        gates_ref[b] = gates[rows[b]]

    ret = _mm(u_b, w_in, (RET_COL0, GATE_COL0))
    cc = cc_ref[...]
    ss = ss_ref[...]
    for b in range(nb):
        rb = rows[b]
        for h in range(RET_HEADS):
            c0 = h * RET_DIM
            qc = _rope(ret[rb, c0:c0 + RET_DIM], cc, ss).astype(BF16)
            kc = _rope(ret[rb, RET_WIDTH + c0:RET_WIDTH + c0 + RET_DIM], cc, ss) * (RET_DIM ** -0.5)
            vc = ret[rb, 2 * RET_WIDTH + c0:2 * RET_WIDTH + c0 + RET_DIM].astype(BF16)
            gr = ret[rb, 3 * RET_WIDTH + c0:3 * RET_WIDTH + c0 + RET_DIM]
            state = ret_state[b, h]
            scores = lax.dot_general(qc, kc.astype(BF16), (((1,), (1,)), ((), ())),
                                     preferred_element_type=F32) * rt_ref[h, 0]
            intra = jnp.dot(scores.astype(BF16), vc, preferred_element_type=F32)
            cross = jnp.dot(qc, state.astype(BF16), preferred_element_type=F32) * rt_ref[h, 1]
            kt = (kc * rt_ref[h, 2]).T.astype(BF16)
            ret_state[b, h] = gc_ref[h, 0:1, :] * state + jnp.dot(kt, vc, preferred_element_type=F32)
            o = intra + cross
            o = o * lax.rsqrt(jnp.mean(o * o, axis=-1, keepdims=True) + RMS_EPS)
            ob_ref[b, :, c0:c0 + RET_DIM] = (gr * _sigmoid(gr) * o).astype(ob_ref.dtype)

    @pl.when(t == pl.num_programs(1) - 1)
    def _():
        ret_ref[...] = ret_state[...]


def _inproj_sample_kernel(n_w, x_ref, prev_ref, cc_ref, ss_ref, n1_ref, *rest):
    w_in = rest[:n_w]
    prm, rest = _take_prep_params(rest[n_w:], n_w)
    (pk_ref, v_ref, g_ref, bonus_ref, gates_ref, retq_ref, shift_ref) = rest
    u = _rmsnorm(x_ref[...], n1_ref[...])
    s = _mm(u, w_in, (0, SHIFT_WIDTH))
    shift_ref[...] = s
    _store_rwkv_operands(_rwkv_prep(s, prev_ref[...], prm), pk_ref, v_ref, g_ref, bonus_ref)
    gates_ref[...] = _mm(u, w_in, (GATE_COL0, IN_WIDTH))
    ret = _mm(u, w_in, (RET_COL0, GATE_COL0))
    cc = cc_ref[...]
    ss = ss_ref[...]
    for h in range(RET_HEADS):
        c0 = h * RET_DIM
        retq_ref[:, c0:c0 + RET_DIM] = _rope(ret[:, c0:c0 + RET_DIM], cc, ss)
        retq_ref[:, RET_WIDTH + c0:RET_WIDTH + c0 + RET_DIM] = (
            _rope(ret[:, RET_WIDTH + c0:RET_WIDTH + c0 + RET_DIM], cc, ss) * (RET_DIM ** -0.5))
    retq_ref[:, 2 * RET_WIDTH:] = ret[:, 2 * RET_WIDTH:]


def _const_spec(arr):
    nd = arr.ndim
    return pl.BlockSpec(arr.shape, lambda *_: (0,) * nd, pipeline_mode=pl.Buffered(1))


def _inproj_prompt(x, lead, cc, ss, rt_tab, gc_tab, n1, w_in, prm, nb):
    B, T, _ = x.shape
    tm = RET_CHUNK
    assert lead.shape[0] == tm and T % tm == 0 and B % nb == 0
    TP = tm + T
    n_w = len(w_in)
    consts = (lead, rt_tab, gc_tab, n1) + tuple(w_in) + tuple(prm)
    prow = lambda width: pl.BlockSpec((nb, tm, width), lambda b, t: (b, jnp.maximum(t - 1, 0), 0))
    in_specs = [prow(D_MODEL),
                pl.BlockSpec((tm, LANES), lambda b, t: (t, 0)),
                pl.BlockSpec((tm, LANES), lambda b, t: (t, 0))] + [_const_spec(c) for c in consts]
    out_shape = (
        jax.ShapeDtypeStruct((6, 2, B, RWKV_WIDTH // 2, TP), F32),
        jax.ShapeDtypeStruct((B, T, RWKV_WIDTH), F32),
        jax.ShapeDtypeStruct((B, T, RWKV_WIDTH), F32),
        jax.ShapeDtypeStruct((B, T, 2 * D_MODEL), F32),
        jax.ShapeDtypeStruct((B, T, RET_WIDTH), BF16),
        jax.ShapeDtypeStruct((B, 1, SHIFT_WIDTH), F32),
        jax.ShapeDtypeStruct((B, RET_HEADS, RET_DIM, RET_DIM), F32),
    )
    out_specs = (
        pl.BlockSpec((6, 2, nb, RWKV_WIDTH // 2, tm), lambda b, t: (0, 0, b, 0, t)),
        prow(RWKV_WIDTH), prow(RWKV_WIDTH), prow(2 * D_MODEL), prow(RET_WIDTH),
        pl.BlockSpec((nb, 1, SHIFT_WIDTH), lambda b, t: (b, 0, 0)),
        pl.BlockSpec((nb, RET_HEADS, RET_DIM, RET_DIM), lambda b, t: (b, 0, 0, 0)),
    )
    return pl.pallas_call(
        functools.partial(_inproj_prompt_kernel, n_w),
        grid=(B // nb, TP // tm),
        in_specs=in_specs,
        out_specs=out_specs,
        out_shape=out_shape,
        scratch_shapes=[pltpu.VMEM((nb, tm + SUBLANES, SHIFT_WIDTH), F32),
                        pltpu.VMEM((nb, RET_HEADS, RET_DIM, RET_DIM), F32)],
        compiler_params=pltpu.CompilerParams(
            dimension_semantics=("arbitrary", "arbitrary"), vmem_limit_bytes=VMEM_LIMIT),
        name="inproj_prompt",
    )(x, cc, ss, *consts)


def _inproj_sample(xs, prev, cc, ss, n1, w_in, prm):
    N = xs.shape[0]
    n_w = len(w_in)
    args = (xs, prev, cc, ss, n1) + tuple(w_in) + tuple(prm)
    out_shape = (
        jax.ShapeDtypeStruct((N, 5 * RWKV_WIDTH), F32),
        jax.ShapeDtypeStruct((N, RWKV_WIDTH), F32),
        jax.ShapeDtypeStruct((N, RWKV_WIDTH), F32),
        jax.ShapeDtypeStruct((N, RWKV_WIDTH), F32),
        jax.ShapeDtypeStruct((N, 2 * D_MODEL), F32),
        jax.ShapeDtypeStruct((N, 4 * RET_WIDTH), F32),
        jax.ShapeDtypeStruct((N, SHIFT_WIDTH), F32),
    )
    return pl.pallas_call(
        functools.partial(_inproj_sample_kernel, n_w),
        out_shape=out_shape,
        compiler_params=pltpu.CompilerParams(vmem_limit_bytes=VMEM_LIMIT),
        name="inproj_sample",
    )(*args)


def _rwkv_prompt_kernel(pk_ref, nxt_ref, y_ref, sout_ref, state, sa_buf, kd):
    i = pl.program_id(0)
    tb = pk_ref.shape[0]
    HD = RWKV_HEAD_DIM
    VL = K_LO

    @pl.when(i == 0)
    def _():
        state[...] = jnp.zeros(state.shape, F32)
        sa_buf[...] = jnp.zeros(sa_buf.shape, F32)

    first_half = lax.broadcasted_iota(jnp.int32, (K_LO, LANES), 1) < LANES // 2

    def spread(x, t, q):
        swapped = pltpu.roll(x, LANES // 2, 1)
        kd[t, q, 0:K_LO, :] = jnp.where(first_half, x, swapped)
        kd[t, q, K_LO:HD, :] = jnp.where(first_half, swapped, x)

    for t in range(tb):
        for q in (PK_W, PK_NKK, PK_BB, PK_K, PK_R):
            spread(pk_ref[t, q], t, q)
    spread(nxt_ref[0, 0], tb, PK_NKK)

    def step(t, sa):
        vv = pk_ref[t, PK_V]

        def key_rows(c, accs):
            yacc, sacc = accs
            for j in range(KL_UNROLL):
                k = c * KL_UNROLL + j
                row = lambda tt, q: kd[tt, q, pl.ds(k, 1), :]
                rows = pl.ds(pl.multiple_of(k * VL, VL), VL)
                new = state[rows, :] * row(t, PK_W) + sa * row(t, PK_BB) + vv * row(t, PK_K)
                state[rows, :] = new
                yacc = yacc + new * row(t, PK_R)
                sacc = sacc + new * row(t + 1, PK_NKK)
            return yacc, sacc

        zero = jnp.zeros((VL, LANES), F32)
        yacc, sacc = lax.fori_loop(0, HD // KL_UNROLL, key_rows, (zero, zero))
        y_ref[t] = yacc
        return sacc

    sa_buf[...] = lax.fori_loop(0, tb, step, sa_buf[...])

    @pl.when(i == pl.num_programs(0) - 1)
    def _():
        sout_ref[...] = state[...]


def _rwkv_prompt(pk_t, tb):
    TP = pk_t.shape[0]
    HD = RWKV_HEAD_DIM
    assert (LEAD - N_META) % tb == 0 and LEAD % tb == 0
    skip = (LEAD - N_META) // tb
    meta_blocks = N_META // tb
    return pl.pallas_call(
        _rwkv_prompt_kernel,
        grid=(TP // tb - skip,),
        in_specs=[pl.BlockSpec((tb, 6, K_LO, LANES), lambda i: (i + skip, 0, 0, 0)),
                  pl.BlockSpec((1, 1, K_LO, LANES),
                               lambda i: (jnp.minimum((i + skip + 1) * tb, TP - 1), PK_NKK, 0, 0))],
        out_specs=(pl.BlockSpec((tb, K_LO, LANES), lambda i: (jnp.maximum(i - meta_blocks, 0), 0, 0)),
                   pl.BlockSpec((K_LO * HD, LANES), lambda i: (0, 0))),
        out_shape=(jax.ShapeDtypeStruct((TP - LEAD, K_LO, LANES), F32),
                   jax.ShapeDtypeStruct((K_LO * HD, LANES), F32)),
        scratch_shapes=[pltpu.VMEM((HD * K_LO, LANES), F32), pltpu.VMEM((K_LO, LANES), F32),
                        pltpu.VMEM((tb + 1, 5, HD, LANES), F32)],
        compiler_params=pltpu.CompilerParams(dimension_semantics=("arbitrary",)),
        name="rwkv_prompt",
    )(pk_t, pk_t)


def _lane_col(tile, h):
    lane = lax.broadcasted_iota(jnp.int32, tile.shape, 1)
    return jnp.sum(jnp.where(lane == h, tile, 0.0), axis=-1, keepdims=True)


def _rwkv_sample_kernel(s_ref, pk_ref, vt_ref, sout_ref, yt_ref):
    for b in range(s_ref.shape[0]):
        v_tile = vt_ref[b]
        lane = lax.broadcasted_iota(jnp.int32, v_tile.shape, 1)
        row = lambda q: pk_ref[b, q][:, None, :]
        v_cols = jnp.stack([_lane_col(v_tile, h) for h in range(RWKV_HEADS)], axis=0)
        S = s_ref[b]
        sa = jnp.sum(S * row(PK_NKK), axis=-1, keepdims=True)
        new = S * row(PK_W) + sa * row(PK_BB) + v_cols * row(PK_K)
        sout_ref[b] = new
        y = jnp.sum(new * row(PK_R), axis=-1, keepdims=True)
        y_tile = jnp.zeros(v_tile.shape, F32)
        for h in range(RWKV_HEADS):
            y_tile = jnp.where(lane == h, y[h], y_tile)
        yt_ref[b] = y_tile


def _rwkv_sample(state, pk_h, v_t, bt):
    N = state.shape[0]
    H, HD = RWKV_HEADS, RWKV_HEAD_DIM
    return pl.pallas_call(
        _rwkv_sample_kernel,
        grid=(N // bt,),
        in_specs=[pl.BlockSpec((bt, H, HD, HD), lambda i: (i, 0, 0, 0)),
                  pl.BlockSpec((bt, 5, H, HD), lambda i: (i, 0, 0, 0)),
                  pl.BlockSpec((bt, HD, H), lambda i: (i, 0, 0))],
        out_specs=(pl.BlockSpec((bt, H, HD, HD), lambda i: (i, 0, 0, 0)),
                   pl.BlockSpec((bt, HD, H), lambda i: (i, 0, 0))),
        out_shape=(jax.ShapeDtypeStruct(state.shape, F32),
                   jax.ShapeDtypeStruct((N, HD, H), F32)),
        compiler_params=pltpu.CompilerParams(dimension_semantics=("arbitrary",)),
        name="rwkv_sample",
    )(state, pk_h, v_t)


def _ret_sample_kernel(s_ref, qt_ref, kt_ref, v_ref, g_ref, gam_ref, sout_ref, ob_ref):
    gam = gam_ref[...][:, None, :]
    for b in range(s_ref.shape[0]):
        q_tile = qt_ref[b]
        k_tile = kt_ref[b]
        qc = jnp.stack([_lane_col(q_tile, h) for h in range(RET_HEADS)], axis=0)
        kc = jnp.stack([_lane_col(k_tile, h) for h in range(RET_HEADS)], axis=0)
        vrow = v_ref[b][:, None, :]
        S = s_ref[b]
        qk = jnp.sum(qc * kc, axis=1, keepdims=True)
        cross = jnp.sum(qc * S, axis=1, keepdims=True)
        o = qk * vrow + cross * gam
        sout_ref[b] = gam * S + kc * vrow
        o = o * lax.rsqrt(jnp.mean(o * o, axis=-1, keepdims=True) + RMS_EPS)
        gr = g_ref[b][:, None, :]
        ob_ref[b] = (gr * _sigmoid(gr) * o)[:, 0, :]


def _ret_sample(state, q_t, k_t, v_h, g_h, gam, bt):
    N = state.shape[0]
    H, DK = RET_HEADS, RET_DIM
    col = pl.BlockSpec((bt, DK, H), lambda i: (i, 0, 0))
    rowspec = pl.BlockSpec((bt, H, DK), lambda i: (i, 0, 0))
    st = pl.BlockSpec((bt, H, DK, DK), lambda i: (i, 0, 0, 0))
    return pl.pallas_call(
        _ret_sample_kernel,
        grid=(N // bt,),
        in_specs=[st, col, col, rowspec, rowspec, pl.BlockSpec((H, DK), lambda i: (0, 0))],
        out_specs=(st, rowspec),
        out_shape=(jax.ShapeDtypeStruct(state.shape, F32), jax.ShapeDtypeStruct((N, H, DK), F32)),
        compiler_params=pltpu.CompilerParams(dimension_semantics=("arbitrary",)),
        name="ret_sample",
    )(state, q_t, k_t, v_h, g_h, gam)


def _mixtail_kernel(n_w, y_feature_major, x_ref, y_ref, g_ref, bonus_ref, ob_ref, gates_ref,
                    lnw_ref, lnb_ref, n2_ref, br_ref, *rest):
    wa = rest[0:n_w]
    wb = rest[n_w:2 * n_w]
    wo = rest[2 * n_w:3 * n_w]
    wr = rest[3 * n_w:3 * n_w + 2]
    x1_ref, u2_ref, comb_ref = rest[3 * n_w + 2:]

    if y_feature_major:
        y = jnp.concatenate([y_ref[hi, h * K_LO:(h + 1) * K_LO, :] for h in range(RWKV_HEADS) for hi in range(2)],
                            axis=0).T
    else:
        y = y_ref[...]
    inv_n = 1.0 / RWKV_HEAD_DIM
    mu = _head64_sum(y) * inv_n
    d = y - mu
    var = _head64_sum(d * d) * inv_n
    yn = d * lax.rsqrt(var + LNX_EPS) * lnw_ref[...] + lnb_ref[...]
    ya = (yn + bonus_ref[...]) * g_ref[...]
    branch_a = _mm(ya, wa)
    branch_b = _mm(ob_ref[...], wb)
    gates = gates_ref[...]
    merged = _sigmoid(gates[:, :D_MODEL]) * branch_a + _sigmoid(gates[:, D_MODEL:]) * branch_b
    x1 = x_ref[...] + _mm(merged, wo)
    x1_ref[...] = x1
    u2 = _rmsnorm(x1, n2_ref[...])
    u2_ref[...] = u2.astype(u2_ref.dtype)

    logits = _mm(u2, wr) + br_ref[...]
    lane = lax.broadcasted_iota(jnp.int32, logits.shape, 1)
    lanef = lane.astype(F32)
    neg = jnp.float32(-jnp.inf)
    big = jnp.float32(1e9)
    is_grp = (lane >= N_EXPERTS) & (lane < N_EXPERTS + N_GROUPS)
    lg = jnp.where(is_grp, logits, neg)
    lg_max = jnp.max(lg, axis=-1, keepdims=True)
    p_grp = 1.0 / jnp.sum(jnp.exp(lg - lg_max), axis=-1, keepdims=True)
    grp = jnp.min(jnp.where(lg == lg_max, lanef, big), axis=-1, keepdims=True) - N_EXPERTS
    in_grp = (lanef >= grp * EXPERTS_PER_GROUP) & (lanef < (grp + 1.0) * EXPERTS_PER_GROUP)
    le = jnp.where(in_grp, logits, neg)
    v1 = jnp.max(le, axis=-1, keepdims=True)
    i1 = jnp.min(jnp.where(le == v1, lanef, big), axis=-1, keepdims=True)
    le2 = jnp.where(lanef == i1, neg, le)
    v2 = jnp.max(le2, axis=-1, keepdims=True)
    i2 = jnp.min(jnp.where(le2 == v2, lanef, big), axis=-1, keepdims=True)
    e2 = jnp.exp(v2 - v1)
    w1 = 1.0 / (1.0 + e2)
    w2 = e2 / (1.0 + e2)
    comb_ref[...] = p_grp * (jnp.where(lanef == i1, w1, 0.0) + jnp.where(lanef == i2, w2, 0.0))


def _mixtail(x, y, g, bonus, ob, gates, lnw, lnb, n2, br, wa, wb, wo, wr, tm, y_feature_major):
    G, n, _ = x.shape
    assert n % tm == 0
    n_w = len(wa)
    consts = (lnw, lnb, n2, br) + tuple(wa) + tuple(wb) + tuple(wo) + tuple(wr)
    row = lambda width: pl.BlockSpec((None, tm, width), lambda b, i: (b, i, 0))
    y_spec = (pl.BlockSpec((2, None, RWKV_WIDTH // 2, tm), lambda b, i: (0, b, 0, i)) if y_feature_major
              else row(RWKV_WIDTH))
    in_specs = [row(D_MODEL), y_spec, row(RWKV_WIDTH), row(RWKV_WIDTH), row(RET_WIDTH),
                row(2 * D_MODEL)] + [_const_spec(c) for c in consts]
    return pl.pallas_call(
        functools.partial(_mixtail_kernel, n_w, y_feature_major),
        grid=(G, n // tm),
        in_specs=in_specs,
        out_specs=(row(D_MODEL), row(D_MODEL), row(LANES)),
        out_shape=(jax.ShapeDtypeStruct((G, n, D_MODEL), F32),
                   jax.ShapeDtypeStruct((G, n, D_MODEL), BF16),
                   jax.ShapeDtypeStruct((G, n, LANES), F32)),
        compiler_params=pltpu.CompilerParams(
            dimension_semantics=("arbitrary", "arbitrary"), vmem_limit_bytes=VMEM_LIMIT),
        name="mixtail",
    )(x, y, g, bonus, ob, gates, *consts)


def _moe_kernel(x1_ref, u2_ref, comb_ref, wg_ref, wu_ref, wd_ref, fg_ref, o_ref, acc):
    grp = pl.program_id(1)

    @pl.when(grp == 0)
    def _():
        acc[...] = jnp.zeros(acc.shape, F32)

    u2 = u2_ref[...]
    comb = comb_ref[...]
    lane = lax.broadcasted_iota(jnp.int32, comb.shape, 1)
    total = acc[...]
    for j in range(EXPERTS_PER_GROUP):
        c = jnp.sum(jnp.where(lane == grp * EXPERTS_PER_GROUP + j, comb, 0.0), axis=-1, keepdims=True)
        hg = jnp.dot(u2, wg_ref[j], preferred_element_type=F32)
        hu = jnp.dot(u2, wu_ref[j], preferred_element_type=F32)
        h = hg * _sigmoid(hg) * hu * c
        total = total + jnp.dot(h.astype(BF16), wd_ref[j], preferred_element_type=F32)
    acc[...] = total

    @pl.when(grp == pl.num_programs(1) - 1)
    def _():
        o_ref[...] = _rmsnorm(x1_ref[...] + acc[...], fg_ref[...])


def _moe(x1, u2, comb, wg, wu, wd, fg, tm):
    N = x1.shape[0]
    E = EXPERTS_PER_GROUP
    return pl.pallas_call(
        _moe_kernel,
        grid=(N // tm, N_GROUPS),
        in_specs=[pl.BlockSpec((tm, D_MODEL), lambda i, g: (i, 0)),
                  pl.BlockSpec((tm, D_MODEL), lambda i, g: (i, 0)),
                  pl.BlockSpec((tm, LANES), lambda i, g: (i, 0)),
                  pl.BlockSpec((E, D_MODEL, EXPERT_FF), lambda i, g: (g, 0, 0)),
                  pl.BlockSpec((E, D_MODEL, EXPERT_FF), lambda i, g: (g, 0, 0)),
                  pl.BlockSpec((E, EXPERT_FF, D_MODEL), lambda i, g: (g, 0, 0)),
                  pl.BlockSpec((1, D_MODEL), lambda i, g: (0, 0))],
        out_specs=pl.BlockSpec((tm, D_MODEL), lambda i, g: (i, 0)),
        out_shape=jax.ShapeDtypeStruct((N, D_MODEL), F32),
        scratch_shapes=[pltpu.VMEM((tm, D_MODEL), F32)],
        compiler_params=pltpu.CompilerParams(
            dimension_semantics=("arbitrary", "arbitrary"), vmem_limit_bytes=VMEM_LIMIT),
        name="moe",
    )(x1, u2, comb, wg, wu, wd, fg)


def _pieces(w, n):
    return tuple(_split_bf16(w.astype(F32), n))


def _rope_tables(pos):
    half = RET_DIM // 2
    inv = ROPE_BASE ** (-jnp.arange(half, dtype=F32) / half)
    ang = pos.astype(F32)[:, None] * inv[None, :]
    c = jnp.cos(ang)
    s = jnp.sin(ang)
    cc = jnp.repeat(c, 2, axis=-1)
    ss = jnp.stack([-s, s], axis=-1).reshape(pos.shape[0], RET_DIM)
    return cc, ss


def _retention_tables():
    C = RET_CHUNK
    log_gamma = jnp.log(1.0 - 2.0 ** (-5.0 - jnp.arange(RET_HEADS, dtype=F32)))
    idx = jnp.arange(C, dtype=F32)
    diff = idx[:, None] - idx[None, :]
    lg = log_gamma[:, None, None]
    dmask = jnp.where(diff[None] >= 0, jnp.exp(jnp.maximum(diff, 0.0)[None] * lg), 0.0)
    q_dec = jnp.exp((idx[None, :] + 1.0) * log_gamma[:, None])
    k_dec = jnp.exp((C - 1.0 - idx[None, :]) * log_gamma[:, None])
    bc = lambda a: jnp.broadcast_to(a[:, :, None], (RET_HEADS, C, C))
    rt_tab = jnp.stack([dmask, bc(q_dec), bc(k_dec)], axis=1)
    gc_tab = jnp.broadcast_to(jnp.exp(C * log_gamma)[:, None, None], (RET_HEADS, SUBLANES, RET_DIM))
    gam1 = jnp.broadcast_to(jnp.exp(log_gamma)[:, None], (RET_HEADS, RET_DIM))
    return rt_tab, gc_tab, gam1


def kernel(x_prompt, x_sample, state_wkv, state_shift, state_ret, meta_tokens, norm1_g, w_in, shift_mu, decay_w0, decay_up, aaa_a0, aaa_up, gate_up, k_k, k_a, r_k, lnx_w, lnx_b, w_branch_a, w_branch_b, w_out, norm2_g, router_group_w, router_group_b, router_expert_w, router_expert_b, expert_w_gate, expert_w_up, expert_w_down, final_norm_g):
    B, T, D = x_prompt.shape
    NS = x_sample.shape[0]
    TP = LEAD + T
    H, HD = RWKV_HEADS, RWKV_HEAD_DIM
    l = 0
    row = lambda a: a.reshape(1, -1)

    zpad = jnp.zeros((LORA_PAD - 64, RWKV_WIDTH), F32)
    dup_pad = jnp.concatenate([decay_up[l], zpad], axis=0)
    aup_pad = jnp.concatenate([zpad, aaa_up[l]], axis=0)
    wr_full = jnp.concatenate([router_expert_w[l], router_group_w[l]], axis=1)
    wr_full = jnp.pad(wr_full, ((0, 0), (0, LANES - wr_full.shape[1])))
    br_full = jnp.pad(jnp.concatenate([router_expert_b[l], router_group_b[l]]), (0, LANES - N_EXPERTS - N_GROUPS))
    wr = _pieces(wr_full, 2)
    wg = expert_w_gate[l].astype(BF16)
    wu = expert_w_up[l].astype(BF16)
    wd = expert_w_down[l].astype(BF16)
    rt_tab, gc_tab, gam1 = _retention_tables()

    def prep_params(n):
        return (row(shift_mu[l]), row(decay_w0[l]), _pieces(dup_pad, n), row(aaa_a0[l]), _pieces(aup_pad, n),
                _pieces(gate_up[l], n), row(k_k[l]), row(k_a[l]), row(r_k[l]))

    def flat_params(p):
        out = []
        for a in p:
            out.extend(a if isinstance(a, tuple) else (a,))
        return tuple(out)

    def tail(n, x, y, g, bonus, ob, gates, tm_tail, tm_moe, y_feature_major):
        x1, u2, comb = _mixtail(x, y, g, bonus, ob, gates, row(lnx_w[l]), row(lnx_b[l]), row(norm2_g[l]),
                                row(br_full), _pieces(w_branch_a[l], n), _pieces(w_branch_b[l], n),
                                _pieces(w_out[l], n), wr, tm_tail, y_feature_major)
        flat = lambda a: a.reshape(-1, a.shape[-1])
        return _moe(flat(x1), flat(u2), flat(comb), wg, wu, wd, row(final_norm_g), tm_moe)

    lead = jnp.concatenate([jnp.zeros((LEAD - N_META, D), F32), meta_tokens.astype(F32)], axis=0)
    cc_p, ss_p = _rope_tables(jnp.arange(TP, dtype=jnp.int32) - (LEAD - N_META))
    pk, g, bonus, gates, ob, shift_p, ret_p = _inproj_prompt(
        x_prompt, lead, cc_p, ss_p, rt_tab, gc_tab, row(norm1_g[l]), _pieces(w_in[l], 1),
        flat_params(prep_params(1)), nb=4)

    pk_t = pk.reshape(6, 2 * B * H, K_LO, TP).transpose(3, 0, 2, 1)

    pk, xs = lax.optimization_barrier((pk, x_sample.reshape(NS, D)))
    cc_s, ss_s = _rope_tables(jnp.full((NS,), PAST_LEN, jnp.int32))
    pk_s, v_s, g_s, bonus_s, gates_s, retq, shift_s = _inproj_sample(
        xs, state_shift[l], cc_s, ss_s, row(norm1_g[l]), _pieces(w_in[l], 2), flat_params(prep_params(2)))
    wkv_s, yt_s = _rwkv_sample(state_wkv[l], pk_s.reshape(NS, 5, H, HD),
                               v_s.reshape(NS, H, HD).transpose(0, 2, 1), bt=8)
    y_s = yt_s.transpose(0, 2, 1).reshape(NS, RWKV_WIDTH)
    heads = lambda a: a.reshape(NS, RET_HEADS, RET_DIM)
    q_s, k_s, v_r, g_r = (retq[:, i * RET_WIDTH:(i + 1) * RET_WIDTH] for i in range(4))
    ret_s, ob_s = _ret_sample(state_ret[l], heads(q_s).transpose(0, 2, 1), heads(k_s).transpose(0, 2, 1),
                              heads(v_r), heads(g_r), gam1, bt=8)
    y_sample = tail(2, xs[None], y_s[None], g_s[None], bonus_s[None], ob_s.reshape(1, NS, RET_WIDTH), gates_s[None],
                    NS, NS, False)

    pk_t, y_sample, wkv_s, ret_s = lax.optimization_barrier((pk_t, y_sample, wkv_s, ret_s))
    y_t, s_fin = _rwkv_prompt(pk_t, tb=16)
    y_fm = y_t.transpose(2, 1, 0).reshape(2, B, RWKV_WIDTH // 2, T)
    wkv_p = s_fin.reshape(HD, K_LO, 2, B, H).transpose(3, 4, 2, 1, 0).reshape(B, H, HD, HD)
    y_prompt = tail(1, x_prompt, y_fm, g, bonus, ob, gates, 512, 1024, True).reshape(B, T, D)

    return (y_prompt, y_sample.reshape(NS, 1, D),
            wkv_p[None], shift_p.reshape(1, B, SHIFT_WIDTH), ret_p[None],
            wkv_s[None], shift_s[None], ret_s[None])
```

```python
import functools

import jax
import jax.numpy as jnp
from jax import lax
from jax.experimental import pallas as pl
from jax.experimental.pallas import tpu as pltpu

F32 = jnp.float32
BF16 = jnp.bfloat16

D_MODEL = 1024
N_META = 16
RWKV_WIDTH = 512
RWKV_HEADS = 8
RWKV_HEAD_DIM = 64
LORA_PAD = 128
GATE_LORA = 128
SHIFT_WIDTH = 3 * RWKV_WIDTH + 64 + 64 + GATE_LORA
LNX_EPS = 64e-5
RET_WIDTH = 512
RET_HEADS = 4
RET_DIM = 128
RET_CHUNK = 128
ROPE_BASE = 10000.0
RET_COL0 = SHIFT_WIDTH
GATE_COL0 = SHIFT_WIDTH + 4 * RET_WIDTH
IN_WIDTH = GATE_COL0 + 2 * D_MODEL
N_GROUPS = 4
EXPERTS_PER_GROUP = 4
N_EXPERTS = 16
EXPERT_FF = 256
RMS_EPS = 1e-6
PAST_LEN = 16384

LANES = 128
SUBLANES = 8
VMEM_LIMIT = 60 * 1024 * 1024

LEAD = 128
PK_W, PK_NKK, PK_BB, PK_K, PK_R, PK_V = range(6)
K_LO = 32
KL_UNROLL = 8


def _split_bf16(a, nterms):
    pieces = []
    rem = a
    for _ in range(nterms - 1):
        c = rem * 65537.0
        hi = c - (c - rem)
        pieces.append(hi.astype(BF16))
        rem = rem - hi
    pieces.append(rem.astype(BF16))
    return pieces


def _mm(a, w_refs, cols=None):
    def w(i):
        r = w_refs[i]
        return r[...] if cols is None else r[:, cols[0]:cols[1]]

    if a.dtype == BF16:
        acc = jnp.dot(a, w(0), preferred_element_type=F32)
        if len(w_refs) > 1:
            acc = acc + jnp.dot(a, w(1), preferred_element_type=F32)
        return acc
    if len(w_refs) == 1:
        return jnp.dot(a.astype(BF16), w(0), preferred_element_type=F32)
    a_hi, a_lo = _split_bf16(a, 2)
    acc = jnp.dot(a_hi, w(0), preferred_element_type=F32)
    acc = acc + jnp.dot(a_lo, w(0), preferred_element_type=F32)
    acc = acc + jnp.dot(a_hi, w(1), preferred_element_type=F32)
    return acc


def _rmsnorm(x, g):
    return x * lax.rsqrt(jnp.mean(x * x, axis=-1, keepdims=True) + RMS_EPS) * g


def _sigmoid(x):
    return 1.0 / (1.0 + jnp.exp(-x))


def _head64_sum(x):
    lane = lax.broadcasted_iota(jnp.int32, (x.shape[0], LANES), 1)
    lo = lane < RWKV_HEAD_DIM
    outs = []
    for j in range(x.shape[1] // LANES):
        xj = x[:, j * LANES:(j + 1) * LANES]
        s_lo = jnp.sum(jnp.where(lo, xj, 0.0), axis=-1, keepdims=True)
        s_hi = jnp.sum(jnp.where(lo, 0.0, xj), axis=-1, keepdims=True)
        outs.append(jnp.where(lo, s_lo, s_hi))
    return jnp.concatenate(outs, axis=-1)


def _rope(x, cc, ss):
    lane = lax.broadcasted_iota(jnp.int32, x.shape, 1)
    even = (lane & 1) == 0
    swapped = jnp.where(even, pltpu.roll(x, LANES - 1, 1), pltpu.roll(x, 1, 1))
    return x * cc + swapped * ss


def _rwkv_prep(s, prev, prm):
    (mu, w0, dup, a0, aup, gup, k_k, k_a, r_k) = prm
    xs = s + mu[...] * (prev - s)
    W = RWKV_WIDTH
    r = xs[:, 0:W]
    k = xs[:, W:2 * W]
    v = xs[:, 2 * W:3 * W]
    wa = xs[:, 3 * W:3 * W + LORA_PAD]
    gd = xs[:, 3 * W + LORA_PAD:]
    dec = _mm(jnp.tanh(wa), dup)
    z = -(w0[...] + dec)
    softplus = jnp.maximum(z, 0.0) + jnp.log1p(jnp.exp(-jnp.abs(z)))
    decay = jnp.exp(-jnp.exp(-softplus - 0.5))
    a = _sigmoid(a0[...] + _mm(wa, aup))
    g = _mm(_sigmoid(gd), gup)
    kk = k * k_k[...]
    kk = kk / jnp.maximum(jnp.sqrt(_head64_sum(kk * kk)), 1e-12)
    k_mod = k * (1.0 + (a - 1.0) * k_a[...])
    bonus = _head64_sum(r * k_mod * r_k[...]) * v
    return r, decay, k_mod, v, kk, a, g, bonus


def _take_prep_params(refs, n_w):
    it = iter(refs)
    take = lambda n: tuple(next(it) for _ in range(n))
    (mu, w0), dup, (a0,), aup, gup, (k_k, k_a, r_k) = take(2), take(n_w), take(1), take(n_w), take(n_w), take(3)
    return (mu, w0, dup, a0, aup, gup, k_k, k_a, r_k), tuple(it)


def _store_rwkv_operands(ops, pk_ref, v_ref, g_ref, bonus_ref):
    r, decay, k_mod, v, kk, a, g, bonus = ops
    W = RWKV_WIDTH
    pk_ref[:, PK_W * W:(PK_W + 1) * W] = decay
    pk_ref[:, PK_NKK * W:(PK_NKK + 1) * W] = -kk
    pk_ref[:, PK_BB * W:(PK_BB + 1) * W] = kk * a
    pk_ref[:, PK_K * W:(PK_K + 1) * W] = k_mod
    pk_ref[:, PK_R * W:(PK_R + 1) * W] = r
    v_ref[...] = v
    g_ref[...] = g
    bonus_ref[...] = bonus


def _inproj_prompt_kernel(n_w, x_ref, cc_ref, ss_ref, lead_ref, rt_ref, gc_ref, n1_ref, *rest):
    w_in = rest[:n_w]
    prm, rest = _take_prep_params(rest[n_w:], n_w)
    (pk_ref, g_ref, bonus_ref, gates_ref, ob_ref, shift_ref, ret_ref, prev_buf, ret_state) = rest
    t = pl.program_id(1)
    nb, tm, _ = x_ref.shape
    rows = [slice(b * tm, (b + 1) * tm) for b in range(nb)]

    @pl.when(t == 0)
    def _():
        prev_buf[:, 0:SUBLANES, :] = jnp.zeros((nb, SUBLANES, SHIFT_WIDTH), F32)
        ret_state[...] = jnp.zeros(ret_state.shape, F32)

    x = jnp.where(t == 0, lead_ref[...][None], x_ref[...])
    u = _rmsnorm(x.reshape(nb * tm, D_MODEL), n1_ref[...])
    u_b = u.astype(BF16) if n_w == 1 else u

    s = _mm(u_b, w_in, (0, SHIFT_WIDTH))
    prevs = []
    for b in range(nb):
        s_b = s[rows[b]]
        prev_buf[b, SUBLANES:SUBLANES + tm, :] = s_b
        prevs.append(prev_buf[b, SUBLANES - 1:SUBLANES - 1 + tm, :])
        prev_buf[b, SUBLANES - 1:SUBLANES, :] = s_b[tm - 1:tm, :]
        shift_ref[b] = s_b[tm - 1:tm, :]
    prev = jnp.concatenate(prevs, axis=0)

    r, decay, k_mod, v, kk, a, g, bonus = _rwkv_prep(s, prev, prm)
    W = RWKV_WIDTH
    for b in range(nb):
        rb = rows[b]
        for q, val in ((PK_W, decay), (PK_NKK, -kk), (PK_BB, kk * a), (PK_K, k_mod), (PK_R, r), (PK_V, v)):
            fm = val[rb].T
            for hi in range(2):
                pk_ref[q, hi, b] = jnp.concatenate(
                    [fm[h * RWKV_HEAD_DIM + hi * K_LO:h * RWKV_HEAD_DIM + (hi + 1) * K_LO] for h in range(RWKV_HEADS)],
                    axis=0)
        g_ref[b] = g[rb]
        bonus_ref[b] = bonus[rb]

    gates = _mm(u_b, w_in, (GATE_COL0, IN_WIDTH))
    for b in range(nb):
        gates_ref[b] = gates[rows[b]]

    ret = _mm(u_b, w_in, (RET_COL0, GATE_COL0))
    cc = cc_ref[...]
    ss = ss_ref[...]
    for b in range(nb):
        rb = rows[b]
        for h in range(RET_HEADS):
            c0 = h * RET_DIM
            qc = _rope(ret[rb, c0:c0 + RET_DIM], cc, ss).astype(BF16)
            kc = _rope(ret[rb, RET_WIDTH + c0:RET_WIDTH + c0 + RET_DIM], cc, ss) * (RET_DIM ** -0.5)
            vc = ret[rb, 2 * RET_WIDTH + c0:2 * RET_WIDTH + c0 + RET_DIM].astype(BF16)
            gr = ret[rb, 3 * RET_WIDTH + c0:3 * RET_WIDTH + c0 + RET_DIM]
            state = ret_state[b, h]
            scores = lax.dot_general(qc, kc.astype(BF16), (((1,), (1,)), ((), ())),
                                     preferred_element_type=F32) * rt_ref[h, 0]
            intra = jnp.dot(scores.astype(BF16), vc, preferred_element_type=F32)
            cross = jnp.dot(qc, state.astype(BF16), preferred_element_type=F32) * rt_ref[h, 1]
            kt = (kc * rt_ref[h, 2]).T.astype(BF16)
            ret_state[b, h] = gc_ref[h, 0:1, :] * state + jnp.dot(kt, vc, preferred_element_type=F32)
            o = intra + cross
            o = o * lax.rsqrt(jnp.mean(o * o, axis=-1, keepdims=True) + RMS_EPS)
            ob_ref[b, :, c0:c0 + RET_DIM] = (gr * _sigmoid(gr) * o).astype(ob_ref.dtype)

    @pl.when(t == pl.num_programs(1) - 1)
    def _():
        ret_ref[...] = ret_state[...]


def _inproj_sample_kernel(n_w, x_ref, prev_ref, cc_ref, ss_ref, n1_ref, *rest):
    w_in = rest[:n_w]
    prm, rest = _take_prep_params(rest[n_w:], n_w)
    (pk_ref, v_ref, g_ref, bonus_ref, gates_ref, retq_ref, shift_ref) = rest
    u = _rmsnorm(x_ref[...], n1_ref[...])
    s = _mm(u, w_in, (0, SHIFT_WIDTH))
    shift_ref[...] = s
    _store_rwkv_operands(_rwkv_prep(s, prev_ref[...], prm), pk_ref, v_ref, g_ref, bonus_ref)
    gates_ref[...] = _mm(u, w_in, (GATE_COL0, IN_WIDTH))
    ret = _mm(u, w_in, (RET_COL0, GATE_COL0))
    cc = cc_ref[...]
    ss = ss_ref[...]
    for h in range(RET_HEADS):
        c0 = h * RET_DIM
        retq_ref[:, c0:c0 + RET_DIM] = _rope(ret[:, c0:c0 + RET_DIM], cc, ss)
        retq_ref[:, RET_WIDTH + c0:RET_WIDTH + c0 + RET_DIM] = (
            _rope(ret[:, RET_WIDTH + c0:RET_WIDTH + c0 + RET_DIM], cc, ss) * (RET_DIM ** -0.5))
    retq_ref[:, 2 * RET_WIDTH:] = ret[:, 2 * RET_WIDTH:]


def _const_spec(arr):
    nd = arr.ndim
    return pl.BlockSpec(arr.shape, lambda *_: (0,) * nd, pipeline_mode=pl.Buffered(1))


def _inproj_prompt(x, lead, cc, ss, rt_tab, gc_tab, n1, w_in, prm, nb):
    B, T, _ = x.shape
    tm = RET_CHUNK
    assert lead.shape[0] == tm and T % tm == 0 and B % nb == 0
    TP = tm + T
    n_w = len(w_in)
    consts = (lead, rt_tab, gc_tab, n1) + tuple(w_in) + tuple(prm)
    prow = lambda width: pl.BlockSpec((nb, tm, width), lambda b, t: (b, jnp.maximum(t - 1, 0), 0))
    in_specs = [prow(D_MODEL),
                pl.BlockSpec((tm, LANES), lambda b, t: (t, 0)),
                pl.BlockSpec((tm, LANES), lambda b, t: (t, 0))] + [_const_spec(c) for c in consts]
    out_shape = (
        jax.ShapeDtypeStruct((6, 2, B, RWKV_WIDTH // 2, TP), F32),
        jax.ShapeDtypeStruct((B, T, RWKV_WIDTH), F32),
        jax.ShapeDtypeStruct((B, T, RWKV_WIDTH), F32),
        jax.ShapeDtypeStruct((B, T, 2 * D_MODEL), F32),
        jax.ShapeDtypeStruct((B, T, RET_WIDTH), BF16),
        jax.ShapeDtypeStruct((B, 1, SHIFT_WIDTH), F32),
        jax.ShapeDtypeStruct((B, RET_HEADS, RET_DIM, RET_DIM), F32),
    )
    out_specs = (
        pl.BlockSpec((6, 2, nb, RWKV_WIDTH // 2, tm), lambda b, t: (0, 0, b, 0, t)),
        prow(RWKV_WIDTH), prow(RWKV_WIDTH), prow(2 * D_MODEL), prow(RET_WIDTH),
        pl.BlockSpec((nb, 1, SHIFT_WIDTH), lambda b, t: (b, 0, 0)),
        pl.BlockSpec((nb, RET_HEADS, RET_DIM, RET_DIM), lambda b, t: (b, 0, 0, 0)),
    )
    return pl.pallas_call(
        functools.partial(_inproj_prompt_kernel, n_w),
        grid=(B // nb, TP // tm),
        in_specs=in_specs,
        out_specs=out_specs,
        out_shape=out_shape,
        scratch_shapes=[pltpu.VMEM((nb, tm + SUBLANES, SHIFT_WIDTH), F32),
                        pltpu.VMEM((nb, RET_HEADS, RET_DIM, RET_DIM), F32)],
        compiler_params=pltpu.CompilerParams(
            dimension_semantics=("arbitrary", "arbitrary"), vmem_limit_bytes=VMEM_LIMIT),
        name="inproj_prompt",
    )(x, cc, ss, *consts)


def _inproj_sample(xs, prev, cc, ss, n1, w_in, prm):
    N = xs.shape[0]
    n_w = len(w_in)
    args = (xs, prev, cc, ss, n1) + tuple(w_in) + tuple(prm)
    out_shape = (
        jax.ShapeDtypeStruct((N, 5 * RWKV_WIDTH), F32),
        jax.ShapeDtypeStruct((N, RWKV_WIDTH), F32),
        jax.ShapeDtypeStruct((N, RWKV_WIDTH), F32),
        jax.ShapeDtypeStruct((N, RWKV_WIDTH), F32),
        jax.ShapeDtypeStruct((N, 2 * D_MODEL), F32),
        jax.ShapeDtypeStruct((N, 4 * RET_WIDTH), F32),
        jax.ShapeDtypeStruct((N, SHIFT_WIDTH), F32),
    )
    return pl.pallas_call(
        functools.partial(_inproj_sample_kernel, n_w),
        out_shape=out_shape,
        compiler_params=pltpu.CompilerParams(vmem_limit_bytes=VMEM_LIMIT),
        name="inproj_sample",
    )(*args)


def _rwkv_prompt_kernel(pk_ref, nxt_ref, y_ref, sout_ref, state, sa_buf, kd):
    i = pl.program_id(0)
    tb = pk_ref.shape[0]
    HD = RWKV_HEAD_DIM
    VL = K_LO

    @pl.when(i == 0)
    def _():
        state[...] = jnp.zeros(state.shape, F32)
        sa_buf[...] = jnp.zeros(sa_buf.shape, F32)

    first_half = lax.broadcasted_iota(jnp.int32, (K_LO, LANES), 1) < LANES // 2

    def spread(x, t, q):
        swapped = pltpu.roll(x, LANES // 2, 1)
        kd[t, q, 0:K_LO, :] = jnp.where(first_half, x, swapped)
        kd[t, q, K_LO:HD, :] = jnp.where(first_half, swapped, x)

    for t in range(tb):
        for q in (PK_W, PK_NKK, PK_BB, PK_K, PK_R):
            spread(pk_ref[t, q], t, q)
    spread(nxt_ref[0, 0], tb, PK_NKK)

    def step(t, sa):
        vv = pk_ref[t, PK_V]

        def key_rows(c, accs):
            yacc, sacc = accs
            for j in range(KL_UNROLL):
                k = c * KL_UNROLL + j
                row = lambda tt, q: kd[tt, q, pl.ds(k, 1), :]
                rows = pl.ds(pl.multiple_of(k * VL, VL), VL)
                new = state[rows, :] * row(t, PK_W) + sa * row(t, PK_BB) + vv * row(t, PK_K)
                state[rows, :] = new
                yacc = yacc + new * row(t, PK_R)
                sacc = sacc + new * row(t + 1, PK_NKK)
            return yacc, sacc

        zero = jnp.zeros((VL, LANES), F32)
        yacc, sacc = lax.fori_loop(0, HD // KL_UNROLL, key_rows, (zero, zero))
        y_ref[t] = yacc
        return sacc

    sa_buf[...] = lax.fori_loop(0, tb, step, sa_buf[...])

    @pl.when(i == pl.num_programs(0) - 1)
    def _():
        sout_ref[...] = state[...]


def _rwkv_prompt(pk_t, tb):
    TP = pk_t.shape[0]
    HD = RWKV_HEAD_DIM
    assert (LEAD - N_META) % tb == 0 and LEAD % tb == 0
    skip = (LEAD - N_META) // tb
    meta_blocks = N_META // tb
    return pl.pallas_call(
        _rwkv_prompt_kernel,
        grid=(TP // tb - skip,),
        in_specs=[pl.BlockSpec((tb, 6, K_LO, LANES), lambda i: (i + skip, 0, 0, 0)),
                  pl.BlockSpec((1, 1, K_LO, LANES),
                               lambda i: (jnp.minimum((i + skip + 1) * tb, TP - 1), PK_NKK, 0, 0))],
        out_specs=(pl.BlockSpec((tb, K_LO, LANES), lambda i: (jnp.maximum(i - meta_blocks, 0), 0, 0)),
                   pl.BlockSpec((K_LO * HD, LANES), lambda i: (0, 0))),
        out_shape=(jax.ShapeDtypeStruct((TP - LEAD, K_LO, LANES), F32),
                   jax.ShapeDtypeStruct((K_LO * HD, LANES), F32)),
        scratch_shapes=[pltpu.VMEM((HD * K_LO, LANES), F32), pltpu.VMEM((K_LO, LANES), F32),
                        pltpu.VMEM((tb + 1, 5, HD, LANES), F32)],
        compiler_params=pltpu.CompilerParams(dimension_semantics=("arbitrary",)),
        name="rwkv_prompt",
    )(pk_t, pk_t)


def _lane_col(tile, h):
    lane = lax.broadcasted_iota(jnp.int32, tile.shape, 1)
    return jnp.sum(jnp.where(lane == h, tile, 0.0), axis=-1, keepdims=True)


def _rwkv_sample_kernel(s_ref, pk_ref, vt_ref, sout_ref, yt_ref):
    for b in range(s_ref.shape[0]):
        v_tile = vt_ref[b]
        lane = lax.broadcasted_iota(jnp.int32, v_tile.shape, 1)
        row = lambda q: pk_ref[b, q][:, None, :]
        v_cols = jnp.stack([_lane_col(v_tile, h) for h in range(RWKV_HEADS)], axis=0)
        S = s_ref[b]
        sa = jnp.sum(S * row(PK_NKK), axis=-1, keepdims=True)
        new = S * row(PK_W) + sa * row(PK_BB) + v_cols * row(PK_K)
        sout_ref[b] = new
        y = jnp.sum(new * row(PK_R), axis=-1, keepdims=True)
        y_tile = jnp.zeros(v_tile.shape, F32)
        for h in range(RWKV_HEADS):
            y_tile = jnp.where(lane == h, y[h], y_tile)
        yt_ref[b] = y_tile


def _rwkv_sample(state, pk_h, v_t, bt):
    N = state.shape[0]
    H, HD = RWKV_HEADS, RWKV_HEAD_DIM
    return pl.pallas_call(
        _rwkv_sample_kernel,
        grid=(N // bt,),
        in_specs=[pl.BlockSpec((bt, H, HD, HD), lambda i: (i, 0, 0, 0)),
                  pl.BlockSpec((bt, 5, H, HD), lambda i: (i, 0, 0, 0)),
                  pl.BlockSpec((bt, HD, H), lambda i: (i, 0, 0))],
        out_specs=(pl.BlockSpec((bt, H, HD, HD), lambda i: (i, 0, 0, 0)),
                   pl.BlockSpec((bt, HD, H), lambda i: (i, 0, 0))),
        out_shape=(jax.ShapeDtypeStruct(state.shape, F32),
                   jax.ShapeDtypeStruct((N, HD, H), F32)),
        compiler_params=pltpu.CompilerParams(dimension_semantics=("arbitrary",)),
        name="rwkv_sample",
    )(state, pk_h, v_t)


def _ret_sample_kernel(s_ref, qt_ref, kt_ref, v_ref, g_ref, gam_ref, sout_ref, ob_ref):
    gam = gam_ref[...][:, None, :]
    for b in range(s_ref.shape[0]):
        q_tile = qt_ref[b]
        k_tile = kt_ref[b]
        qc = jnp.stack([_lane_col(q_tile, h) for h in range(RET_HEADS)], axis=0)
        kc = jnp.stack([_lane_col(k_tile, h) for h in range(RET_HEADS)], axis=0)
        vrow = v_ref[b][:, None, :]
        S = s_ref[b]
        qk = jnp.sum(qc * kc, axis=1, keepdims=True)
        cross = jnp.sum(qc * S, axis=1, keepdims=True)
        o = qk * vrow + cross * gam
        sout_ref[b] = gam * S + kc * vrow
        o = o * lax.rsqrt(jnp.mean(o * o, axis=-1, keepdims=True) + RMS_EPS)
        gr = g_ref[b][:, None, :]
        ob_ref[b] = (gr * _sigmoid(gr) * o)[:, 0, :]


def _ret_sample(state, q_t, k_t, v_h, g_h, gam, bt):
    N = state.shape[0]
    H, DK = RET_HEADS, RET_DIM
    col = pl.BlockSpec((bt, DK, H), lambda i: (i, 0, 0))
    rowspec = pl.BlockSpec((bt, H, DK), lambda i: (i, 0, 0))
    st = pl.BlockSpec((bt, H, DK, DK), lambda i: (i, 0, 0, 0))
    return pl.pallas_call(
        _ret_sample_kernel,
        grid=(N // bt,),
        in_specs=[st, col, col, rowspec, rowspec, pl.BlockSpec((H, DK), lambda i: (0, 0))],
        out_specs=(st, rowspec),
        out_shape=(jax.ShapeDtypeStruct(state.shape, F32), jax.ShapeDtypeStruct((N, H, DK), F32)),
        compiler_params=pltpu.CompilerParams(dimension_semantics=("arbitrary",)),
        name="ret_sample",
    )(state, q_t, k_t, v_h, g_h, gam)


def _mixtail_kernel(n_w, y_feature_major, x_ref, y_ref, g_ref, bonus_ref, ob_ref, gates_ref,
                    lnw_ref, lnb_ref, n2_ref, br_ref, *rest):
    wa = rest[0:n_w]
    wb = rest[n_w:2 * n_w]
    wo = rest[2 * n_w:3 * n_w]
    wr = rest[3 * n_w:3 * n_w + 2]
    x1_ref, u2_ref, comb_ref = rest[3 * n_w + 2:]

    inv_n = 1.0 / RWKV_HEAD_DIM
    if y_feature_major:
        normed = []
        for h in range(RWKV_HEADS):
            yh = jnp.concatenate([y_ref[hi, h * K_LO:(h + 1) * K_LO, :] for hi in range(2)], axis=0)
            d = yh - jnp.sum(yh, axis=0, keepdims=True) * inv_n
            var = jnp.sum(d * d, axis=0, keepdims=True) * inv_n
            normed.append(d * lax.rsqrt(var + LNX_EPS))
        yhat = jnp.concatenate(normed, axis=0).T
    else:
        y = y_ref[...]
        d = y - _head64_sum(y) * inv_n
        var = _head64_sum(d * d) * inv_n
        yhat = d * lax.rsqrt(var + LNX_EPS)
    yn = yhat * lnw_ref[...] + lnb_ref[...]
    ya = (yn + bonus_ref[...]) * g_ref[...]
    branch_a = _mm(ya, wa)
    branch_b = _mm(ob_ref[...], wb)
    gates = gates_ref[...]
    merged = _sigmoid(gates[:, :D_MODEL]) * branch_a + _sigmoid(gates[:, D_MODEL:]) * branch_b
    x1 = x_ref[...] + _mm(merged, wo)
    x1_ref[...] = x1
    u2 = _rmsnorm(x1, n2_ref[...])
    u2_ref[...] = u2.astype(u2_ref.dtype)

    logits = (_mm(u2, wr) + br_ref[...]).T
    tm = logits.shape[1]
    neg = jnp.float32(-jnp.inf)
    big = jnp.float32(1e9)
    first = lambda hit, idx: jnp.min(jnp.where(hit, idx, big), axis=0, keepdims=True)
    lg = logits[N_EXPERTS:N_EXPERTS + N_GROUPS]
    gidx = lax.broadcasted_iota(jnp.int32, lg.shape, 0).astype(F32)
    lg_max = jnp.max(lg, axis=0, keepdims=True)
    p_grp = 1.0 / jnp.sum(jnp.exp(lg - lg_max), axis=0, keepdims=True)
    grp = first(lg == lg_max, gidx)
    eidx = lax.broadcasted_iota(jnp.int32, (N_EXPERTS, tm), 0).astype(F32)
    in_grp = (eidx >= grp * EXPERTS_PER_GROUP) & (eidx < (grp + 1.0) * EXPERTS_PER_GROUP)
    le = jnp.where(in_grp, logits[0:N_EXPERTS], neg)
    v1 = jnp.max(le, axis=0, keepdims=True)
    i1 = first(le == v1, eidx)
    le2 = jnp.where(eidx == i1, neg, le)
    v2 = jnp.max(le2, axis=0, keepdims=True)
    i2 = first(le2 == v2, eidx)
    e2 = jnp.exp(v2 - v1)
    w1 = 1.0 / (1.0 + e2)
    w2 = e2 / (1.0 + e2)
    comb = p_grp * (jnp.where(eidx == i1, w1, 0.0) + jnp.where(eidx == i2, w2, 0.0))
    comb_ref[...] = jnp.concatenate([comb, jnp.zeros((LANES - N_EXPERTS, tm), F32)], axis=0).T


def _mixtail(x, y, g, bonus, ob, gates, lnw, lnb, n2, br, wa, wb, wo, wr, tm, y_feature_major):
    G, n, _ = x.shape
    assert n % tm == 0
    n_w = len(wa)
    consts = (lnw, lnb, n2, br) + tuple(wa) + tuple(wb) + tuple(wo) + tuple(wr)
    row = lambda width: pl.BlockSpec((None, tm, width), lambda b, i: (b, i, 0))
    y_spec = (pl.BlockSpec((2, None, RWKV_WIDTH // 2, tm), lambda b, i: (0, b, 0, i)) if y_feature_major
              else row(RWKV_WIDTH))
    in_specs = [row(D_MODEL), y_spec, row(RWKV_WIDTH), row(RWKV_WIDTH), row(RET_WIDTH),
                row(2 * D_MODEL)] + [_const_spec(c) for c in consts]
    return pl.pallas_call(
        functools.partial(_mixtail_kernel, n_w, y_feature_major),
        grid=(G, n // tm),
        in_specs=in_specs,
        out_specs=(row(D_MODEL), row(D_MODEL), row(LANES)),
        out_shape=(jax.ShapeDtypeStruct((G, n, D_MODEL), F32),
                   jax.ShapeDtypeStruct((G, n, D_MODEL), BF16),
                   jax.ShapeDtypeStruct((G, n, LANES), F32)),
        compiler_params=pltpu.CompilerParams(
            dimension_semantics=("arbitrary", "arbitrary"), vmem_limit_bytes=VMEM_LIMIT),
        name="mixtail",
    )(x, y, g, bonus, ob, gates, *consts)


def _moe_kernel(x1_ref, u2_ref, comb_ref, wg_ref, wu_ref, wd_ref, fg_ref, o_ref, acc):
    grp = pl.program_id(1)

    @pl.when(grp == 0)
    def _():
        acc[...] = jnp.zeros(acc.shape, F32)

    u2 = u2_ref[...]
    comb = comb_ref[...]
    lane = lax.broadcasted_iota(jnp.int32, comb.shape, 1)
    total = acc[...]
    for j in range(EXPERTS_PER_GROUP):
        c = jnp.sum(jnp.where(lane == grp * EXPERTS_PER_GROUP + j, comb, 0.0), axis=-1, keepdims=True)
        hg = jnp.dot(u2, wg_ref[j], preferred_element_type=F32)
        hu = jnp.dot(u2, wu_ref[j], preferred_element_type=F32)
        h = hg * _sigmoid(hg) * hu * c
        total = total + jnp.dot(h.astype(BF16), wd_ref[j], preferred_element_type=F32)
    acc[...] = total

    @pl.when(grp == pl.num_programs(1) - 1)
    def _():
        o_ref[...] = _rmsnorm(x1_ref[...] + acc[...], fg_ref[...])


def _moe(x1, u2, comb, wg, wu, wd, fg, tm):
    N = x1.shape[0]
    E = EXPERTS_PER_GROUP
    return pl.pallas_call(
        _moe_kernel,
        grid=(N // tm, N_GROUPS),
        in_specs=[pl.BlockSpec((tm, D_MODEL), lambda i, g: (i, 0)),
                  pl.BlockSpec((tm, D_MODEL), lambda i, g: (i, 0)),
                  pl.BlockSpec((tm, LANES), lambda i, g: (i, 0)),
                  pl.BlockSpec((E, D_MODEL, EXPERT_FF), lambda i, g: (g, 0, 0)),
                  pl.BlockSpec((E, D_MODEL, EXPERT_FF), lambda i, g: (g, 0, 0)),
                  pl.BlockSpec((E, EXPERT_FF, D_MODEL), lambda i, g: (g, 0, 0)),
                  pl.BlockSpec((1, D_MODEL), lambda i, g: (0, 0))],
        out_specs=pl.BlockSpec((tm, D_MODEL), lambda i, g: (i, 0)),
        out_shape=jax.ShapeDtypeStruct((N, D_MODEL), F32),
        scratch_shapes=[pltpu.VMEM((tm, D_MODEL), F32)],
        compiler_params=pltpu.CompilerParams(
            dimension_semantics=("arbitrary", "arbitrary"), vmem_limit_bytes=VMEM_LIMIT),
        name="moe",
    )(x1, u2, comb, wg, wu, wd, fg)


def _pieces(w, n):
    return tuple(_split_bf16(w.astype(F32), n))


def _rope_tables(pos):
    half = RET_DIM // 2
    inv = ROPE_BASE ** (-jnp.arange(half, dtype=F32) / half)
    ang = pos.astype(F32)[:, None] * inv[None, :]
    c = jnp.cos(ang)
    s = jnp.sin(ang)
    cc = jnp.repeat(c, 2, axis=-1)
    ss = jnp.stack([-s, s], axis=-1).reshape(pos.shape[0], RET_DIM)
    return cc, ss


def _retention_tables():
    C = RET_CHUNK
    log_gamma = jnp.log(1.0 - 2.0 ** (-5.0 - jnp.arange(RET_HEADS, dtype=F32)))
    idx = jnp.arange(C, dtype=F32)
    diff = idx[:, None] - idx[None, :]
    lg = log_gamma[:, None, None]
    dmask = jnp.where(diff[None] >= 0, jnp.exp(jnp.maximum(diff, 0.0)[None] * lg), 0.0)
    q_dec = jnp.exp((idx[None, :] + 1.0) * log_gamma[:, None])
    k_dec = jnp.exp((C - 1.0 - idx[None, :]) * log_gamma[:, None])
    bc = lambda a: jnp.broadcast_to(a[:, :, None], (RET_HEADS, C, C))
    rt_tab = jnp.stack([dmask, bc(q_dec), bc(k_dec)], axis=1)
    gc_tab = jnp.broadcast_to(jnp.exp(C * log_gamma)[:, None, None], (RET_HEADS, SUBLANES, RET_DIM))
    gam1 = jnp.broadcast_to(jnp.exp(log_gamma)[:, None], (RET_HEADS, RET_DIM))
    return rt_tab, gc_tab, gam1


def kernel(x_prompt, x_sample, state_wkv, state_shift, state_ret, meta_tokens, norm1_g, w_in, shift_mu, decay_w0, decay_up, aaa_a0, aaa_up, gate_up, k_k, k_a, r_k, lnx_w, lnx_b, w_branch_a, w_branch_b, w_out, norm2_g, router_group_w, router_group_b, router_expert_w, router_expert_b, expert_w_gate, expert_w_up, expert_w_down, final_norm_g):
    B, T, D = x_prompt.shape
    NS = x_sample.shape[0]
    TP = LEAD + T
    H, HD = RWKV_HEADS, RWKV_HEAD_DIM
    l = 0
    row = lambda a: a.reshape(1, -1)

    zpad = jnp.zeros((LORA_PAD - 64, RWKV_WIDTH), F32)
    dup_pad = jnp.concatenate([decay_up[l], zpad], axis=0)
    aup_pad = jnp.concatenate([zpad, aaa_up[l]], axis=0)
    wr_full = jnp.concatenate([router_expert_w[l], router_group_w[l]], axis=1)
    wr_full = jnp.pad(wr_full, ((0, 0), (0, LANES - wr_full.shape[1])))
    br_full = jnp.pad(jnp.concatenate([router_expert_b[l], router_group_b[l]]), (0, LANES - N_EXPERTS - N_GROUPS))
    wr = _pieces(wr_full, 2)
    wg = expert_w_gate[l].astype(BF16)
    wu = expert_w_up[l].astype(BF16)
    wd = expert_w_down[l].astype(BF16)
    rt_tab, gc_tab, gam1 = _retention_tables()

    def prep_params(n):
        return (row(shift_mu[l]), row(decay_w0[l]), _pieces(dup_pad, n), row(aaa_a0[l]), _pieces(aup_pad, n),
                _pieces(gate_up[l], n), row(k_k[l]), row(k_a[l]), row(r_k[l]))

    def flat_params(p):
        out = []
        for a in p:
            out.extend(a if isinstance(a, tuple) else (a,))
        return tuple(out)

    def tail(n, x, y, g, bonus, ob, gates, tm_tail, tm_moe, y_feature_major):
        x1, u2, comb = _mixtail(x, y, g, bonus, ob, gates, row(lnx_w[l]), row(lnx_b[l]), row(norm2_g[l]),
                                row(br_full), _pieces(w_branch_a[l], n), _pieces(w_branch_b[l], n),
                                _pieces(w_out[l], n), wr, tm_tail, y_feature_major)
        flat = lambda a: a.reshape(-1, a.shape[-1])
        return _moe(flat(x1), flat(u2), flat(comb), wg, wu, wd, row(final_norm_g), tm_moe)

    lead = jnp.concatenate([jnp.zeros((LEAD - N_META, D), F32), meta_tokens.astype(F32)], axis=0)
    cc_p, ss_p = _rope_tables(jnp.arange(TP, dtype=jnp.int32) - (LEAD - N_META))
    pk, g, bonus, gates, ob, shift_p, ret_p = _inproj_prompt(
        x_prompt, lead, cc_p, ss_p, rt_tab, gc_tab, row(norm1_g[l]), _pieces(w_in[l], 1),
        flat_params(prep_params(1)), nb=4)

    pk_t = pk.reshape(6, 2 * B * H, K_LO, TP).transpose(3, 0, 2, 1)

    shift_p, xs = lax.optimization_barrier((shift_p, x_sample.reshape(NS, D)))
    cc_s, ss_s = _rope_tables(jnp.full((NS,), PAST_LEN, jnp.int32))
    pk_s, v_s, g_s, bonus_s, gates_s, retq, shift_s = _inproj_sample(
        xs, state_shift[l], cc_s, ss_s, row(norm1_g[l]), _pieces(w_in[l], 2), flat_params(prep_params(2)))
    wkv_s, yt_s = _rwkv_sample(state_wkv[l], pk_s.reshape(NS, 5, H, HD),
                               v_s.reshape(NS, H, HD).transpose(0, 2, 1), bt=8)
    y_s = yt_s.transpose(0, 2, 1).reshape(NS, RWKV_WIDTH)
    heads = lambda a: a.reshape(NS, RET_HEADS, RET_DIM)
    q_s, k_s, v_r, g_r = (retq[:, i * RET_WIDTH:(i + 1) * RET_WIDTH] for i in range(4))
    ret_s, ob_s = _ret_sample(state_ret[l], heads(q_s).transpose(0, 2, 1), heads(k_s).transpose(0, 2, 1),
                              heads(v_r), heads(g_r), gam1, bt=8)
    y_sample = tail(2, xs[None], y_s[None], g_s[None], bonus_s[None], ob_s.reshape(1, NS, RET_WIDTH), gates_s[None],
                    NS, NS, False)

    pk_t, y_sample, wkv_s, ret_s = lax.optimization_barrier((pk_t, y_sample, wkv_s, ret_s))
    y_t, s_fin = _rwkv_prompt(pk_t, tb=16)
    y_fm = y_t.transpose(2, 1, 0).reshape(2, B, RWKV_WIDTH // 2, T)
    wkv_p = s_fin.reshape(HD, K_LO, 2, B, H).transpose(3, 4, 2, 1, 0).reshape(B, H, HD, HD)
    y_prompt = tail(1, x_prompt, y_fm, g, bonus, ob, gates, 512, 1024, True).reshape(B, T, D)

    return (y_prompt, y_sample.reshape(NS, 1, D),
            wkv_p[None], shift_p.reshape(1, B, SHIFT_WIDTH), ret_p[None],
            wkv_s[None], shift_s[None], ret_s[None])
```

```python
import functools

import jax
import jax.numpy as jnp
from jax import lax
from jax.experimental import pallas as pl
from jax.experimental.pallas import tpu as pltpu

F32 = jnp.float32
BF16 = jnp.bfloat16

D_MODEL = 1024
N_META = 16
RWKV_WIDTH = 512
RWKV_HEADS = 8
RWKV_HEAD_DIM = 64
LORA_PAD = 128
GATE_LORA = 128
SHIFT_WIDTH = 3 * RWKV_WIDTH + 64 + 64 + GATE_LORA
LNX_EPS = 64e-5
RET_WIDTH = 512
RET_HEADS = 4
RET_DIM = 128
RET_CHUNK = 128
ROPE_BASE = 10000.0
RET_COL0 = SHIFT_WIDTH
GATE_COL0 = SHIFT_WIDTH + 4 * RET_WIDTH
IN_WIDTH = GATE_COL0 + 2 * D_MODEL
N_GROUPS = 4
EXPERTS_PER_GROUP = 4
N_EXPERTS = 16
EXPERT_FF = 256
RMS_EPS = 1e-6
PAST_LEN = 16384

LANES = 128
SUBLANES = 8
VMEM_LIMIT = 60 * 1024 * 1024

LEAD = 128
PK_W, PK_NKK, PK_BB, PK_K, PK_R, PK_V = range(6)
K_LO = 32
KL_UNROLL = 8


def _split_bf16(a, nterms):
    pieces = []
    rem = a
    for _ in range(nterms - 1):
        c = rem * 65537.0
        hi = c - (c - rem)
        pieces.append(hi.astype(BF16))
        rem = rem - hi
    pieces.append(rem.astype(BF16))
    return pieces


def _mm(a, w_refs, cols=None):
    def w(i):
        r = w_refs[i]
        return r[...] if cols is None else r[:, cols[0]:cols[1]]

    if a.dtype == BF16:
        acc = jnp.dot(a, w(0), preferred_element_type=F32)
        if len(w_refs) > 1:
            acc = acc + jnp.dot(a, w(1), preferred_element_type=F32)
        return acc
    if len(w_refs) == 1:
        return jnp.dot(a.astype(BF16), w(0), preferred_element_type=F32)
    a_hi, a_lo = _split_bf16(a, 2)
    acc = jnp.dot(a_hi, w(0), preferred_element_type=F32)
    acc = acc + jnp.dot(a_lo, w(0), preferred_element_type=F32)
    acc = acc + jnp.dot(a_hi, w(1), preferred_element_type=F32)
    return acc


def _rmsnorm(x, g):
    return x * lax.rsqrt(jnp.mean(x * x, axis=-1, keepdims=True) + RMS_EPS) * g


def _sigmoid(x):
    return 1.0 / (1.0 + jnp.exp(-x))


def _head64_sum(x):
    lane = lax.broadcasted_iota(jnp.int32, (x.shape[0], LANES), 1)
    lo = lane < RWKV_HEAD_DIM
    outs = []
    for j in range(x.shape[1] // LANES):
        xj = x[:, j * LANES:(j + 1) * LANES]
        s_lo = jnp.sum(jnp.where(lo, xj, 0.0), axis=-1, keepdims=True)
        s_hi = jnp.sum(jnp.where(lo, 0.0, xj), axis=-1, keepdims=True)
        outs.append(jnp.where(lo, s_lo, s_hi))
    return jnp.concatenate(outs, axis=-1)


def _rope(x, cc, ss):
    lane = lax.broadcasted_iota(jnp.int32, x.shape, 1)
    even = (lane & 1) == 0
    swapped = jnp.where(even, pltpu.roll(x, LANES - 1, 1), pltpu.roll(x, 1, 1))
    return x * cc + swapped * ss


def _rwkv_prep(s, prev, prm):
    (mu, w0, dup, a0, aup, gup, k_k, k_a, r_k) = prm
    xs = s + mu[...] * (prev - s)
    W = RWKV_WIDTH
    r = xs[:, 0:W]
    k = xs[:, W:2 * W]
    v = xs[:, 2 * W:3 * W]
    wa = xs[:, 3 * W:3 * W + LORA_PAD]
    gd = xs[:, 3 * W + LORA_PAD:]
    dec = _mm(jnp.tanh(wa), dup)
    z = -(w0[...] + dec)
    softplus = jnp.maximum(z, 0.0) + jnp.log1p(jnp.exp(-jnp.abs(z)))
    decay = jnp.exp(-jnp.exp(-softplus - 0.5))
    a = _sigmoid(a0[...] + _mm(wa, aup))
    g = _mm(_sigmoid(gd), gup)
    kk = k * k_k[...]
    kk = kk / jnp.maximum(jnp.sqrt(_head64_sum(kk * kk)), 1e-12)
    k_mod = k * (1.0 + (a - 1.0) * k_a[...])
    bonus = _head64_sum(r * k_mod * r_k[...]) * v
    return r, decay, k_mod, v, kk, a, g, bonus


def _take_prep_params(refs, n_w):
    it = iter(refs)
    take = lambda n: tuple(next(it) for _ in range(n))
    (mu, w0), dup, (a0,), aup, gup, (k_k, k_a, r_k) = take(2), take(n_w), take(1), take(n_w), take(n_w), take(3)
    return (mu, w0, dup, a0, aup, gup, k_k, k_a, r_k), tuple(it)


def _store_rwkv_operands(ops, pk_ref, v_ref, g_ref, bonus_ref):
    r, decay, k_mod, v, kk, a, g, bonus = ops
    W = RWKV_WIDTH
    pk_ref[:, PK_W * W:(PK_W + 1) * W] = decay
    pk_ref[:, PK_NKK * W:(PK_NKK + 1) * W] = -kk
    pk_ref[:, PK_BB * W:(PK_BB + 1) * W] = kk * a
    pk_ref[:, PK_K * W:(PK_K + 1) * W] = k_mod
    pk_ref[:, PK_R * W:(PK_R + 1) * W] = r
    v_ref[...] = v
    g_ref[...] = g
    bonus_ref[...] = bonus


def _inproj_prompt_kernel(n_w, x_ref, cc_ref, ss_ref, lead_ref, rt_ref, gc_ref, n1_ref, *rest):
    w_in = rest[:n_w]
    prm, rest = _take_prep_params(rest[n_w:], n_w)
    (pk_ref, g_ref, bonus_ref, gates_ref, ob_ref, shift_ref, ret_ref, prev_buf, ret_state) = rest
    t = pl.program_id(1)
    nb, tm, _ = x_ref.shape
    rows = [slice(b * tm, (b + 1) * tm) for b in range(nb)]

    @pl.when(t == 0)
    def _():
        prev_buf[:, 0:SUBLANES, :] = jnp.zeros((nb, SUBLANES, SHIFT_WIDTH), F32)
        ret_state[...] = jnp.zeros(ret_state.shape, F32)

    x = jnp.where(t == 0, lead_ref[...][None], x_ref[...])
    u = _rmsnorm(x.reshape(nb * tm, D_MODEL), n1_ref[...])
    u_b = u.astype(BF16) if n_w == 1 else u

    s = _mm(u_b, w_in, (0, SHIFT_WIDTH))
    prevs = []
    for b in range(nb):
        s_b = s[rows[b]]
        prev_buf[b, SUBLANES:SUBLANES + tm, :] = s_b
        prevs.append(prev_buf[b, SUBLANES - 1:SUBLANES - 1 + tm, :])
        prev_buf[b, SUBLANES - 1:SUBLANES, :] = s_b[tm - 1:tm, :]
        shift_ref[b] = s_b[tm - 1:tm, :]
    prev = jnp.concatenate(prevs, axis=0)

    r, decay, k_mod, v, kk, a, g, bonus = _rwkv_prep(s, prev, prm)
    W = RWKV_WIDTH
    for b in range(nb):
        rb = rows[b]
        for q, val in ((PK_W, decay), (PK_NKK, -kk), (PK_BB, kk * a), (PK_K, k_mod), (PK_R, r), (PK_V, v)):
            fm = val[rb].T
            for hi in range(2):
                pk_ref[q, hi, b] = jnp.concatenate(
                    [fm[h * RWKV_HEAD_DIM + hi * K_LO:h * RWKV_HEAD_DIM + (hi + 1) * K_LO] for h in range(RWKV_HEADS)],
                    axis=0)
        g_ref[b] = g[rb]
        bonus_ref[b] = bonus[rb]

    gates = _mm(u_b, w_in, (GATE_COL0, IN_WIDTH))
    for b in range(nb):
        gates_ref[b] = gates[rows[b]]

    ret = _mm(u_b, w_in, (RET_COL0, GATE_COL0))
    cc = cc_ref[...]
    ss = ss_ref[...]
    for b in range(nb):
        rb = rows[b]
        for h in range(RET_HEADS):
            c0 = h * RET_DIM
            qc = _rope(ret[rb, c0:c0 + RET_DIM], cc, ss).astype(BF16)
            kc = _rope(ret[rb, RET_WIDTH + c0:RET_WIDTH + c0 + RET_DIM], cc, ss) * (RET_DIM ** -0.5)
            vc = ret[rb, 2 * RET_WIDTH + c0:2 * RET_WIDTH + c0 + RET_DIM].astype(BF16)
            gr = ret[rb, 3 * RET_WIDTH + c0:3 * RET_WIDTH + c0 + RET_DIM]
            state = ret_state[b, h]
            scores = lax.dot_general(qc, kc.astype(BF16), (((1,), (1,)), ((), ())),
                                     preferred_element_type=F32) * rt_ref[h, 0]
            intra = jnp.dot(scores.astype(BF16), vc, preferred_element_type=F32)
            cross = jnp.dot(qc, state.astype(BF16), preferred_element_type=F32) * rt_ref[h, 1]
            kt = (kc * rt_ref[h, 2]).T.astype(BF16)
            ret_state[b, h] = gc_ref[h, 0:1, :] * state + jnp.dot(kt, vc, preferred_element_type=F32)
            o = intra + cross
            o = o * lax.rsqrt(jnp.mean(o * o, axis=-1, keepdims=True) + RMS_EPS)
            ob_ref[b, :, c0:c0 + RET_DIM] = (gr * _sigmoid(gr) * o).astype(ob_ref.dtype)

    @pl.when(t == pl.num_programs(1) - 1)
    def _():
        ret_ref[...] = ret_state[...]


def _inproj_sample_kernel(n_w, x_ref, prev_ref, cc_ref, ss_ref, n1_ref, *rest):
    w_in = rest[:n_w]
    prm, rest = _take_prep_params(rest[n_w:], n_w)
    (pk_ref, v_ref, g_ref, bonus_ref, gates_ref, retq_ref, shift_ref) = rest
    u = _rmsnorm(x_ref[...], n1_ref[...])
    s = _mm(u, w_in, (0, SHIFT_WIDTH))
    shift_ref[...] = s
    _store_rwkv_operands(_rwkv_prep(s, prev_ref[...], prm), pk_ref, v_ref, g_ref, bonus_ref)
    gates_ref[...] = _mm(u, w_in, (GATE_COL0, IN_WIDTH))
    ret = _mm(u, w_in, (RET_COL0, GATE_COL0))
    cc = cc_ref[...]
    ss = ss_ref[...]
    for h in range(RET_HEADS):
        c0 = h * RET_DIM
        retq_ref[:, c0:c0 + RET_DIM] = _rope(ret[:, c0:c0 + RET_DIM], cc, ss)
        retq_ref[:, RET_WIDTH + c0:RET_WIDTH + c0 + RET_DIM] = (
            _rope(ret[:, RET_WIDTH + c0:RET_WIDTH + c0 + RET_DIM], cc, ss) * (RET_DIM ** -0.5))
    retq_ref[:, 2 * RET_WIDTH:] = ret[:, 2 * RET_WIDTH:]


def _const_spec(arr):
    nd = arr.ndim
    return pl.BlockSpec(arr.shape, lambda *_: (0,) * nd, pipeline_mode=pl.Buffered(1))


def _inproj_prompt(x, lead, cc, ss, rt_tab, gc_tab, n1, w_in, prm, nb):
    B, T, _ = x.shape
    tm = RET_CHUNK
    assert lead.shape[0] == tm and T % tm == 0 and B % nb == 0
    TP = tm + T
    n_w = len(w_in)
    consts = (lead, rt_tab, gc_tab, n1) + tuple(w_in) + tuple(prm)
    prow = lambda width: pl.BlockSpec((nb, tm, width), lambda b, t: (b, jnp.maximum(t - 1, 0), 0))
    in_specs = [prow(D_MODEL),
                pl.BlockSpec((tm, LANES), lambda b, t: (t, 0)),
                pl.BlockSpec((tm, LANES), lambda b, t: (t, 0))] + [_const_spec(c) for c in consts]
    out_shape = (
        jax.ShapeDtypeStruct((6, 2, B, RWKV_WIDTH // 2, TP), F32),
        jax.ShapeDtypeStruct((B, T, RWKV_WIDTH), F32),
        jax.ShapeDtypeStruct((B, T, RWKV_WIDTH), F32),
        jax.ShapeDtypeStruct((B, T, 2 * D_MODEL), F32),
        jax.ShapeDtypeStruct((B, T, RET_WIDTH), BF16),
        jax.ShapeDtypeStruct((B, 1, SHIFT_WIDTH), F32),
        jax.ShapeDtypeStruct((B, RET_HEADS, RET_DIM, RET_DIM), F32),
    )
    out_specs = (
        pl.BlockSpec((6, 2, nb, RWKV_WIDTH // 2, tm), lambda b, t: (0, 0, b, 0, t)),
        prow(RWKV_WIDTH), prow(RWKV_WIDTH), prow(2 * D_MODEL), prow(RET_WIDTH),
        pl.BlockSpec((nb, 1, SHIFT_WIDTH), lambda b, t: (b, 0, 0)),
        pl.BlockSpec((nb, RET_HEADS, RET_DIM, RET_DIM), lambda b, t: (b, 0, 0, 0)),
    )
    return pl.pallas_call(
        functools.partial(_inproj_prompt_kernel, n_w),
        grid=(B // nb, TP // tm),
        in_specs=in_specs,
        out_specs=out_specs,
        out_shape=out_shape,
        scratch_shapes=[pltpu.VMEM((nb, tm + SUBLANES, SHIFT_WIDTH), F32),
                        pltpu.VMEM((nb, RET_HEADS, RET_DIM, RET_DIM), F32)],
        compiler_params=pltpu.CompilerParams(
            dimension_semantics=("arbitrary", "arbitrary"), vmem_limit_bytes=VMEM_LIMIT),
        name="inproj_prompt",
    )(x, cc, ss, *consts)


def _inproj_sample(xs, prev, cc, ss, n1, w_in, prm):
    N = xs.shape[0]
    n_w = len(w_in)
    args = (xs, prev, cc, ss, n1) + tuple(w_in) + tuple(prm)
    out_shape = (
        jax.ShapeDtypeStruct((N, 5 * RWKV_WIDTH), F32),
        jax.ShapeDtypeStruct((N, RWKV_WIDTH), F32),
        jax.ShapeDtypeStruct((N, RWKV_WIDTH), F32),
        jax.ShapeDtypeStruct((N, RWKV_WIDTH), F32),
        jax.ShapeDtypeStruct((N, 2 * D_MODEL), F32),
        jax.ShapeDtypeStruct((N, 4 * RET_WIDTH), F32),
        jax.ShapeDtypeStruct((N, SHIFT_WIDTH), F32),
    )
    return pl.pallas_call(
        functools.partial(_inproj_sample_kernel, n_w),
        out_shape=out_shape,
        compiler_params=pltpu.CompilerParams(vmem_limit_bytes=VMEM_LIMIT),
        name="inproj_sample",
    )(*args)


def _rwkv_prompt_kernel(pk_ref, nxt_ref, y_ref, sout_ref, state, sa_buf, kd):
    i = pl.program_id(0)
    tb = pk_ref.shape[0]
    HD = RWKV_HEAD_DIM
    VL = K_LO

    @pl.when(i == 0)
    def _():
        state[...] = jnp.zeros(state.shape, F32)
        sa_buf[...] = jnp.zeros(sa_buf.shape, F32)

    first_half = lax.broadcasted_iota(jnp.int32, (K_LO, LANES), 1) < LANES // 2

    def spread(x, t, q):
        swapped = pltpu.roll(x, LANES // 2, 1)
        kd[t, q, 0:K_LO, :] = jnp.where(first_half, x, swapped)
        kd[t, q, K_LO:HD, :] = jnp.where(first_half, swapped, x)

    for t in range(tb):
        for q in (PK_W, PK_NKK, PK_BB, PK_K, PK_R):
            spread(pk_ref[t, q], t, q)
    spread(nxt_ref[0, 0], tb, PK_NKK)

    def step(t, sa):
        vv = pk_ref[t, PK_V]

        def key_rows(c, accs):
            yacc, sacc = accs
            for j in range(KL_UNROLL):
                k = c * KL_UNROLL + j
                row = lambda tt, q: kd[tt, q, pl.ds(k, 1), :]
                rows = pl.ds(pl.multiple_of(k * VL, VL), VL)
                new = state[rows, :] * row(t, PK_W) + sa * row(t, PK_BB) + vv * row(t, PK_K)
                state[rows, :] = new
                yacc = yacc + new * row(t, PK_R)
                sacc = sacc + new * row(t + 1, PK_NKK)
            return yacc, sacc

        zero = jnp.zeros((VL, LANES), F32)
        yacc, sacc = lax.fori_loop(0, HD // KL_UNROLL, key_rows, (zero, zero))
        y_ref[t] = yacc
        return sacc

    sa_buf[...] = lax.fori_loop(0, tb, step, sa_buf[...])

    @pl.when(i == pl.num_programs(0) - 1)
    def _():
        sout_ref[...] = state[...]


def _rwkv_prompt(pk_t, tb):
    TP = pk_t.shape[0]
    HD = RWKV_HEAD_DIM
    assert (LEAD - N_META) % tb == 0 and LEAD % tb == 0
    skip = (LEAD - N_META) // tb
    meta_blocks = N_META // tb
    return pl.pallas_call(
        _rwkv_prompt_kernel,
        grid=(TP // tb - skip,),
        in_specs=[pl.BlockSpec((tb, 6, K_LO, LANES), lambda i: (i + skip, 0, 0, 0)),
                  pl.BlockSpec((1, 1, K_LO, LANES),
                               lambda i: (jnp.minimum((i + skip + 1) * tb, TP - 1), PK_NKK, 0, 0))],
        out_specs=(pl.BlockSpec((tb, K_LO, LANES), lambda i: (jnp.maximum(i - meta_blocks, 0), 0, 0)),
                   pl.BlockSpec((K_LO * HD, LANES), lambda i: (0, 0))),
        out_shape=(jax.ShapeDtypeStruct((TP - LEAD, K_LO, LANES), F32),
                   jax.ShapeDtypeStruct((K_LO * HD, LANES), F32)),
        scratch_shapes=[pltpu.VMEM((HD * K_LO, LANES), F32), pltpu.VMEM((K_LO, LANES), F32),
                        pltpu.VMEM((tb + 1, 5, HD, LANES), F32)],
        compiler_params=pltpu.CompilerParams(dimension_semantics=("arbitrary",)),
        name="rwkv_prompt",
    )(pk_t, pk_t)


def _lane_col(tile, h):
    lane = lax.broadcasted_iota(jnp.int32, tile.shape, 1)
    return jnp.sum(jnp.where(lane == h, tile, 0.0), axis=-1, keepdims=True)


def _rwkv_sample_kernel(s_ref, pk_ref, v_ref, sout_ref, y_ref):
    def value_row(v, carry):
        S = s_ref[v]
        sa = jnp.sum(S * pk_ref[PK_NKK], axis=0, keepdims=True)
        new = S * pk_ref[PK_W] + sa * pk_ref[PK_BB] + v_ref[pl.ds(v, 1), :] * pk_ref[PK_K]
        sout_ref[v] = new
        y_ref[pl.ds(v, 1), :] = jnp.sum(new * pk_ref[PK_R], axis=0, keepdims=True)
        return carry

    lax.fori_loop(0, s_ref.shape[0], value_row, 0, unroll=2)


def _rwkv_sample(state_t, pk_t, v_t):
    H, HD, _, N = state_t.shape
    return pl.pallas_call(
        _rwkv_sample_kernel,
        grid=(H,),
        in_specs=[pl.BlockSpec((None, HD, HD, N), lambda h: (h, 0, 0, 0)),
                  pl.BlockSpec((5, None, HD, N), lambda h: (0, h, 0, 0)),
                  pl.BlockSpec((None, HD, N), lambda h: (h, 0, 0))],
        out_specs=(pl.BlockSpec((None, HD, HD, N), lambda h: (h, 0, 0, 0)),
                   pl.BlockSpec((None, HD, N), lambda h: (h, 0, 0))),
        out_shape=(jax.ShapeDtypeStruct(state_t.shape, F32),
                   jax.ShapeDtypeStruct((H, HD, N), F32)),
        compiler_params=pltpu.CompilerParams(dimension_semantics=("arbitrary",)),
        name="rwkv_sample",
    )(state_t, pk_t, v_t)


def _ret_sample_kernel(s_ref, qkvg_ref, gam_ref, sout_ref, ob_ref, cross_acc, qk_acc):
    kb = pl.program_id(1)
    nk = s_ref.shape[0]
    gam = gam_ref[...]
    v_tile = qkvg_ref[2]

    @pl.when(kb == 0)
    def _():
        cross_acc[...] = jnp.zeros(cross_acc.shape, F32)
        qk_acc[...] = jnp.zeros(qk_acc.shape, F32)

    def key_row(d, carry):
        cross, qk = carry
        q_row = qkvg_ref[0, pl.ds(kb * nk + d, 1), :]
        k_row = qkvg_ref[1, pl.ds(kb * nk + d, 1), :]
        S = s_ref[d]
        sout_ref[d] = gam * S + k_row * v_tile
        return cross + q_row * S, qk + q_row * k_row

    cross, qk = lax.fori_loop(0, nk, key_row, (cross_acc[...], qk_acc[...]), unroll=2)
    cross_acc[...] = cross
    qk_acc[...] = qk

    @pl.when(kb == pl.num_programs(1) - 1)
    def _():
        o = qk * v_tile + cross * gam
        o = o * lax.rsqrt(jnp.mean(o * o, axis=0, keepdims=True) + RMS_EPS)
        gr = qkvg_ref[3]
        ob_ref[...] = gr * _sigmoid(gr) * o


def _ret_sample(state_t, qkvg_t, gam, key_block):
    H, DK, DV, N = state_t.shape
    assert DK % key_block == 0
    st = pl.BlockSpec((None, key_block, DV, N), lambda h, kb: (h, kb, 0, 0))
    return pl.pallas_call(
        _ret_sample_kernel,
        grid=(H, DK // key_block),
        in_specs=[st, pl.BlockSpec((4, None, DK, N), lambda h, kb: (0, h, 0, 0)),
                  pl.BlockSpec((None, 1, N), lambda h, kb: (h, 0, 0))],
        out_specs=(st, pl.BlockSpec((None, DV, N), lambda h, kb: (h, 0, 0))),
        out_shape=(jax.ShapeDtypeStruct(state_t.shape, F32), jax.ShapeDtypeStruct((H, DV, N), F32)),
        scratch_shapes=[pltpu.VMEM((DV, N), F32), pltpu.VMEM((1, N), F32)],
        compiler_params=pltpu.CompilerParams(dimension_semantics=("arbitrary", "arbitrary")),
        name="ret_sample",
    )(state_t, qkvg_t, gam)


def _mixtail_kernel(n_w, y_feature_major, x_ref, y_ref, g_ref, bonus_ref, ob_ref, gates_ref,
                    lnw_ref, lnb_ref, n2_ref, br_ref, *rest):
    wa = rest[0:n_w]
    wb = rest[n_w:2 * n_w]
    wo = rest[2 * n_w:3 * n_w]
    wr = rest[3 * n_w:3 * n_w + 2]
    x1_ref, u2_ref, comb_ref = rest[3 * n_w + 2:]

    inv_n = 1.0 / RWKV_HEAD_DIM
    if y_feature_major:
        normed = []
        for h in range(RWKV_HEADS):
            yh = jnp.concatenate([y_ref[hi, h * K_LO:(h + 1) * K_LO, :] for hi in range(2)], axis=0)
            d = yh - jnp.sum(yh, axis=0, keepdims=True) * inv_n
            var = jnp.sum(d * d, axis=0, keepdims=True) * inv_n
            normed.append(d * lax.rsqrt(var + LNX_EPS))
        yhat = jnp.concatenate(normed, axis=0).T
    else:
        y = y_ref[...]
        d = y - _head64_sum(y) * inv_n
        var = _head64_sum(d * d) * inv_n
        yhat = d * lax.rsqrt(var + LNX_EPS)
    yn = yhat * lnw_ref[...] + lnb_ref[...]
    ya = (yn + bonus_ref[...]) * g_ref[...]
    branch_a = _mm(ya, wa)
    branch_b = _mm(ob_ref[...], wb)
    gates = gates_ref[...]
    merged = _sigmoid(gates[:, :D_MODEL]) * branch_a + _sigmoid(gates[:, D_MODEL:]) * branch_b
    x1 = x_ref[...] + _mm(merged, wo)
    x1_ref[...] = x1
    u2 = _rmsnorm(x1, n2_ref[...])
    u2_ref[...] = u2.astype(u2_ref.dtype)

    logits = (_mm(u2, wr) + br_ref[...]).T
    tm = logits.shape[1]
    neg = jnp.float32(-jnp.inf)
    big = jnp.float32(1e9)
    first = lambda hit, idx: jnp.min(jnp.where(hit, idx, big), axis=0, keepdims=True)
    lg = logits[N_EXPERTS:N_EXPERTS + N_GROUPS]
    gidx = lax.broadcasted_iota(jnp.int32, lg.shape, 0).astype(F32)
    lg_max = jnp.max(lg, axis=0, keepdims=True)
    p_grp = 1.0 / jnp.sum(jnp.exp(lg - lg_max), axis=0, keepdims=True)
    grp = first(lg == lg_max, gidx)
    eidx = lax.broadcasted_iota(jnp.int32, (N_EXPERTS, tm), 0).astype(F32)
    in_grp = (eidx >= grp * EXPERTS_PER_GROUP) & (eidx < (grp + 1.0) * EXPERTS_PER_GROUP)
    le = jnp.where(in_grp, logits[0:N_EXPERTS], neg)
    v1 = jnp.max(le, axis=0, keepdims=True)
    i1 = first(le == v1, eidx)
    le2 = jnp.where(eidx == i1, neg, le)
    v2 = jnp.max(le2, axis=0, keepdims=True)
    i2 = first(le2 == v2, eidx)
    e2 = jnp.exp(v2 - v1)
    w1 = 1.0 / (1.0 + e2)
    w2 = e2 / (1.0 + e2)
    comb = p_grp * (jnp.where(eidx == i1, w1, 0.0) + jnp.where(eidx == i2, w2, 0.0))
    comb_ref[...] = jnp.concatenate([comb, jnp.zeros((LANES - N_EXPERTS, tm), F32)], axis=0).T


def _mixtail(x, y, g, bonus, ob, gates, lnw, lnb, n2, br, wa, wb, wo, wr, tm, y_feature_major):
    G, n, _ = x.shape
    assert n % tm == 0
    n_w = len(wa)
    consts = (lnw, lnb, n2, br) + tuple(wa) + tuple(wb) + tuple(wo) + tuple(wr)
    row = lambda width: pl.BlockSpec((None, tm, width), lambda b, i: (b, i, 0))
    y_spec = (pl.BlockSpec((2, None, RWKV_WIDTH // 2, tm), lambda b, i: (0, b, 0, i)) if y_feature_major
              else row(RWKV_WIDTH))
    in_specs = [row(D_MODEL), y_spec, row(RWKV_WIDTH), row(RWKV_WIDTH), row(RET_WIDTH),
                row(2 * D_MODEL)] + [_const_spec(c) for c in consts]
    return pl.pallas_call(
        functools.partial(_mixtail_kernel, n_w, y_feature_major),
        grid=(G, n // tm),
        in_specs=in_specs,
        out_specs=(row(D_MODEL), row(D_MODEL), row(LANES)),
        out_shape=(jax.ShapeDtypeStruct((G, n, D_MODEL), F32),
                   jax.ShapeDtypeStruct((G, n, D_MODEL), BF16),
                   jax.ShapeDtypeStruct((G, n, LANES), F32)),
        compiler_params=pltpu.CompilerParams(
            dimension_semantics=("arbitrary", "arbitrary"), vmem_limit_bytes=VMEM_LIMIT),
        name="mixtail",
    )(x, y, g, bonus, ob, gates, *consts)


def _moe_kernel(x1_ref, u2_ref, comb_ref, wg_ref, wu_ref, wd_ref, fg_ref, o_ref, acc):
    grp = pl.program_id(1)

    @pl.when(grp == 0)
    def _():
        acc[...] = jnp.zeros(acc.shape, F32)

    u2 = u2_ref[...]
    comb = comb_ref[...]
    lane = lax.broadcasted_iota(jnp.int32, comb.shape, 1)
    total = acc[...]
    for j in range(EXPERTS_PER_GROUP):
        c = jnp.sum(jnp.where(lane == grp * EXPERTS_PER_GROUP + j, comb, 0.0), axis=-1, keepdims=True)
        hg = jnp.dot(u2, wg_ref[j], preferred_element_type=F32)
        hu = jnp.dot(u2, wu_ref[j], preferred_element_type=F32)
        h = hg * _sigmoid(hg) * hu * c
        total = total + jnp.dot(h.astype(BF16), wd_ref[j], preferred_element_type=F32)
    acc[...] = total

    @pl.when(grp == pl.num_programs(1) - 1)
    def _():
        o_ref[...] = _rmsnorm(x1_ref[...] + acc[...], fg_ref[...])


def _moe(x1, u2, comb, wg, wu, wd, fg, tm):
    N = x1.shape[0]
    E = EXPERTS_PER_GROUP
    return pl.pallas_call(
        _moe_kernel,
        grid=(N // tm, N_GROUPS),
        in_specs=[pl.BlockSpec((tm, D_MODEL), lambda i, g: (i, 0)),
                  pl.BlockSpec((tm, D_MODEL), lambda i, g: (i, 0)),
                  pl.BlockSpec((tm, LANES), lambda i, g: (i, 0)),
                  pl.BlockSpec((E, D_MODEL, EXPERT_FF), lambda i, g: (g, 0, 0)),
                  pl.BlockSpec((E, D_MODEL, EXPERT_FF), lambda i, g: (g, 0, 0)),
                  pl.BlockSpec((E, EXPERT_FF, D_MODEL), lambda i, g: (g, 0, 0)),
                  pl.BlockSpec((1, D_MODEL), lambda i, g: (0, 0))],
        out_specs=pl.BlockSpec((tm, D_MODEL), lambda i, g: (i, 0)),
        out_shape=jax.ShapeDtypeStruct((N, D_MODEL), F32),
        scratch_shapes=[pltpu.VMEM((tm, D_MODEL), F32)],
        compiler_params=pltpu.CompilerParams(
            dimension_semantics=("arbitrary", "arbitrary"), vmem_limit_bytes=VMEM_LIMIT),
        name="moe",
    )(x1, u2, comb, wg, wu, wd, fg)


def _pieces(w, n):
    return tuple(_split_bf16(w.astype(F32), n))


def _rope_tables(pos):
    half = RET_DIM // 2
    inv = ROPE_BASE ** (-jnp.arange(half, dtype=F32) / half)
    ang = pos.astype(F32)[:, None] * inv[None, :]
    c = jnp.cos(ang)
    s = jnp.sin(ang)
    cc = jnp.repeat(c, 2, axis=-1)
    ss = jnp.stack([-s, s], axis=-1).reshape(pos.shape[0], RET_DIM)
    return cc, ss


def _retention_tables():
    C = RET_CHUNK
    log_gamma = jnp.log(1.0 - 2.0 ** (-5.0 - jnp.arange(RET_HEADS, dtype=F32)))
    idx = jnp.arange(C, dtype=F32)
    diff = idx[:, None] - idx[None, :]
    lg = log_gamma[:, None, None]
    dmask = jnp.where(diff[None] >= 0, jnp.exp(jnp.maximum(diff, 0.0)[None] * lg), 0.0)
    q_dec = jnp.exp((idx[None, :] + 1.0) * log_gamma[:, None])
    k_dec = jnp.exp((C - 1.0 - idx[None, :]) * log_gamma[:, None])
    bc = lambda a: jnp.broadcast_to(a[:, :, None], (RET_HEADS, C, C))
    rt_tab = jnp.stack([dmask, bc(q_dec), bc(k_dec)], axis=1)
    gc_tab = jnp.broadcast_to(jnp.exp(C * log_gamma)[:, None, None], (RET_HEADS, SUBLANES, RET_DIM))
    gam1 = jnp.broadcast_to(jnp.exp(log_gamma)[:, None], (RET_HEADS, RET_DIM))
    return rt_tab, gc_tab, gam1


def kernel(x_prompt, x_sample, state_wkv, state_shift, state_ret, meta_tokens, norm1_g, w_in, shift_mu, decay_w0, decay_up, aaa_a0, aaa_up, gate_up, k_k, k_a, r_k, lnx_w, lnx_b, w_branch_a, w_branch_b, w_out, norm2_g, router_group_w, router_group_b, router_expert_w, router_expert_b, expert_w_gate, expert_w_up, expert_w_down, final_norm_g):
    B, T, D = x_prompt.shape
    NS = x_sample.shape[0]
    TP = LEAD + T
    H, HD = RWKV_HEADS, RWKV_HEAD_DIM
    l = 0
    row = lambda a: a.reshape(1, -1)

    zpad = jnp.zeros((LORA_PAD - 64, RWKV_WIDTH), F32)
    dup_pad = jnp.concatenate([decay_up[l], zpad], axis=0)
    aup_pad = jnp.concatenate([zpad, aaa_up[l]], axis=0)
    wr_full = jnp.concatenate([router_expert_w[l], router_group_w[l]], axis=1)
    wr_full = jnp.pad(wr_full, ((0, 0), (0, LANES - wr_full.shape[1])))
    br_full = jnp.pad(jnp.concatenate([router_expert_b[l], router_group_b[l]]), (0, LANES - N_EXPERTS - N_GROUPS))
    wr = _pieces(wr_full, 2)
    wg = expert_w_gate[l].astype(BF16)
    wu = expert_w_up[l].astype(BF16)
    wd = expert_w_down[l].astype(BF16)
    rt_tab, gc_tab, gam1 = _retention_tables()

    def prep_params(n):
        return (row(shift_mu[l]), row(decay_w0[l]), _pieces(dup_pad, n), row(aaa_a0[l]), _pieces(aup_pad, n),
                _pieces(gate_up[l], n), row(k_k[l]), row(k_a[l]), row(r_k[l]))

    def flat_params(p):
        out = []
        for a in p:
            out.extend(a if isinstance(a, tuple) else (a,))
        return tuple(out)

    def tail(n, x, y, g, bonus, ob, gates, tm_tail, tm_moe, y_feature_major):
        x1, u2, comb = _mixtail(x, y, g, bonus, ob, gates, row(lnx_w[l]), row(lnx_b[l]), row(norm2_g[l]),
                                row(br_full), _pieces(w_branch_a[l], n), _pieces(w_branch_b[l], n),
                                _pieces(w_out[l], n), wr, tm_tail, y_feature_major)
        flat = lambda a: a.reshape(-1, a.shape[-1])
        return _moe(flat(x1), flat(u2), flat(comb), wg, wu, wd, row(final_norm_g), tm_moe)

    lead = jnp.concatenate([jnp.zeros((LEAD - N_META, D), F32), meta_tokens.astype(F32)], axis=0)
    cc_p, ss_p = _rope_tables(jnp.arange(TP, dtype=jnp.int32) - (LEAD - N_META))
    pk, g, bonus, gates, ob, shift_p, ret_p = _inproj_prompt(
        x_prompt, lead, cc_p, ss_p, rt_tab, gc_tab, row(norm1_g[l]), _pieces(w_in[l], 1),
        flat_params(prep_params(1)), nb=4)

    pk_t = pk.reshape(6, 2 * B * H, K_LO, TP).transpose(3, 0, 2, 1)

    shift_p, xs = lax.optimization_barrier((shift_p, x_sample.reshape(NS, D)))
    cc_s, ss_s = _rope_tables(jnp.full((NS,), PAST_LEN, jnp.int32))
    pk_s, v_s, g_s, bonus_s, gates_s, retq, shift_s = _inproj_sample(
        xs, state_shift[l], cc_s, ss_s, row(norm1_g[l]), _pieces(w_in[l], 2), flat_params(prep_params(2)))
    wkv_st, y_st = _rwkv_sample(state_wkv[l].transpose(1, 2, 3, 0),
                                pk_s.reshape(NS, 5, H, HD).transpose(1, 2, 3, 0),
                                v_s.reshape(NS, H, HD).transpose(1, 2, 0))
    wkv_s = wkv_st.transpose(3, 0, 1, 2)
    y_s = y_st.transpose(2, 0, 1).reshape(NS, RWKV_WIDTH)
    ret_st, ob_st = _ret_sample(state_ret[l].transpose(1, 2, 3, 0),
                                retq.reshape(NS, 4, RET_HEADS, RET_DIM).transpose(1, 2, 3, 0),
                                jnp.broadcast_to(gam1[:, :1, None], (RET_HEADS, 1, NS)), key_block=64)
    ret_s = ret_st.transpose(3, 0, 1, 2)
    ob_s = ob_st.transpose(2, 0, 1).reshape(1, NS, RET_WIDTH)
    y_sample = tail(2, xs[None], y_s[None], g_s[None], bonus_s[None], ob_s, gates_s[None], NS, NS, False)

    pk_t, y_sample, wkv_s, ret_s = lax.optimization_barrier((pk_t, y_sample, wkv_s, ret_s))
    y_t, s_fin = _rwkv_prompt(pk_t, tb=16)
    y_fm = y_t.transpose(2, 1, 0).reshape(2, B, RWKV_WIDTH // 2, T)
    wkv_p = s_fin.reshape(HD, K_LO, 2, B, H).transpose(3, 4, 2, 1, 0).reshape(B, H, HD, HD)
    y_prompt = tail(1, x_prompt, y_fm, g, bonus, ob, gates, 512, 1024, True).reshape(B, T, D)

    return (y_prompt, y_sample.reshape(NS, 1, D),
            wkv_p[None], shift_p.reshape(1, B, SHIFT_WIDTH), ret_p[None],
            wkv_s[None], shift_s[None], ret_s[None])
```

```python
import functools

import jax
import jax.numpy as jnp
from jax import lax
from jax.experimental import pallas as pl
from jax.experimental.pallas import tpu as pltpu

F32 = jnp.float32
BF16 = jnp.bfloat16

D_MODEL = 1024
N_META = 16
RWKV_WIDTH = 512
RWKV_HEADS = 8
RWKV_HEAD_DIM = 64
LORA_PAD = 128
GATE_LORA = 128
SHIFT_WIDTH = 3 * RWKV_WIDTH + 64 + 64 + GATE_LORA
LNX_EPS = 64e-5
RET_WIDTH = 512
RET_HEADS = 4
RET_DIM = 128
RET_CHUNK = 128
ROPE_BASE = 10000.0
RET_COL0 = SHIFT_WIDTH
GATE_COL0 = SHIFT_WIDTH + 4 * RET_WIDTH
IN_WIDTH = GATE_COL0 + 2 * D_MODEL
N_GROUPS = 4
EXPERTS_PER_GROUP = 4
N_EXPERTS = 16
EXPERT_FF = 256
RMS_EPS = 1e-6
PAST_LEN = 16384

LANES = 128
SUBLANES = 8
VMEM_LIMIT = 60 * 1024 * 1024

LEAD = 128
PK_W, PK_NKK, PK_BB, PK_K, PK_R, PK_V = range(6)
K_LO = 32
KL_UNROLL = 8


def _split_bf16(a, nterms):
    pieces = []
    rem = a
    for _ in range(nterms - 1):
        c = rem * 65537.0
        hi = c - (c - rem)
        pieces.append(hi.astype(BF16))
        rem = rem - hi
    pieces.append(rem.astype(BF16))
    return pieces


def _mm(a, w_refs, cols=None):
    def w(i):
        r = w_refs[i]
        return r[...] if cols is None else r[:, cols[0]:cols[1]]

    if a.dtype == BF16:
        acc = jnp.dot(a, w(0), preferred_element_type=F32)
        if len(w_refs) > 1:
            acc = acc + jnp.dot(a, w(1), preferred_element_type=F32)
        return acc
    if len(w_refs) == 1:
        return jnp.dot(a.astype(BF16), w(0), preferred_element_type=F32)
    a_hi, a_lo = _split_bf16(a, 2)
    acc = jnp.dot(a_hi, w(0), preferred_element_type=F32)
    acc = acc + jnp.dot(a_lo, w(0), preferred_element_type=F32)
    acc = acc + jnp.dot(a_hi, w(1), preferred_element_type=F32)
    return acc


def _rmsnorm(x, g):
    return x * lax.rsqrt(jnp.mean(x * x, axis=-1, keepdims=True) + RMS_EPS) * g


def _sigmoid(x):
    return 1.0 / (1.0 + jnp.exp(-x))


def _head64_sum(x):
    lane = lax.broadcasted_iota(jnp.int32, (x.shape[0], LANES), 1)
    lo = lane < RWKV_HEAD_DIM
    outs = []
    for j in range(x.shape[1] // LANES):
        xj = x[:, j * LANES:(j + 1) * LANES]
        s_lo = jnp.sum(jnp.where(lo, xj, 0.0), axis=-1, keepdims=True)
        s_hi = jnp.sum(jnp.where(lo, 0.0, xj), axis=-1, keepdims=True)
        outs.append(jnp.where(lo, s_lo, s_hi))
    return jnp.concatenate(outs, axis=-1)


def _rope(x, cc, ss):
    lane = lax.broadcasted_iota(jnp.int32, x.shape, 1)
    even = (lane & 1) == 0
    swapped = jnp.where(even, pltpu.roll(x, LANES - 1, 1), pltpu.roll(x, 1, 1))
    return x * cc + swapped * ss


def _rwkv_prep(s, prev, prm):
    (mu, w0, dup, a0, aup, gup, k_k, k_a, r_k) = prm
    xs = s + mu[...] * (prev - s)
    W = RWKV_WIDTH
    r = xs[:, 0:W]
    k = xs[:, W:2 * W]
    v = xs[:, 2 * W:3 * W]
    wa = xs[:, 3 * W:3 * W + LORA_PAD]
    gd = xs[:, 3 * W + LORA_PAD:]
    dec = _mm(jnp.tanh(wa), dup)
    z = -(w0[...] + dec)
    softplus = jnp.maximum(z, 0.0) + jnp.log1p(jnp.exp(-jnp.abs(z)))
    decay = jnp.exp(-jnp.exp(-softplus - 0.5))
    a = _sigmoid(a0[...] + _mm(wa, aup))
    g = _mm(_sigmoid(gd), gup)
    kk = k * k_k[...]
    kk = kk / jnp.maximum(jnp.sqrt(_head64_sum(kk * kk)), 1e-12)
    k_mod = k * (1.0 + (a - 1.0) * k_a[...])
    bonus = _head64_sum(r * k_mod * r_k[...]) * v
    return r, decay, k_mod, v, kk, a, g, bonus


def _take_prep_params(refs, n_w):
    it = iter(refs)
    take = lambda n: tuple(next(it) for _ in range(n))
    (mu, w0), dup, (a0,), aup, gup, (k_k, k_a, r_k) = take(2), take(n_w), take(1), take(n_w), take(n_w), take(3)
    return (mu, w0, dup, a0, aup, gup, k_k, k_a, r_k), tuple(it)


def _store_rwkv_operands(ops, pk_ref, v_ref, g_ref, bonus_ref):
    r, decay, k_mod, v, kk, a, g, bonus = ops
    W = RWKV_WIDTH
    pk_ref[:, PK_W * W:(PK_W + 1) * W] = decay
    pk_ref[:, PK_NKK * W:(PK_NKK + 1) * W] = -kk
    pk_ref[:, PK_BB * W:(PK_BB + 1) * W] = kk * a
    pk_ref[:, PK_K * W:(PK_K + 1) * W] = k_mod
    pk_ref[:, PK_R * W:(PK_R + 1) * W] = r
    v_ref[...] = v
    g_ref[...] = g
    bonus_ref[...] = bonus


def _inproj_prompt_kernel(n_w, x_ref, cc_ref, ss_ref, lead_ref, rt_ref, gc_ref, n1_ref, *rest):
    w_in = rest[:n_w]
    prm, rest = _take_prep_params(rest[n_w:], n_w)
    (pk_ref, g_ref, bonus_ref, gates_ref, ob_ref, shift_ref, ret_ref, prev_buf, ret_state) = rest
    t = pl.program_id(1)
    nb, tm, _ = x_ref.shape
    rows = [slice(b * tm, (b + 1) * tm) for b in range(nb)]

    @pl.when(t == 0)
    def _():
        prev_buf[:, 0:SUBLANES, :] = jnp.zeros((nb, SUBLANES, SHIFT_WIDTH), F32)
        ret_state[...] = jnp.zeros(ret_state.shape, F32)

    x = jnp.where(t == 0, lead_ref[...][None], x_ref[...])
    u = _rmsnorm(x.reshape(nb * tm, D_MODEL), n1_ref[...])
    u_b = u.astype(BF16) if n_w == 1 else u

    s = _mm(u_b, w_in, (0, SHIFT_WIDTH))
    prevs = []
    for b in range(nb):
        s_b = s[rows[b]]
        prev_buf[b, SUBLANES:SUBLANES + tm, :] = s_b
        prevs.append(prev_buf[b, SUBLANES - 1:SUBLANES - 1 + tm, :])
        prev_buf[b, SUBLANES - 1:SUBLANES, :] = s_b[tm - 1:tm, :]
        shift_ref[b] = s_b[tm - 1:tm, :]
    prev = jnp.concatenate(prevs, axis=0)

    r, decay, k_mod, v, kk, a, g, bonus = _rwkv_prep(s, prev, prm)
    W = RWKV_WIDTH
    for b in range(nb):
        rb = rows[b]
        for q, val in ((PK_W, decay), (PK_NKK, -kk), (PK_BB, kk * a), (PK_K, k_mod), (PK_R, r), (PK_V, v)):
            fm = val[rb].T
            for hi in range(2):
                pk_ref[q, hi, b] = jnp.concatenate(
                    [fm[h * RWKV_HEAD_DIM + hi * K_LO:h * RWKV_HEAD_DIM + (hi + 1) * K_LO] for h in range(RWKV_HEADS)],
                    axis=0)
        g_ref[b] = g[rb]
        bonus_ref[b] = bonus[rb]

    gates = _mm(u_b, w_in, (GATE_COL0, IN_WIDTH))
    for b in range(nb):
        gates_ref[b] = gates[rows[b]]

    ret = _mm(u_b, w_in, (RET_COL0, GATE_COL0))
    cc = cc_ref[...]
    ss = ss_ref[...]
    for b in range(nb):
        rb = rows[b]
        for h in range(RET_HEADS):
            c0 = h * RET_DIM
            qc = _rope(ret[rb, c0:c0 + RET_DIM], cc, ss).astype(BF16)
            kc = _rope(ret[rb, RET_WIDTH + c0:RET_WIDTH + c0 + RET_DIM], cc, ss) * (RET_DIM ** -0.5)
            vc = ret[rb, 2 * RET_WIDTH + c0:2 * RET_WIDTH + c0 + RET_DIM].astype(BF16)
            gr = ret[rb, 3 * RET_WIDTH + c0:3 * RET_WIDTH + c0 + RET_DIM]
            state = ret_state[b, h]
            scores = lax.dot_general(qc, kc.astype(BF16), (((1,), (1,)), ((), ())),
                                     preferred_element_type=F32) * rt_ref[h, 0]
            intra = jnp.dot(scores.astype(BF16), vc, preferred_element_type=F32)
            cross = jnp.dot(qc, state.astype(BF16), preferred_element_type=F32) * rt_ref[h, 1]
            kt = (kc * rt_ref[h, 2]).T.astype(BF16)
            ret_state[b, h] = gc_ref[h, 0:1, :] * state + jnp.dot(kt, vc, preferred_element_type=F32)
            o = intra + cross
            o = o * lax.rsqrt(jnp.mean(o * o, axis=-1, keepdims=True) + RMS_EPS)
            ob_ref[b, :, c0:c0 + RET_DIM] = (gr * _sigmoid(gr) * o).astype(ob_ref.dtype)

    @pl.when(t == pl.num_programs(1) - 1)
    def _():
        ret_ref[...] = ret_state[...]


def _inproj_sample_kernel(n_w, x_ref, prev_ref, cc_ref, ss_ref, n1_ref, *rest):
    w_in = rest[:n_w]
    prm, rest = _take_prep_params(rest[n_w:], n_w)
    (pk_ref, v_ref, g_ref, bonus_ref, gates_ref, retq_ref, shift_ref) = rest
    u = _rmsnorm(x_ref[...], n1_ref[...])
    s = _mm(u, w_in, (0, SHIFT_WIDTH))
    shift_ref[...] = s
    _store_rwkv_operands(_rwkv_prep(s, prev_ref[...], prm), pk_ref, v_ref, g_ref, bonus_ref)
    gates_ref[...] = _mm(u, w_in, (GATE_COL0, IN_WIDTH))
    ret = _mm(u, w_in, (RET_COL0, GATE_COL0))
    cc = cc_ref[...]
    ss = ss_ref[...]
    for h in range(RET_HEADS):
        c0 = h * RET_DIM
        retq_ref[:, c0:c0 + RET_DIM] = _rope(ret[:, c0:c0 + RET_DIM], cc, ss)
        retq_ref[:, RET_WIDTH + c0:RET_WIDTH + c0 + RET_DIM] = (
            _rope(ret[:, RET_WIDTH + c0:RET_WIDTH + c0 + RET_DIM], cc, ss) * (RET_DIM ** -0.5))
    retq_ref[:, 2 * RET_WIDTH:] = ret[:, 2 * RET_WIDTH:]


def _const_spec(arr):
    nd = arr.ndim
    return pl.BlockSpec(arr.shape, lambda *_: (0,) * nd, pipeline_mode=pl.Buffered(1))


def _inproj_prompt(x, lead, cc, ss, rt_tab, gc_tab, n1, w_in, prm, nb):
    B, T, _ = x.shape
    tm = RET_CHUNK
    assert lead.shape[0] == tm and T % tm == 0 and B % nb == 0
    TP = tm + T
    n_w = len(w_in)
    consts = (lead, rt_tab, gc_tab, n1) + tuple(w_in) + tuple(prm)
    prow = lambda width: pl.BlockSpec((nb, tm, width), lambda b, t: (b, jnp.maximum(t - 1, 0), 0))
    in_specs = [prow(D_MODEL),
                pl.BlockSpec((tm, LANES), lambda b, t: (t, 0)),
                pl.BlockSpec((tm, LANES), lambda b, t: (t, 0))] + [_const_spec(c) for c in consts]
    out_shape = (
        jax.ShapeDtypeStruct((6, 2, B, RWKV_WIDTH // 2, TP), F32),
        jax.ShapeDtypeStruct((B, T, RWKV_WIDTH), F32),
        jax.ShapeDtypeStruct((B, T, RWKV_WIDTH), F32),
        jax.ShapeDtypeStruct((B, T, 2 * D_MODEL), F32),
        jax.ShapeDtypeStruct((B, T, RET_WIDTH), BF16),
        jax.ShapeDtypeStruct((B, 1, SHIFT_WIDTH), F32),
        jax.ShapeDtypeStruct((B, RET_HEADS, RET_DIM, RET_DIM), F32),
    )
    out_specs = (
        pl.BlockSpec((6, 2, nb, RWKV_WIDTH // 2, tm), lambda b, t: (0, 0, b, 0, t)),
        prow(RWKV_WIDTH), prow(RWKV_WIDTH), prow(2 * D_MODEL), prow(RET_WIDTH),
        pl.BlockSpec((nb, 1, SHIFT_WIDTH), lambda b, t: (b, 0, 0)),
        pl.BlockSpec((nb, RET_HEADS, RET_DIM, RET_DIM), lambda b, t: (b, 0, 0, 0)),
    )
    return pl.pallas_call(
        functools.partial(_inproj_prompt_kernel, n_w),
        grid=(B // nb, TP // tm),
        in_specs=in_specs,
        out_specs=out_specs,
        out_shape=out_shape,
        scratch_shapes=[pltpu.VMEM((nb, tm + SUBLANES, SHIFT_WIDTH), F32),
                        pltpu.VMEM((nb, RET_HEADS, RET_DIM, RET_DIM), F32)],
        compiler_params=pltpu.CompilerParams(
            dimension_semantics=("arbitrary", "arbitrary"), vmem_limit_bytes=VMEM_LIMIT),
        name="inproj_prompt",
    )(x, cc, ss, *consts)


def _inproj_sample(xs, prev, cc, ss, n1, w_in, prm):
    N = xs.shape[0]
    n_w = len(w_in)
    args = (xs, prev, cc, ss, n1) + tuple(w_in) + tuple(prm)
    out_shape = (
        jax.ShapeDtypeStruct((N, 5 * RWKV_WIDTH), F32),
        jax.ShapeDtypeStruct((N, RWKV_WIDTH), F32),
        jax.ShapeDtypeStruct((N, RWKV_WIDTH), F32),
        jax.ShapeDtypeStruct((N, RWKV_WIDTH), F32),
        jax.ShapeDtypeStruct((N, 2 * D_MODEL), F32),
        jax.ShapeDtypeStruct((N, 4 * RET_WIDTH), F32),
        jax.ShapeDtypeStruct((N, SHIFT_WIDTH), F32),
    )
    return pl.pallas_call(
        functools.partial(_inproj_sample_kernel, n_w),
        out_shape=out_shape,
        compiler_params=pltpu.CompilerParams(vmem_limit_bytes=VMEM_LIMIT),
        name="inproj_sample",
    )(*args)


def _rwkv_prompt_kernel(pk_ref, nxt_ref, y_ref, sout_ref, state, sa_buf, kd):
    i = pl.program_id(0)
    tb = pk_ref.shape[0]
    HD = RWKV_HEAD_DIM
    VL = K_LO

    @pl.when(i == 0)
    def _():
        state[...] = jnp.zeros(state.shape, F32)
        sa_buf[...] = jnp.zeros(sa_buf.shape, F32)

    first_half = lax.broadcasted_iota(jnp.int32, (K_LO, LANES), 1) < LANES // 2

    def spread(x, t, q):
        swapped = pltpu.roll(x, LANES // 2, 1)
        kd[t, q, 0:K_LO, :] = jnp.where(first_half, x, swapped)
        kd[t, q, K_LO:HD, :] = jnp.where(first_half, swapped, x)

    for t in range(tb):
        for q in (PK_W, PK_NKK, PK_BB, PK_K, PK_R):
            spread(pk_ref[t, q], t, q)
    spread(nxt_ref[0, 0], tb, PK_NKK)

    def step(t, sa):
        vv = pk_ref[t, PK_V]

        def key_rows(c, accs):
            yacc, sacc = accs
            for j in range(KL_UNROLL):
                k = c * KL_UNROLL + j
                row = lambda tt, q: kd[tt, q, pl.ds(k, 1), :]
                rows = pl.ds(pl.multiple_of(k * VL, VL), VL)
                new = state[rows, :] * row(t, PK_W) + sa * row(t, PK_BB) + vv * row(t, PK_K)
                state[rows, :] = new
                yacc = yacc + new * row(t, PK_R)
                sacc = sacc + new * row(t + 1, PK_NKK)
            return yacc, sacc

        zero = jnp.zeros((VL, LANES), F32)
        yacc, sacc = lax.fori_loop(0, HD // KL_UNROLL, key_rows, (zero, zero))
        y_ref[t] = yacc
        return sacc

    sa_buf[...] = lax.fori_loop(0, tb, step, sa_buf[...])

    @pl.when(i == pl.num_programs(0) - 1)
    def _():
        sout_ref[...] = state[...]


def _rwkv_prompt(pk_t, tb):
    TP = pk_t.shape[0]
    HD = RWKV_HEAD_DIM
    assert (LEAD - N_META) % tb == 0 and LEAD % tb == 0
    skip = (LEAD - N_META) // tb
    meta_blocks = N_META // tb
    return pl.pallas_call(
        _rwkv_prompt_kernel,
        grid=(TP // tb - skip,),
        in_specs=[pl.BlockSpec((tb, 6, K_LO, LANES), lambda i: (i + skip, 0, 0, 0)),
                  pl.BlockSpec((1, 1, K_LO, LANES),
                               lambda i: (jnp.minimum((i + skip + 1) * tb, TP - 1), PK_NKK, 0, 0))],
        out_specs=(pl.BlockSpec((tb, K_LO, LANES), lambda i: (jnp.maximum(i - meta_blocks, 0), 0, 0)),
                   pl.BlockSpec((K_LO * HD, LANES), lambda i: (0, 0))),
        out_shape=(jax.ShapeDtypeStruct((TP - LEAD, K_LO, LANES), F32),
                   jax.ShapeDtypeStruct((K_LO * HD, LANES), F32)),
        scratch_shapes=[pltpu.VMEM((HD * K_LO, LANES), F32), pltpu.VMEM((K_LO, LANES), F32),
                        pltpu.VMEM((tb + 1, 5, HD, LANES), F32)],
        compiler_params=pltpu.CompilerParams(dimension_semantics=("arbitrary",)),
        name="rwkv_prompt",
    )(pk_t, pk_t)


def _lane_col(tile, h):
    lane = lax.broadcasted_iota(jnp.int32, tile.shape, 1)
    return jnp.sum(jnp.where(lane == h, tile, 0.0), axis=-1, keepdims=True)


def _rwkv_sample_kernel(s_ref, pk_ref, v_ref, sout_ref, y_ref):
    def value_row(v, carry):
        S = s_ref[v]
        sa = jnp.sum(S * pk_ref[PK_NKK], axis=0, keepdims=True)
        new = S * pk_ref[PK_W] + sa * pk_ref[PK_BB] + v_ref[pl.ds(v, 1), :] * pk_ref[PK_K]
        sout_ref[v] = new
        y_ref[pl.ds(v, 1), :] = jnp.sum(new * pk_ref[PK_R], axis=0, keepdims=True)
        return carry

    lax.fori_loop(0, s_ref.shape[0], value_row, 0, unroll=2)


def _rwkv_sample(state_t, pk_t, v_t):
    H, HD, _, N = state_t.shape
    return pl.pallas_call(
        _rwkv_sample_kernel,
        grid=(H,),
        in_specs=[pl.BlockSpec((None, HD, HD, N), lambda h: (h, 0, 0, 0)),
                  pl.BlockSpec((5, None, HD, N), lambda h: (0, h, 0, 0)),
                  pl.BlockSpec((None, HD, N), lambda h: (h, 0, 0))],
        out_specs=(pl.BlockSpec((None, HD, HD, N), lambda h: (h, 0, 0, 0)),
                   pl.BlockSpec((None, HD, N), lambda h: (h, 0, 0))),
        out_shape=(jax.ShapeDtypeStruct(state_t.shape, F32),
                   jax.ShapeDtypeStruct((H, HD, N), F32)),
        compiler_params=pltpu.CompilerParams(dimension_semantics=("arbitrary",)),
        name="rwkv_sample",
    )(state_t, pk_t, v_t)


def _ret_sample_kernel(s_ref, qt_ref, kt_ref, v_ref, g_ref, gam_ref, sout_ref, ob_ref):
    gam = gam_ref[...][:, None, :]
    for b in range(s_ref.shape[0]):
        q_tile = qt_ref[b]
        k_tile = kt_ref[b]
        qc = jnp.stack([_lane_col(q_tile, h) for h in range(RET_HEADS)], axis=0)
        kc = jnp.stack([_lane_col(k_tile, h) for h in range(RET_HEADS)], axis=0)
        vrow = v_ref[b][:, None, :]
        S = s_ref[b]
        qk = jnp.sum(qc * kc, axis=1, keepdims=True)
        cross = jnp.sum(qc * S, axis=1, keepdims=True)
        o = qk * vrow + cross * gam
        sout_ref[b] = gam * S + kc * vrow
        o = o * lax.rsqrt(jnp.mean(o * o, axis=-1, keepdims=True) + RMS_EPS)
        gr = g_ref[b][:, None, :]
        ob_ref[b] = (gr * _sigmoid(gr) * o)[:, 0, :]


def _ret_sample(state, q_t, k_t, v_h, g_h, gam, bt):
    N = state.shape[0]
    H, DK = RET_HEADS, RET_DIM
    col = pl.BlockSpec((bt, DK, H), lambda i: (i, 0, 0))
    rowspec = pl.BlockSpec((bt, H, DK), lambda i: (i, 0, 0))
    st = pl.BlockSpec((bt, H, DK, DK), lambda i: (i, 0, 0, 0))
    return pl.pallas_call(
        _ret_sample_kernel,
        grid=(N // bt,),
        in_specs=[st, col, col, rowspec, rowspec, pl.BlockSpec((H, DK), lambda i: (0, 0))],
        out_specs=(st, rowspec),
        out_shape=(jax.ShapeDtypeStruct(state.shape, F32), jax.ShapeDtypeStruct((N, H, DK), F32)),
        compiler_params=pltpu.CompilerParams(dimension_semantics=("arbitrary",)),
        name="ret_sample",
    )(state, q_t, k_t, v_h, g_h, gam)


def _mixtail_kernel(n_w, y_feature_major, x_ref, y_ref, g_ref, bonus_ref, ob_ref, gates_ref,
                    lnw_ref, lnb_ref, n2_ref, br_ref, *rest):
    wa = rest[0:n_w]
    wb = rest[n_w:2 * n_w]
    wo = rest[2 * n_w:3 * n_w]
    wr = rest[3 * n_w:3 * n_w + 2]
    x1_ref, u2_ref, comb_ref = rest[3 * n_w + 2:]

    inv_n = 1.0 / RWKV_HEAD_DIM
    if y_feature_major:
        normed = []
        for h in range(RWKV_HEADS):
            yh = jnp.concatenate([y_ref[hi, h * K_LO:(h + 1) * K_LO, :] for hi in range(2)], axis=0)
            d = yh - jnp.sum(yh, axis=0, keepdims=True) * inv_n
            var = jnp.sum(d * d, axis=0, keepdims=True) * inv_n
            normed.append(d * lax.rsqrt(var + LNX_EPS))
        yhat = jnp.concatenate(normed, axis=0).T
    else:
        y = y_ref[...]
        d = y - _head64_sum(y) * inv_n
        var = _head64_sum(d * d) * inv_n
        yhat = d * lax.rsqrt(var + LNX_EPS)
    yn = yhat * lnw_ref[...] + lnb_ref[...]
    ya = (yn + bonus_ref[...]) * g_ref[...]
    branch_a = _mm(ya, wa)
    branch_b = _mm(ob_ref[...], wb)
    gates = gates_ref[...]
    merged = _sigmoid(gates[:, :D_MODEL]) * branch_a + _sigmoid(gates[:, D_MODEL:]) * branch_b
    x1 = x_ref[...] + _mm(merged, wo)
    x1_ref[...] = x1
    u2 = _rmsnorm(x1, n2_ref[...])
    u2_ref[...] = u2.astype(u2_ref.dtype)

    logits = (_mm(u2, wr) + br_ref[...]).T
    tm = logits.shape[1]
    neg = jnp.float32(-jnp.inf)
    big = jnp.float32(1e9)
    first = lambda hit, idx: jnp.min(jnp.where(hit, idx, big), axis=0, keepdims=True)
    lg = logits[N_EXPERTS:N_EXPERTS + N_GROUPS]
    gidx = lax.broadcasted_iota(jnp.int32, lg.shape, 0).astype(F32)
    lg_max = jnp.max(lg, axis=0, keepdims=True)
    p_grp = 1.0 / jnp.sum(jnp.exp(lg - lg_max), axis=0, keepdims=True)
    grp = first(lg == lg_max, gidx)
    eidx = lax.broadcasted_iota(jnp.int32, (N_EXPERTS, tm), 0).astype(F32)
    in_grp = (eidx >= grp * EXPERTS_PER_GROUP) & (eidx < (grp + 1.0) * EXPERTS_PER_GROUP)
    le = jnp.where(in_grp, logits[0:N_EXPERTS], neg)
    v1 = jnp.max(le, axis=0, keepdims=True)
    i1 = first(le == v1, eidx)
    le2 = jnp.where(eidx == i1, neg, le)
    v2 = jnp.max(le2, axis=0, keepdims=True)
    i2 = first(le2 == v2, eidx)
    e2 = jnp.exp(v2 - v1)
    w1 = 1.0 / (1.0 + e2)
    w2 = e2 / (1.0 + e2)
    comb = p_grp * (jnp.where(eidx == i1, w1, 0.0) + jnp.where(eidx == i2, w2, 0.0))
    comb_ref[...] = jnp.concatenate([comb, jnp.zeros((LANES - N_EXPERTS, tm), F32)], axis=0).T


def _mixtail(x, y, g, bonus, ob, gates, lnw, lnb, n2, br, wa, wb, wo, wr, tm, y_feature_major):
    G, n, _ = x.shape
    assert n % tm == 0
    n_w = len(wa)
    consts = (lnw, lnb, n2, br) + tuple(wa) + tuple(wb) + tuple(wo) + tuple(wr)
    row = lambda width: pl.BlockSpec((None, tm, width), lambda b, i: (b, i, 0))
    y_spec = (pl.BlockSpec((2, None, RWKV_WIDTH // 2, tm), lambda b, i: (0, b, 0, i)) if y_feature_major
              else row(RWKV_WIDTH))
    in_specs = [row(D_MODEL), y_spec, row(RWKV_WIDTH), row(RWKV_WIDTH), row(RET_WIDTH),
                row(2 * D_MODEL)] + [_const_spec(c) for c in consts]
    return pl.pallas_call(
        functools.partial(_mixtail_kernel, n_w, y_feature_major),
        grid=(G, n // tm),
        in_specs=in_specs,
        out_specs=(row(D_MODEL), row(D_MODEL), row(LANES)),
        out_shape=(jax.ShapeDtypeStruct((G, n, D_MODEL), F32),
                   jax.ShapeDtypeStruct((G, n, D_MODEL), BF16),
                   jax.ShapeDtypeStruct((G, n, LANES), F32)),
        compiler_params=pltpu.CompilerParams(
            dimension_semantics=("arbitrary", "arbitrary"), vmem_limit_bytes=VMEM_LIMIT),
        name="mixtail",
    )(x, y, g, bonus, ob, gates, *consts)


def _moe_kernel(x1_ref, u2_ref, comb_ref, wg_ref, wu_ref, wd_ref, fg_ref, o_ref, acc):
    grp = pl.program_id(1)

    @pl.when(grp == 0)
    def _():
        acc[...] = jnp.zeros(acc.shape, F32)

    u2 = u2_ref[...]
    comb = comb_ref[...]
    lane = lax.broadcasted_iota(jnp.int32, comb.shape, 1)
    total = acc[...]
    for j in range(EXPERTS_PER_GROUP):
        c = jnp.sum(jnp.where(lane == grp * EXPERTS_PER_GROUP + j, comb, 0.0), axis=-1, keepdims=True)
        hg = jnp.dot(u2, wg_ref[j], preferred_element_type=F32)
        hu = jnp.dot(u2, wu_ref[j], preferred_element_type=F32)
        h = hg * _sigmoid(hg) * hu * c
        total = total + jnp.dot(h.astype(BF16), wd_ref[j], preferred_element_type=F32)
    acc[...] = total

    @pl.when(grp == pl.num_programs(1) - 1)
    def _():
        o_ref[...] = _rmsnorm(x1_ref[...] + acc[...], fg_ref[...])


def _moe(x1, u2, comb, wg, wu, wd, fg, tm):
    N = x1.shape[0]
    E = EXPERTS_PER_GROUP
    return pl.pallas_call(
        _moe_kernel,
        grid=(N // tm, N_GROUPS),
        in_specs=[pl.BlockSpec((tm, D_MODEL), lambda i, g: (i, 0)),
                  pl.BlockSpec((tm, D_MODEL), lambda i, g: (i, 0)),
                  pl.BlockSpec((tm, LANES), lambda i, g: (i, 0)),
                  pl.BlockSpec((E, D_MODEL, EXPERT_FF), lambda i, g: (g, 0, 0)),
                  pl.BlockSpec((E, D_MODEL, EXPERT_FF), lambda i, g: (g, 0, 0)),
                  pl.BlockSpec((E, EXPERT_FF, D_MODEL), lambda i, g: (g, 0, 0)),
                  pl.BlockSpec((1, D_MODEL), lambda i, g: (0, 0))],
        out_specs=pl.BlockSpec((tm, D_MODEL), lambda i, g: (i, 0)),
        out_shape=jax.ShapeDtypeStruct((N, D_MODEL), F32),
        scratch_shapes=[pltpu.VMEM((tm, D_MODEL), F32)],
        compiler_params=pltpu.CompilerParams(
            dimension_semantics=("arbitrary", "arbitrary"), vmem_limit_bytes=VMEM_LIMIT),
        name="moe",
    )(x1, u2, comb, wg, wu, wd, fg)


def _pieces(w, n):
    return tuple(_split_bf16(w.astype(F32), n))


def _rope_tables(pos):
    half = RET_DIM // 2
    inv = ROPE_BASE ** (-jnp.arange(half, dtype=F32) / half)
    ang = pos.astype(F32)[:, None] * inv[None, :]
    c = jnp.cos(ang)
    s = jnp.sin(ang)
    cc = jnp.repeat(c, 2, axis=-1)
    ss = jnp.stack([-s, s], axis=-1).reshape(pos.shape[0], RET_DIM)
    return cc, ss


def _retention_tables():
    C = RET_CHUNK
    log_gamma = jnp.log(1.0 - 2.0 ** (-5.0 - jnp.arange(RET_HEADS, dtype=F32)))
    idx = jnp.arange(C, dtype=F32)
    diff = idx[:, None] - idx[None, :]
    lg = log_gamma[:, None, None]
    dmask = jnp.where(diff[None] >= 0, jnp.exp(jnp.maximum(diff, 0.0)[None] * lg), 0.0)
    q_dec = jnp.exp((idx[None, :] + 1.0) * log_gamma[:, None])
    k_dec = jnp.exp((C - 1.0 - idx[None, :]) * log_gamma[:, None])
    bc = lambda a: jnp.broadcast_to(a[:, :, None], (RET_HEADS, C, C))
    rt_tab = jnp.stack([dmask, bc(q_dec), bc(k_dec)], axis=1)
    gc_tab = jnp.broadcast_to(jnp.exp(C * log_gamma)[:, None, None], (RET_HEADS, SUBLANES, RET_DIM))
    gam1 = jnp.broadcast_to(jnp.exp(log_gamma)[:, None], (RET_HEADS, RET_DIM))
    return rt_tab, gc_tab, gam1


def kernel(x_prompt, x_sample, state_wkv, state_shift, state_ret, meta_tokens, norm1_g, w_in, shift_mu, decay_w0, decay_up, aaa_a0, aaa_up, gate_up, k_k, k_a, r_k, lnx_w, lnx_b, w_branch_a, w_branch_b, w_out, norm2_g, router_group_w, router_group_b, router_expert_w, router_expert_b, expert_w_gate, expert_w_up, expert_w_down, final_norm_g):
    B, T, D = x_prompt.shape
    NS = x_sample.shape[0]
    TP = LEAD + T
    H, HD = RWKV_HEADS, RWKV_HEAD_DIM
    l = 0
    row = lambda a: a.reshape(1, -1)

    zpad = jnp.zeros((LORA_PAD - 64, RWKV_WIDTH), F32)
    dup_pad = jnp.concatenate([decay_up[l], zpad], axis=0)
    aup_pad = jnp.concatenate([zpad, aaa_up[l]], axis=0)
    wr_full = jnp.concatenate([router_expert_w[l], router_group_w[l]], axis=1)
    wr_full = jnp.pad(wr_full, ((0, 0), (0, LANES - wr_full.shape[1])))
    br_full = jnp.pad(jnp.concatenate([router_expert_b[l], router_group_b[l]]), (0, LANES - N_EXPERTS - N_GROUPS))
    wr = _pieces(wr_full, 2)
    wg = expert_w_gate[l].astype(BF16)
    wu = expert_w_up[l].astype(BF16)
    wd = expert_w_down[l].astype(BF16)
    rt_tab, gc_tab, gam1 = _retention_tables()

    split = {name: _pieces(w, 2) for name, w in (
        ("w_in", w_in[l]), ("dup", dup_pad), ("aup", aup_pad), ("gup", gate_up[l]),
        ("wa", w_branch_a[l]), ("wb", w_branch_b[l]), ("wo", w_out[l]))}

    def prep_params(n):
        return (row(shift_mu[l]), row(decay_w0[l]), split["dup"][:n], row(aaa_a0[l]), split["aup"][:n],
                split["gup"][:n], row(k_k[l]), row(k_a[l]), row(r_k[l]))

    def flat_params(p):
        out = []
        for a in p:
            out.extend(a if isinstance(a, tuple) else (a,))
        return tuple(out)

    def tail(n, x, y, g, bonus, ob, gates, tm_tail, tm_moe, y_feature_major):
        x1, u2, comb = _mixtail(x, y, g, bonus, ob, gates, row(lnx_w[l]), row(lnx_b[l]), row(norm2_g[l]),
                                row(br_full), split["wa"][:n], split["wb"][:n], split["wo"][:n], wr, tm_tail,
                                y_feature_major)
        flat = lambda a: a.reshape(-1, a.shape[-1])
        return _moe(flat(x1), flat(u2), flat(comb), wg, wu, wd, row(final_norm_g), tm_moe)

    lead = jnp.concatenate([jnp.zeros((LEAD - N_META, D), F32), meta_tokens.astype(F32)], axis=0)
    cc_p, ss_p = _rope_tables(jnp.arange(TP, dtype=jnp.int32) - (LEAD - N_META))
    pk, g, bonus, gates, ob, shift_p, ret_p = _inproj_prompt(
        x_prompt, lead, cc_p, ss_p, rt_tab, gc_tab, row(norm1_g[l]), split["w_in"][:1],
        flat_params(prep_params(1)), nb=4)

    pk_t = pk.reshape(6, 2 * B * H, K_LO, TP).transpose(3, 0, 2, 1)

    shift_p, xs = lax.optimization_barrier((shift_p, x_sample.reshape(NS, D)))
    cc_s, ss_s = _rope_tables(jnp.full((NS,), PAST_LEN, jnp.int32))
    pk_s, v_s, g_s, bonus_s, gates_s, retq, shift_s = _inproj_sample(
        xs, state_shift[l], cc_s, ss_s, row(norm1_g[l]), split["w_in"], flat_params(prep_params(2)))
    wkv_st, y_st = _rwkv_sample(state_wkv[l].transpose(1, 2, 3, 0),
                                pk_s.reshape(NS, 5, H, HD).transpose(1, 2, 3, 0),
                                v_s.reshape(NS, H, HD).transpose(1, 2, 0))
    wkv_s = wkv_st.transpose(3, 0, 1, 2)
    y_s = y_st.transpose(2, 0, 1).reshape(NS, RWKV_WIDTH)
    heads = lambda a: a.reshape(NS, RET_HEADS, RET_DIM)
    q_s, k_s, v_r, g_r = (retq[:, i * RET_WIDTH:(i + 1) * RET_WIDTH] for i in range(4))
    ret_s, ob_s = _ret_sample(state_ret[l], heads(q_s).transpose(0, 2, 1), heads(k_s).transpose(0, 2, 1),
                              heads(v_r), heads(g_r), gam1, bt=8)
    y_sample = tail(2, xs[None], y_s[None], g_s[None], bonus_s[None], ob_s.reshape(1, NS, RET_WIDTH), gates_s[None],
                    NS, NS, False)

    pk_t, y_sample, wkv_s, ret_s = lax.optimization_barrier((pk_t, y_sample, wkv_s, ret_s))
    y_t, s_fin = _rwkv_prompt(pk_t, tb=16)
    y_fm = y_t.transpose(2, 1, 0).reshape(2, B, RWKV_WIDTH // 2, T)
    wkv_p = s_fin.reshape(HD, K_LO, 2, B, H).transpose(3, 4, 2, 1, 0).reshape(B, H, HD, HD)
    y_prompt = tail(1, x_prompt, y_fm, g, bonus, ob, gates, 512, 1024, True).reshape(B, T, D)

    return (y_prompt, y_sample.reshape(NS, 1, D),
            wkv_p[None], shift_p.reshape(1, B, SHIFT_WIDTH), ret_p[None],
            wkv_s[None], shift_s[None], ret_s[None])
```

```python
import functools

import jax
import jax.numpy as jnp
from jax import lax
from jax.experimental import pallas as pl
from jax.experimental.pallas import tpu as pltpu

F32 = jnp.float32
BF16 = jnp.bfloat16

D_MODEL = 1024
N_META = 16
RWKV_WIDTH = 512
RWKV_HEADS = 8
RWKV_HEAD_DIM = 64
LORA_PAD = 128
GATE_LORA = 128
SHIFT_WIDTH = 3 * RWKV_WIDTH + 64 + 64 + GATE_LORA
LNX_EPS = 64e-5
RET_WIDTH = 512
RET_HEADS = 4
RET_DIM = 128
RET_CHUNK = 128
ROPE_BASE = 10000.0
RET_COL0 = SHIFT_WIDTH
GATE_COL0 = SHIFT_WIDTH + 4 * RET_WIDTH
IN_WIDTH = GATE_COL0 + 2 * D_MODEL
N_GROUPS = 4
EXPERTS_PER_GROUP = 4
N_EXPERTS = 16
EXPERT_FF = 256
RMS_EPS = 1e-6
PAST_LEN = 16384

LANES = 128
SUBLANES = 8
VMEM_LIMIT = 60 * 1024 * 1024

LEAD = 128
PK_W, PK_NKK, PK_BB, PK_K, PK_R, PK_V = range(6)
K_LO = 32
KL_UNROLL = 64


def _split_bf16(a, nterms):
    pieces = []
    rem = a
    for _ in range(nterms - 1):
        c = rem * 65537.0
        hi = c - (c - rem)
        pieces.append(hi.astype(BF16))
        rem = rem - hi
    pieces.append(rem.astype(BF16))
    return pieces


def _mm(a, w_refs, cols=None):
    def w(i):
        r = w_refs[i]
        return r[...] if cols is None else r[:, cols[0]:cols[1]]

    if a.dtype == BF16:
        acc = jnp.dot(a, w(0), preferred_element_type=F32)
        if len(w_refs) > 1:
            acc = acc + jnp.dot(a, w(1), preferred_element_type=F32)
        return acc
    if len(w_refs) == 1:
        return jnp.dot(a.astype(BF16), w(0), preferred_element_type=F32)
    a_hi, a_lo = _split_bf16(a, 2)
    acc = jnp.dot(a_hi, w(0), preferred_element_type=F32)
    acc = acc + jnp.dot(a_lo, w(0), preferred_element_type=F32)
    acc = acc + jnp.dot(a_hi, w(1), preferred_element_type=F32)
    return acc


def _rmsnorm(x, g):
    return x * lax.rsqrt(jnp.mean(x * x, axis=-1, keepdims=True) + RMS_EPS) * g


def _sigmoid(x):
    return 1.0 / (1.0 + jnp.exp(-x))


def _head64_sum(x):
    lane = lax.broadcasted_iota(jnp.int32, (x.shape[0], LANES), 1)
    lo = lane < RWKV_HEAD_DIM
    outs = []
    for j in range(x.shape[1] // LANES):
        xj = x[:, j * LANES:(j + 1) * LANES]
        s_lo = jnp.sum(jnp.where(lo, xj, 0.0), axis=-1, keepdims=True)
        s_hi = jnp.sum(jnp.where(lo, 0.0, xj), axis=-1, keepdims=True)
        outs.append(jnp.where(lo, s_lo, s_hi))
    return jnp.concatenate(outs, axis=-1)


def _rope(x, cc, ss):
    lane = lax.broadcasted_iota(jnp.int32, x.shape, 1)
    even = (lane & 1) == 0
    swapped = jnp.where(even, pltpu.roll(x, LANES - 1, 1), pltpu.roll(x, 1, 1))
    return x * cc + swapped * ss


def _rwkv_prep(s, prev, prm):
    (mu, w0, dup, a0, aup, gup, k_k, k_a, r_k) = prm
    xs = s + mu[...] * (prev - s)
    W = RWKV_WIDTH
    r = xs[:, 0:W]
    k = xs[:, W:2 * W]
    v = xs[:, 2 * W:3 * W]
    wa = xs[:, 3 * W:3 * W + LORA_PAD]
    gd = xs[:, 3 * W + LORA_PAD:]
    dec = _mm(jnp.tanh(wa), dup)
    z = -(w0[...] + dec)
    softplus = jnp.maximum(z, 0.0) + jnp.log1p(jnp.exp(-jnp.abs(z)))
    decay = jnp.exp(-jnp.exp(-softplus - 0.5))
    a = _sigmoid(a0[...] + _mm(wa, aup))
    g = _mm(_sigmoid(gd), gup)
    kk = k * k_k[...]
    kk = kk / jnp.maximum(jnp.sqrt(_head64_sum(kk * kk)), 1e-12)
    k_mod = k * (1.0 + (a - 1.0) * k_a[...])
    bonus = _head64_sum(r * k_mod * r_k[...]) * v
    return r, decay, k_mod, v, kk, a, g, bonus


def _take_prep_params(refs, n_w):
    it = iter(refs)
    take = lambda n: tuple(next(it) for _ in range(n))
    (mu, w0), dup, (a0,), aup, gup, (k_k, k_a, r_k) = take(2), take(n_w), take(1), take(n_w), take(n_w), take(3)
    return (mu, w0, dup, a0, aup, gup, k_k, k_a, r_k), tuple(it)


def _store_rwkv_operands(ops, pk_ref, v_ref, g_ref, bonus_ref):
    r, decay, k_mod, v, kk, a, g, bonus = ops
    W = RWKV_WIDTH
    pk_ref[:, PK_W * W:(PK_W + 1) * W] = decay
    pk_ref[:, PK_NKK * W:(PK_NKK + 1) * W] = -kk
    pk_ref[:, PK_BB * W:(PK_BB + 1) * W] = kk * a
    pk_ref[:, PK_K * W:(PK_K + 1) * W] = k_mod
    pk_ref[:, PK_R * W:(PK_R + 1) * W] = r
    v_ref[...] = v
    g_ref[...] = g
    bonus_ref[...] = bonus


def _inproj_prompt_kernel(n_w, x_ref, cc_ref, ss_ref, lead_ref, rt_ref, gc_ref, n1_ref, *rest):
    w_in = rest[:n_w]
    prm, rest = _take_prep_params(rest[n_w:], n_w)
    (pk_ref, g_ref, bonus_ref, gates_ref, ob_ref, shift_ref, ret_ref, prev_buf, ret_state) = rest
    t = pl.program_id(1)
    nb, tm, _ = x_ref.shape
    rows = [slice(b * tm, (b + 1) * tm) for b in range(nb)]

    @pl.when(t == 0)
    def _():
        prev_buf[:, 0:SUBLANES, :] = jnp.zeros((nb, SUBLANES, SHIFT_WIDTH), F32)
        ret_state[...] = jnp.zeros(ret_state.shape, F32)

    x = jnp.where(t == 0, lead_ref[...][None], x_ref[...])
    u = _rmsnorm(x.reshape(nb * tm, D_MODEL), n1_ref[...])
    u_b = u.astype(BF16) if n_w == 1 else u

    s = _mm(u_b, w_in, (0, SHIFT_WIDTH))
    prevs = []
    for b in range(nb):
        s_b = s[rows[b]]
        prev_buf[b, SUBLANES:SUBLANES + tm, :] = s_b
        prevs.append(prev_buf[b, SUBLANES - 1:SUBLANES - 1 + tm, :])
        prev_buf[b, SUBLANES - 1:SUBLANES, :] = s_b[tm - 1:tm, :]
        shift_ref[b] = s_b[tm - 1:tm, :]
    prev = jnp.concatenate(prevs, axis=0)

    r, decay, k_mod, v, kk, a, g, bonus = _rwkv_prep(s, prev, prm)
    W = RWKV_WIDTH
    for b in range(nb):
        rb = rows[b]
        for q, val in ((PK_W, decay), (PK_NKK, -kk), (PK_BB, kk * a), (PK_K, k_mod), (PK_R, r), (PK_V, v)):
            fm = val[rb].T
            for hi in range(2):
                pk_ref[q, hi, b] = jnp.concatenate(
                    [fm[h * RWKV_HEAD_DIM + hi * K_LO:h * RWKV_HEAD_DIM + (hi + 1) * K_LO] for h in range(RWKV_HEADS)],
                    axis=0)
        g_ref[b] = g[rb]
        bonus_ref[b] = bonus[rb]

    gates = _mm(u_b, w_in, (GATE_COL0, IN_WIDTH))
    for b in range(nb):
        gates_ref[b] = gates[rows[b]]

    ret = _mm(u_b, w_in, (RET_COL0, GATE_COL0))
    cc = cc_ref[...]
    ss = ss_ref[...]
    for b in range(nb):
        rb = rows[b]
        for h in range(RET_HEADS):
            c0 = h * RET_DIM
            qc = _rope(ret[rb, c0:c0 + RET_DIM], cc, ss).astype(BF16)
            kc = _rope(ret[rb, RET_WIDTH + c0:RET_WIDTH + c0 + RET_DIM], cc, ss) * (RET_DIM ** -0.5)
            vc = ret[rb, 2 * RET_WIDTH + c0:2 * RET_WIDTH + c0 + RET_DIM].astype(BF16)
            gr = ret[rb, 3 * RET_WIDTH + c0:3 * RET_WIDTH + c0 + RET_DIM]
            state = ret_state[b, h]
            scores = lax.dot_general(qc, kc.astype(BF16), (((1,), (1,)), ((), ())),
                                     preferred_element_type=F32) * rt_ref[h, 0]
            intra = jnp.dot(scores.astype(BF16), vc, preferred_element_type=F32)
            cross = jnp.dot(qc, state.astype(BF16), preferred_element_type=F32) * rt_ref[h, 1]
            kt = (kc * rt_ref[h, 2]).T.astype(BF16)
            ret_state[b, h] = gc_ref[h, 0:1, :] * state + jnp.dot(kt, vc, preferred_element_type=F32)
            o = intra + cross
            o = o * lax.rsqrt(jnp.mean(o * o, axis=-1, keepdims=True) + RMS_EPS)
            ob_ref[b, :, c0:c0 + RET_DIM] = (gr * _sigmoid(gr) * o).astype(ob_ref.dtype)

    @pl.when(t == pl.num_programs(1) - 1)
    def _():
        ret_ref[...] = ret_state[...]


def _inproj_sample_kernel(n_w, x_ref, prev_ref, cc_ref, ss_ref, n1_ref, *rest):
    w_in = rest[:n_w]
    prm, rest = _take_prep_params(rest[n_w:], n_w)
    (pk_ref, v_ref, g_ref, bonus_ref, gates_ref, retq_ref, shift_ref) = rest
    u = _rmsnorm(x_ref[...], n1_ref[...])
    s = _mm(u, w_in, (0, SHIFT_WIDTH))
    shift_ref[...] = s
    _store_rwkv_operands(_rwkv_prep(s, prev_ref[...], prm), pk_ref, v_ref, g_ref, bonus_ref)
    gates_ref[...] = _mm(u, w_in, (GATE_COL0, IN_WIDTH))
    ret = _mm(u, w_in, (RET_COL0, GATE_COL0))
    cc = cc_ref[...]
    ss = ss_ref[...]
    for h in range(RET_HEADS):
        c0 = h * RET_DIM
        retq_ref[:, c0:c0 + RET_DIM] = _rope(ret[:, c0:c0 + RET_DIM], cc, ss)
        retq_ref[:, RET_WIDTH + c0:RET_WIDTH + c0 + RET_DIM] = (
            _rope(ret[:, RET_WIDTH + c0:RET_WIDTH + c0 + RET_DIM], cc, ss) * (RET_DIM ** -0.5))
    retq_ref[:, 2 * RET_WIDTH:] = ret[:, 2 * RET_WIDTH:]


def _const_spec(arr):
    nd = arr.ndim
    return pl.BlockSpec(arr.shape, lambda *_: (0,) * nd, pipeline_mode=pl.Buffered(1))


def _inproj_prompt(x, lead, cc, ss, rt_tab, gc_tab, n1, w_in, prm, nb):
    B, T, _ = x.shape
    tm = RET_CHUNK
    assert lead.shape[0] == tm and T % tm == 0 and B % nb == 0
    TP = tm + T
    split = (TP // tm) // 2
    seg_tiles = TP // tm - split
    n_w = len(w_in)
    consts = (lead, rt_tab, gc_tab, n1) + tuple(w_in) + tuple(prm)
    prow = lambda width: pl.BlockSpec((nb, tm, width), lambda b, t: (b, jnp.maximum(t - 1, 0), 0))
    in_specs = [prow(D_MODEL),
                pl.BlockSpec((tm, LANES), lambda b, t: (t, 0)),
                pl.BlockSpec((tm, LANES), lambda b, t: (t, 0))] + [_const_spec(c) for c in consts]
    out_shape = (
        jax.ShapeDtypeStruct((2, 6, 2, B, RWKV_WIDTH // 2, seg_tiles * tm), F32),
        jax.ShapeDtypeStruct((B, T, RWKV_WIDTH), F32),
        jax.ShapeDtypeStruct((B, T, RWKV_WIDTH), F32),
        jax.ShapeDtypeStruct((B, T, 2 * D_MODEL), F32),
        jax.ShapeDtypeStruct((B, T, RET_WIDTH), BF16),
        jax.ShapeDtypeStruct((B, 1, SHIFT_WIDTH), F32),
        jax.ShapeDtypeStruct((B, RET_HEADS, RET_DIM, RET_DIM), F32),
    )
    out_specs = (
        pl.BlockSpec((None, 6, 2, nb, RWKV_WIDTH // 2, tm),
                     lambda b, t: (jnp.where(t >= split, 1, 0), 0, 0, b, 0, jnp.where(t >= split, t - split, t))),
        prow(RWKV_WIDTH), prow(RWKV_WIDTH), prow(2 * D_MODEL), prow(RET_WIDTH),
        pl.BlockSpec((nb, 1, SHIFT_WIDTH), lambda b, t: (b, 0, 0)),
        pl.BlockSpec((nb, RET_HEADS, RET_DIM, RET_DIM), lambda b, t: (b, 0, 0, 0)),
    )
    return pl.pallas_call(
        functools.partial(_inproj_prompt_kernel, n_w),
        grid=(B // nb, TP // tm),
        in_specs=in_specs,
        out_specs=out_specs,
        out_shape=out_shape,
        scratch_shapes=[pltpu.VMEM((nb, tm + SUBLANES, SHIFT_WIDTH), F32),
                        pltpu.VMEM((nb, RET_HEADS, RET_DIM, RET_DIM), F32)],
        compiler_params=pltpu.CompilerParams(
            dimension_semantics=("arbitrary", "arbitrary"), vmem_limit_bytes=VMEM_LIMIT),
        name="inproj_prompt",
    )(x, cc, ss, *consts)


def _inproj_sample(xs, prev, cc, ss, n1, w_in, prm):
    N = xs.shape[0]
    n_w = len(w_in)
    args = (xs, prev, cc, ss, n1) + tuple(w_in) + tuple(prm)
    out_shape = (
        jax.ShapeDtypeStruct((N, 5 * RWKV_WIDTH), F32),
        jax.ShapeDtypeStruct((N, RWKV_WIDTH), F32),
        jax.ShapeDtypeStruct((N, RWKV_WIDTH), F32),
        jax.ShapeDtypeStruct((N, RWKV_WIDTH), F32),
        jax.ShapeDtypeStruct((N, 2 * D_MODEL), F32),
        jax.ShapeDtypeStruct((N, 4 * RET_WIDTH), F32),
        jax.ShapeDtypeStruct((N, SHIFT_WIDTH), F32),
    )
    return pl.pallas_call(
        functools.partial(_inproj_sample_kernel, n_w),
        out_shape=out_shape,
        compiler_params=pltpu.CompilerParams(vmem_limit_bytes=VMEM_LIMIT),
        name="inproj_sample",
    )(*args)


def _rwkv_prompt_kernel(aliased_y, pk_ref, nxt_ref, tail_ref, s0_ref, sa0_ref, *rest):
    y_ref, sout_ref, saout_ref, state, sa_buf, kd = rest[1:] if aliased_y else rest
    i = pl.program_id(0)
    last = pl.num_programs(0) - 1
    tb = pk_ref.shape[0]
    HD = RWKV_HEAD_DIM
    VL = K_LO

    @pl.when(i == 0)
    def _():
        state[...] = s0_ref[...]
        sa_buf[...] = sa0_ref[...]

    first_half = lax.broadcasted_iota(jnp.int32, (K_LO, LANES), 1) < LANES // 2

    def spread(x, t, q):
        swapped = pltpu.roll(x, LANES // 2, 1)
        kd[t, q, 0:K_LO, :] = jnp.where(first_half, x, swapped)
        kd[t, q, K_LO:HD, :] = jnp.where(first_half, swapped, x)

    for t in range(tb):
        for q in (PK_W, PK_NKK, PK_BB, PK_K, PK_R):
            spread(pk_ref[t, q], t, q)
    spread(jnp.where(i == last, tail_ref[0, 0], nxt_ref[0, 0]), tb, PK_NKK)

    def step(t, sa):
        vv = pk_ref[t, PK_V]

        def key_rows(c, accs):
            yacc, sacc = accs
            for j in range(KL_UNROLL):
                k = c * KL_UNROLL + j
                row = lambda tt, q: kd[tt, q, pl.ds(k, 1), :]
                rows = pl.ds(pl.multiple_of(k * VL, VL), VL)
                new = state[rows, :] * row(t, PK_W) + sa * row(t, PK_BB) + vv * row(t, PK_K)
                state[rows, :] = new
                yacc = yacc + new * row(t, PK_R)
                sacc = sacc + new * row(t + 1, PK_NKK)
            return yacc, sacc

        zero = jnp.zeros((VL, LANES), F32)
        yacc, sacc = lax.fori_loop(0, HD // KL_UNROLL, key_rows, (zero, zero))
        y_ref[t] = yacc
        return sacc

    sa_buf[...] = lax.fori_loop(0, tb, step, sa_buf[...])

    @pl.when(i == last)
    def _():
        sout_ref[...] = state[...]
        saout_ref[...] = sa_buf[...]


def _rwkv_prompt(pk_t, tail, state0, sa0, y_prev, *, tb, first_row, last_row, y_rows, y_row0):
    rows = pk_t.shape[0]
    HD = RWKV_HEAD_DIM
    assert first_row % tb == 0 and last_row % tb == 0 and y_row0 % tb == 0
    skip = first_row // tb
    y_blk0 = y_row0 // tb
    aliased = y_prev is not None
    in_specs = [pl.BlockSpec((tb, 6, K_LO, LANES), lambda i: (i + skip, 0, 0, 0)),
                pl.BlockSpec((1, 1, K_LO, LANES),
                             lambda i: (jnp.minimum((i + skip + 1) * tb, last_row - 1), PK_NKK, 0, 0)),
                pl.BlockSpec((1, 1, K_LO, LANES), lambda i: (0, 0, 0, 0)),
                pl.BlockSpec((HD * K_LO, LANES), lambda i: (0, 0)),
                pl.BlockSpec((K_LO, LANES), lambda i: (0, 0))]
    args = [pk_t, pk_t, tail, state0, sa0]
    if aliased:
        in_specs.append(pl.BlockSpec(memory_space=pl.ANY))
        args.append(y_prev)
    return pl.pallas_call(
        functools.partial(_rwkv_prompt_kernel, aliased),
        grid=((last_row - first_row) // tb,),
        in_specs=in_specs,
        out_specs=(pl.BlockSpec((tb, K_LO, LANES), lambda i: (jnp.maximum(i + y_blk0, max(y_blk0, 0)), 0, 0)),
                   pl.BlockSpec((HD * K_LO, LANES), lambda i: (0, 0)),
                   pl.BlockSpec((K_LO, LANES), lambda i: (0, 0))),
        out_shape=(jax.ShapeDtypeStruct((y_rows, K_LO, LANES), F32),
                   jax.ShapeDtypeStruct((HD * K_LO, LANES), F32),
                   jax.ShapeDtypeStruct((K_LO, LANES), F32)),
        scratch_shapes=[pltpu.VMEM((HD * K_LO, LANES), F32), pltpu.VMEM((K_LO, LANES), F32),
                        pltpu.VMEM((tb + 1, 5, HD, LANES), F32)],
        input_output_aliases={5: 0} if aliased else {},
        compiler_params=pltpu.CompilerParams(dimension_semantics=("arbitrary",)),
        name="rwkv_prompt",
    )(*args)


def _lane_col(tile, h):
    lane = lax.broadcasted_iota(jnp.int32, tile.shape, 1)
    return jnp.sum(jnp.where(lane == h, tile, 0.0), axis=-1, keepdims=True)


def _rwkv_sample_kernel(s_ref, pk_ref, v_ref, sout_ref, y_ref):
    def value_row(v, carry):
        S = s_ref[v]
        sa = jnp.sum(S * pk_ref[PK_NKK], axis=0, keepdims=True)
        new = S * pk_ref[PK_W] + sa * pk_ref[PK_BB] + v_ref[pl.ds(v, 1), :] * pk_ref[PK_K]
        sout_ref[v] = new
        y_ref[pl.ds(v, 1), :] = jnp.sum(new * pk_ref[PK_R], axis=0, keepdims=True)
        return carry

    lax.fori_loop(0, s_ref.shape[0], value_row, 0, unroll=2)


def _rwkv_sample(state_t, pk_t, v_t):
    H, HD, _, N = state_t.shape
    return pl.pallas_call(
        _rwkv_sample_kernel,
        grid=(H,),
        in_specs=[pl.BlockSpec((None, HD, HD, N), lambda h: (h, 0, 0, 0)),
                  pl.BlockSpec((5, None, HD, N), lambda h: (0, h, 0, 0)),
                  pl.BlockSpec((None, HD, N), lambda h: (h, 0, 0))],
        out_specs=(pl.BlockSpec((None, HD, HD, N), lambda h: (h, 0, 0, 0)),
                   pl.BlockSpec((None, HD, N), lambda h: (h, 0, 0))),
        out_shape=(jax.ShapeDtypeStruct(state_t.shape, F32),
                   jax.ShapeDtypeStruct((H, HD, N), F32)),
        compiler_params=pltpu.CompilerParams(dimension_semantics=("arbitrary",)),
        name="rwkv_sample",
    )(state_t, pk_t, v_t)


def _ret_sample_kernel(s_ref, qt_ref, kt_ref, v_ref, g_ref, gam_ref, sout_ref, ob_ref):
    gam = gam_ref[...][:, None, :]
    for b in range(s_ref.shape[0]):
        q_tile = qt_ref[b]
        k_tile = kt_ref[b]
        qc = jnp.stack([_lane_col(q_tile, h) for h in range(RET_HEADS)], axis=0)
        kc = jnp.stack([_lane_col(k_tile, h) for h in range(RET_HEADS)], axis=0)
        vrow = v_ref[b][:, None, :]
        S = s_ref[b]
        qk = jnp.sum(qc * kc, axis=1, keepdims=True)
        cross = jnp.sum(qc * S, axis=1, keepdims=True)
        o = qk * vrow + cross * gam
        sout_ref[b] = gam * S + kc * vrow
        o = o * lax.rsqrt(jnp.mean(o * o, axis=-1, keepdims=True) + RMS_EPS)
        gr = g_ref[b][:, None, :]
        ob_ref[b] = (gr * _sigmoid(gr) * o)[:, 0, :]


def _ret_sample(state, q_t, k_t, v_h, g_h, gam, bt):
    N = state.shape[0]
    H, DK = RET_HEADS, RET_DIM
    col = pl.BlockSpec((bt, DK, H), lambda i: (i, 0, 0))
    rowspec = pl.BlockSpec((bt, H, DK), lambda i: (i, 0, 0))
    st = pl.BlockSpec((bt, H, DK, DK), lambda i: (i, 0, 0, 0))
    return pl.pallas_call(
        _ret_sample_kernel,
        grid=(N // bt,),
        in_specs=[st, col, col, rowspec, rowspec, pl.BlockSpec((H, DK), lambda i: (0, 0))],
        out_specs=(st, rowspec),
        out_shape=(jax.ShapeDtypeStruct(state.shape, F32), jax.ShapeDtypeStruct((N, H, DK), F32)),
        compiler_params=pltpu.CompilerParams(dimension_semantics=("arbitrary",)),
        name="ret_sample",
    )(state, q_t, k_t, v_h, g_h, gam)


def _mixtail_kernel(n_w, y_feature_major, x_ref, y_ref, g_ref, bonus_ref, ob_ref, gates_ref,
                    lnw_ref, lnb_ref, n2_ref, br_ref, *rest):
    wa = rest[0:n_w]
    wb = rest[n_w:2 * n_w]
    wo = rest[2 * n_w:3 * n_w]
    wr = rest[3 * n_w:3 * n_w + 2]
    x1_ref, u2_ref, comb_ref = rest[3 * n_w + 2:]

    inv_n = 1.0 / RWKV_HEAD_DIM
    if y_feature_major:
        normed = []
        for h in range(RWKV_HEADS):
            yh = jnp.concatenate([y_ref[hi, h * K_LO:(h + 1) * K_LO, :] for hi in range(2)], axis=0)
            d = yh - jnp.sum(yh, axis=0, keepdims=True) * inv_n
            var = jnp.sum(d * d, axis=0, keepdims=True) * inv_n
            normed.append(d * lax.rsqrt(var + LNX_EPS))
        yhat = jnp.concatenate(normed, axis=0).T
    else:
        y = y_ref[...]
        d = y - _head64_sum(y) * inv_n
        var = _head64_sum(d * d) * inv_n
        yhat = d * lax.rsqrt(var + LNX_EPS)
    yn = yhat * lnw_ref[...] + lnb_ref[...]
    ya = (yn + bonus_ref[...]) * g_ref[...]
    branch_a = _mm(ya, wa)
    branch_b = _mm(ob_ref[...], wb)
    gates = gates_ref[...]
    merged = _sigmoid(gates[:, :D_MODEL]) * branch_a + _sigmoid(gates[:, D_MODEL:]) * branch_b
    x1 = x_ref[...] + _mm(merged, wo)
    x1_ref[...] = x1
    u2 = _rmsnorm(x1, n2_ref[...])
    u2_ref[...] = u2.astype(u2_ref.dtype)

    logits = (_mm(u2, wr) + br_ref[...]).T
    tm = logits.shape[1]
    neg = jnp.float32(-jnp.inf)
    big = jnp.float32(1e9)
    first = lambda hit, idx: jnp.min(jnp.where(hit, idx, big), axis=0, keepdims=True)
    lg = logits[N_EXPERTS:N_EXPERTS + N_GROUPS]
    gidx = lax.broadcasted_iota(jnp.int32, lg.shape, 0).astype(F32)
    lg_max = jnp.max(lg, axis=0, keepdims=True)
    p_grp = 1.0 / jnp.sum(jnp.exp(lg - lg_max), axis=0, keepdims=True)
    grp = first(lg == lg_max, gidx)
    eidx = lax.broadcasted_iota(jnp.int32, (N_EXPERTS, tm), 0).astype(F32)
    in_grp = (eidx >= grp * EXPERTS_PER_GROUP) & (eidx < (grp + 1.0) * EXPERTS_PER_GROUP)
    le = jnp.where(in_grp, logits[0:N_EXPERTS], neg)
    v1 = jnp.max(le, axis=0, keepdims=True)
    i1 = first(le == v1, eidx)
    le2 = jnp.where(eidx == i1, neg, le)
    v2 = jnp.max(le2, axis=0, keepdims=True)
    i2 = first(le2 == v2, eidx)
    e2 = jnp.exp(v2 - v1)
    w1 = 1.0 / (1.0 + e2)
    w2 = e2 / (1.0 + e2)
    comb = p_grp * (jnp.where(eidx == i1, w1, 0.0) + jnp.where(eidx == i2, w2, 0.0))
    comb_ref[...] = jnp.concatenate([comb, jnp.zeros((LANES - N_EXPERTS, tm), F32)], axis=0).T


def _mixtail(x, y, g, bonus, ob, gates, lnw, lnb, n2, br, wa, wb, wo, wr, tm, y_feature_major):
    G, n, _ = x.shape
    assert n % tm == 0
    n_w = len(wa)
    consts = (lnw, lnb, n2, br) + tuple(wa) + tuple(wb) + tuple(wo) + tuple(wr)
    row = lambda width: pl.BlockSpec((None, tm, width), lambda b, i: (b, i, 0))
    y_spec = (pl.BlockSpec((2, None, RWKV_WIDTH // 2, tm), lambda b, i: (0, b, 0, i)) if y_feature_major
              else row(RWKV_WIDTH))
    in_specs = [row(D_MODEL), y_spec, row(RWKV_WIDTH), row(RWKV_WIDTH), row(RET_WIDTH),
                row(2 * D_MODEL)] + [_const_spec(c) for c in consts]
    return pl.pallas_call(
        functools.partial(_mixtail_kernel, n_w, y_feature_major),
        grid=(G, n // tm),
        in_specs=in_specs,
        out_specs=(row(D_MODEL), row(D_MODEL), row(LANES)),
        out_shape=(jax.ShapeDtypeStruct((G, n, D_MODEL), F32),
                   jax.ShapeDtypeStruct((G, n, D_MODEL), BF16),
                   jax.ShapeDtypeStruct((G, n, LANES), F32)),
        compiler_params=pltpu.CompilerParams(
            dimension_semantics=("arbitrary", "arbitrary"), vmem_limit_bytes=VMEM_LIMIT),
        name="mixtail",
    )(x, y, g, bonus, ob, gates, *consts)


def _moe_kernel(x1_ref, u2_ref, comb_ref, wg_ref, wu_ref, wd_ref, fg_ref, o_ref, acc):
    grp = pl.program_id(1)

    @pl.when(grp == 0)
    def _():
        acc[...] = jnp.zeros(acc.shape, F32)

    u2 = u2_ref[...]
    comb = comb_ref[...]
    lane = lax.broadcasted_iota(jnp.int32, comb.shape, 1)
    total = acc[...]
    for j in range(EXPERTS_PER_GROUP):
        c = jnp.sum(jnp.where(lane == grp * EXPERTS_PER_GROUP + j, comb, 0.0), axis=-1, keepdims=True)
        hg = jnp.dot(u2, wg_ref[j], preferred_element_type=F32)
        hu = jnp.dot(u2, wu_ref[j], preferred_element_type=F32)
        h = hg * _sigmoid(hg) * hu * c
        total = total + jnp.dot(h.astype(BF16), wd_ref[j], preferred_element_type=F32)
    acc[...] = total

    @pl.when(grp == pl.num_programs(1) - 1)
    def _():
        o_ref[...] = _rmsnorm(x1_ref[...] + acc[...], fg_ref[...])


def _moe(x1, u2, comb, wg, wu, wd, fg, tm):
    N = x1.shape[0]
    E = EXPERTS_PER_GROUP
    return pl.pallas_call(
        _moe_kernel,
        grid=(N // tm, N_GROUPS),
        in_specs=[pl.BlockSpec((tm, D_MODEL), lambda i, g: (i, 0)),
                  pl.BlockSpec((tm, D_MODEL), lambda i, g: (i, 0)),
                  pl.BlockSpec((tm, LANES), lambda i, g: (i, 0)),
                  pl.BlockSpec((E, D_MODEL, EXPERT_FF), lambda i, g: (g, 0, 0)),
                  pl.BlockSpec((E, D_MODEL, EXPERT_FF), lambda i, g: (g, 0, 0)),
                  pl.BlockSpec((E, EXPERT_FF, D_MODEL), lambda i, g: (g, 0, 0)),
                  pl.BlockSpec((1, D_MODEL), lambda i, g: (0, 0))],
        out_specs=pl.BlockSpec((tm, D_MODEL), lambda i, g: (i, 0)),
        out_shape=jax.ShapeDtypeStruct((N, D_MODEL), F32),
        scratch_shapes=[pltpu.VMEM((tm, D_MODEL), F32)],
        compiler_params=pltpu.CompilerParams(
            dimension_semantics=("arbitrary", "arbitrary"), vmem_limit_bytes=VMEM_LIMIT),
        name="moe",
    )(x1, u2, comb, wg, wu, wd, fg)


def _pieces(w, n):
    return tuple(_split_bf16(w.astype(F32), n))


def _rope_tables(pos):
    half = RET_DIM // 2
    inv = ROPE_BASE ** (-jnp.arange(half, dtype=F32) / half)
    ang = pos.astype(F32)[:, None] * inv[None, :]
    c = jnp.cos(ang)
    s = jnp.sin(ang)
    cc = jnp.repeat(c, 2, axis=-1)
    ss = jnp.stack([-s, s], axis=-1).reshape(pos.shape[0], RET_DIM)
    return cc, ss


def _retention_tables():
    C = RET_CHUNK
    log_gamma = jnp.log(1.0 - 2.0 ** (-5.0 - jnp.arange(RET_HEADS, dtype=F32)))
    idx = jnp.arange(C, dtype=F32)
    diff = idx[:, None] - idx[None, :]
    lg = log_gamma[:, None, None]
    dmask = jnp.where(diff[None] >= 0, jnp.exp(jnp.maximum(diff, 0.0)[None] * lg), 0.0)
    q_dec = jnp.exp((idx[None, :] + 1.0) * log_gamma[:, None])
    k_dec = jnp.exp((C - 1.0 - idx[None, :]) * log_gamma[:, None])
    bc = lambda a: jnp.broadcast_to(a[:, :, None], (RET_HEADS, C, C))
    rt_tab = jnp.stack([dmask, bc(q_dec), bc(k_dec)], axis=1)
    gc_tab = jnp.broadcast_to(jnp.exp(C * log_gamma)[:, None, None], (RET_HEADS, SUBLANES, RET_DIM))
    gam1 = jnp.broadcast_to(jnp.exp(log_gamma)[:, None], (RET_HEADS, RET_DIM))
    return rt_tab, gc_tab, gam1


def kernel(x_prompt, x_sample, state_wkv, state_shift, state_ret, meta_tokens, norm1_g, w_in, shift_mu, decay_w0, decay_up, aaa_a0, aaa_up, gate_up, k_k, k_a, r_k, lnx_w, lnx_b, w_branch_a, w_branch_b, w_out, norm2_g, router_group_w, router_group_b, router_expert_w, router_expert_b, expert_w_gate, expert_w_up, expert_w_down, final_norm_g):
    B, T, D = x_prompt.shape
    NS = x_sample.shape[0]
    TP = LEAD + T
    H, HD = RWKV_HEADS, RWKV_HEAD_DIM
    l = 0
    row = lambda a: a.reshape(1, -1)

    zpad = jnp.zeros((LORA_PAD - 64, RWKV_WIDTH), F32)
    dup_pad = jnp.concatenate([decay_up[l], zpad], axis=0)
    aup_pad = jnp.concatenate([zpad, aaa_up[l]], axis=0)
    wr_full = jnp.concatenate([router_expert_w[l], router_group_w[l]], axis=1)
    wr_full = jnp.pad(wr_full, ((0, 0), (0, LANES - wr_full.shape[1])))
    br_full = jnp.pad(jnp.concatenate([router_expert_b[l], router_group_b[l]]), (0, LANES - N_EXPERTS - N_GROUPS))
    wr = _pieces(wr_full, 2)
    wg = expert_w_gate[l].astype(BF16)
    wu = expert_w_up[l].astype(BF16)
    wd = expert_w_down[l].astype(BF16)
    rt_tab, gc_tab, gam1 = _retention_tables()

    split = {name: _pieces(w, 2) for name, w in (
        ("w_in", w_in[l]), ("dup", dup_pad), ("aup", aup_pad), ("gup", gate_up[l]),
        ("wa", w_branch_a[l]), ("wb", w_branch_b[l]), ("wo", w_out[l]))}

    def prep_params(n):
        return (row(shift_mu[l]), row(decay_w0[l]), split["dup"][:n], row(aaa_a0[l]), split["aup"][:n],
                split["gup"][:n], row(k_k[l]), row(k_a[l]), row(r_k[l]))

    def flat_params(p):
        out = []
        for a in p:
            out.extend(a if isinstance(a, tuple) else (a,))
        return tuple(out)

    def tail(n, x, y, g, bonus, ob, gates, tm_tail, tm_moe, y_feature_major):
        x1, u2, comb = _mixtail(x, y, g, bonus, ob, gates, row(lnx_w[l]), row(lnx_b[l]), row(norm2_g[l]),
                                row(br_full), split["wa"][:n], split["wb"][:n], split["wo"][:n], wr, tm_tail,
                                y_feature_major)
        flat = lambda a: a.reshape(-1, a.shape[-1])
        return _moe(flat(x1), flat(u2), flat(comb), wg, wu, wd, row(final_norm_g), tm_moe)

    lead = jnp.concatenate([jnp.zeros((LEAD - N_META, D), F32), meta_tokens.astype(F32)], axis=0)
    cc_p, ss_p = _rope_tables(jnp.arange(TP, dtype=jnp.int32) - (LEAD - N_META))
    pk, g, bonus, gates, ob, shift_p, ret_p = _inproj_prompt(
        x_prompt, lead, cc_p, ss_p, rt_tab, gc_tab, row(norm1_g[l]), split["w_in"][:1],
        flat_params(prep_params(1)), nb=4)

    seg_rows = pk.shape[-1]
    split_row = TP - seg_rows
    pk_t = [pk[s].reshape(6, 2 * B * H, K_LO, seg_rows).transpose(3, 0, 2, 1) for s in range(2)]
    tail_a = pk[1, PK_NKK, :, :, :, 0].reshape(2 * B * H, K_LO).T.reshape(1, 1, K_LO, LANES)

    shift_p, xs = lax.optimization_barrier((shift_p, x_sample.reshape(NS, D)))
    cc_s, ss_s = _rope_tables(jnp.full((NS,), PAST_LEN, jnp.int32))
    pk_s, v_s, g_s, bonus_s, gates_s, retq, shift_s = _inproj_sample(
        xs, state_shift[l], cc_s, ss_s, row(norm1_g[l]), split["w_in"], flat_params(prep_params(2)))
    wkv_st, y_st = _rwkv_sample(state_wkv[l].transpose(1, 2, 3, 0),
                                pk_s.reshape(NS, 5, H, HD).transpose(1, 2, 3, 0),
                                v_s.reshape(NS, H, HD).transpose(1, 2, 0))
    wkv_s = wkv_st.transpose(3, 0, 1, 2)
    y_s = y_st.transpose(2, 0, 1).reshape(NS, RWKV_WIDTH)
    heads = lambda a: a.reshape(NS, RET_HEADS, RET_DIM)
    q_s, k_s, v_r, g_r = (retq[:, i * RET_WIDTH:(i + 1) * RET_WIDTH] for i in range(4))
    ret_s, ob_s = _ret_sample(state_ret[l], heads(q_s).transpose(0, 2, 1), heads(k_s).transpose(0, 2, 1),
                              heads(v_r), heads(g_r), gam1, bt=8)
    y_sample = tail(2, xs[None], y_s[None], g_s[None], bonus_s[None], ob_s.reshape(1, NS, RET_WIDTH), gates_s[None],
                    NS, NS, False)

    pk_t[0], y_sample, wkv_s, ret_s = lax.optimization_barrier((pk_t[0], y_sample, wkv_s, ret_s))
    zeros = lambda r: jnp.zeros((r, LANES), F32)
    y_t, s_mid, sa_mid = _rwkv_prompt(
        pk_t[0], tail_a, zeros(HD * K_LO), zeros(K_LO), None,
        tb=16, first_row=LEAD - N_META, last_row=split_row, y_rows=T, y_row0=-N_META)
    y_t, s_fin, _ = _rwkv_prompt(
        pk_t[1], tail_a, s_mid, sa_mid, y_t,
        tb=16, first_row=0, last_row=seg_rows, y_rows=T, y_row0=split_row - LEAD)
    y_fm = y_t.transpose(2, 1, 0).reshape(2, B, RWKV_WIDTH // 2, T)
    wkv_p = s_fin.reshape(HD, K_LO, 2, B, H).transpose(3, 4, 2, 1, 0).reshape(B, H, HD, HD)
    y_prompt = tail(1, x_prompt, y_fm, g, bonus, ob, gates, 512, 1024, True).reshape(B, T, D)

    return (y_prompt, y_sample.reshape(NS, 1, D),
            wkv_p[None], shift_p.reshape(1, B, SHIFT_WIDTH), ret_p[None],
            wkv_s[None], shift_s[None], ret_s[None])
```

```python
import functools

import jax
import jax.numpy as jnp
from jax import lax
from jax.experimental import pallas as pl
from jax.experimental.pallas import tpu as pltpu

F32 = jnp.float32
BF16 = jnp.bfloat16

D_MODEL = 1024
N_META = 16
RWKV_WIDTH = 512
RWKV_HEADS = 8
RWKV_HEAD_DIM = 64
LORA_PAD = 128
GATE_LORA = 128
SHIFT_WIDTH = 3 * RWKV_WIDTH + 64 + 64 + GATE_LORA
LNX_EPS = 64e-5
RET_WIDTH = 512
RET_HEADS = 4
RET_DIM = 128
RET_CHUNK = 128
ROPE_BASE = 10000.0
RET_COL0 = SHIFT_WIDTH
GATE_COL0 = SHIFT_WIDTH + 4 * RET_WIDTH
IN_WIDTH = GATE_COL0 + 2 * D_MODEL
N_GROUPS = 4
EXPERTS_PER_GROUP = 4
N_EXPERTS = 16
EXPERT_FF = 256
RMS_EPS = 1e-6
PAST_LEN = 16384

LANES = 128
SUBLANES = 8
VMEM_LIMIT = 60 * 1024 * 1024

LEAD = 128
PK_W, PK_NKK, PK_BB, PK_K, PK_R, PK_V = range(6)
K_LO = 32
KL_UNROLL = 64


def _split_bf16(a, nterms):
    pieces = []
    rem = a
    for _ in range(nterms - 1):
        c = rem * 65537.0
        hi = c - (c - rem)
        pieces.append(hi.astype(BF16))
        rem = rem - hi
    pieces.append(rem.astype(BF16))
    return pieces


def _mm(a, w_refs, cols=None):
    def w(i):
        r = w_refs[i]
        return r[...] if cols is None else r[:, cols[0]:cols[1]]

    if a.dtype == BF16:
        acc = jnp.dot(a, w(0), preferred_element_type=F32)
        if len(w_refs) > 1:
            acc = acc + jnp.dot(a, w(1), preferred_element_type=F32)
        return acc
    if len(w_refs) == 1:
        return jnp.dot(a.astype(BF16), w(0), preferred_element_type=F32)
    a_hi, a_lo = _split_bf16(a, 2)
    acc = jnp.dot(a_hi, w(0), preferred_element_type=F32)
    acc = acc + jnp.dot(a_lo, w(0), preferred_element_type=F32)
    acc = acc + jnp.dot(a_hi, w(1), preferred_element_type=F32)
    return acc


def _rmsnorm(x, g):
    return x * lax.rsqrt(jnp.mean(x * x, axis=-1, keepdims=True) + RMS_EPS) * g


def _sigmoid(x):
    return 1.0 / (1.0 + jnp.exp(-x))


def _head64_sum(x):
    lane = lax.broadcasted_iota(jnp.int32, (x.shape[0], LANES), 1)
    lo = lane < RWKV_HEAD_DIM
    outs = []
    for j in range(x.shape[1] // LANES):
        xj = x[:, j * LANES:(j + 1) * LANES]
        s_lo = jnp.sum(jnp.where(lo, xj, 0.0), axis=-1, keepdims=True)
        s_hi = jnp.sum(jnp.where(lo, 0.0, xj), axis=-1, keepdims=True)
        outs.append(jnp.where(lo, s_lo, s_hi))
    return jnp.concatenate(outs, axis=-1)


def _rope(x, cc, ss):
    lane = lax.broadcasted_iota(jnp.int32, x.shape, 1)
    even = (lane & 1) == 0
    swapped = jnp.where(even, pltpu.roll(x, LANES - 1, 1), pltpu.roll(x, 1, 1))
    return x * cc + swapped * ss


def _rwkv_prep(s, prev, prm):
    (mu, w0, dup, a0, aup, gup, k_k, k_a, r_k) = prm
    xs = s + mu[...] * (prev - s)
    W = RWKV_WIDTH
    r = xs[:, 0:W]
    k = xs[:, W:2 * W]
    v = xs[:, 2 * W:3 * W]
    wa = xs[:, 3 * W:3 * W + LORA_PAD]
    gd = xs[:, 3 * W + LORA_PAD:]
    dec = _mm(jnp.tanh(wa), dup)
    z = -(w0[...] + dec)
    softplus = jnp.maximum(z, 0.0) + jnp.log1p(jnp.exp(-jnp.abs(z)))
    decay = jnp.exp(-jnp.exp(-softplus - 0.5))
    a = _sigmoid(a0[...] + _mm(wa, aup))
    g = _mm(_sigmoid(gd), gup)
    kk = k * k_k[...]
    kk = kk / jnp.maximum(jnp.sqrt(_head64_sum(kk * kk)), 1e-12)
    k_mod = k * (1.0 + (a - 1.0) * k_a[...])
    bonus = _head64_sum(r * k_mod * r_k[...]) * v
    return r, decay, k_mod, v, kk, a, g, bonus


def _take_prep_params(refs, n_w):
    it = iter(refs)
    take = lambda n: tuple(next(it) for _ in range(n))
    (mu, w0), dup, (a0,), aup, gup, (k_k, k_a, r_k) = take(2), take(n_w), take(1), take(n_w), take(n_w), take(3)
    return (mu, w0, dup, a0, aup, gup, k_k, k_a, r_k), tuple(it)


def _store_rwkv_operands(ops, pk_ref, v_ref, g_ref, bonus_ref):
    r, decay, k_mod, v, kk, a, g, bonus = ops
    W = RWKV_WIDTH
    pk_ref[:, PK_W * W:(PK_W + 1) * W] = decay
    pk_ref[:, PK_NKK * W:(PK_NKK + 1) * W] = -kk
    pk_ref[:, PK_BB * W:(PK_BB + 1) * W] = kk * a
    pk_ref[:, PK_K * W:(PK_K + 1) * W] = k_mod
    pk_ref[:, PK_R * W:(PK_R + 1) * W] = r
    v_ref[...] = v
    g_ref[...] = g
    bonus_ref[...] = bonus


def _inproj_prompt_kernel(n_w, x_ref, cc_ref, ss_ref, lead_ref, rt_ref, gc_ref, n1_ref, *rest):
    w_in = rest[:n_w]
    prm, rest = _take_prep_params(rest[n_w:], n_w)
    (pk_ref, g_ref, bonus_ref, gates_ref, ob_ref, shift_ref, ret_ref, prev_buf, ret_state) = rest
    t = pl.program_id(1)
    nb, tm, _ = x_ref.shape
    rows = [slice(b * tm, (b + 1) * tm) for b in range(nb)]

    @pl.when(t == 0)
    def _():
        prev_buf[:, 0:SUBLANES, :] = jnp.zeros((nb, SUBLANES, SHIFT_WIDTH), F32)
        ret_state[...] = jnp.zeros(ret_state.shape, F32)

    x = jnp.where(t == 0, lead_ref[...][None], x_ref[...])
    u = _rmsnorm(x.reshape(nb * tm, D_MODEL), n1_ref[...])
    u_b = u.astype(BF16) if n_w == 1 else u

    s = _mm(u_b, w_in, (0, SHIFT_WIDTH))
    ret = _mm(u_b, w_in, (RET_COL0, GATE_COL0))
    gates = _mm(u_b, w_in, (GATE_COL0, IN_WIDTH))
    prevs = []
    for b in range(nb):
        s_b = s[rows[b]]
        prev_buf[b, SUBLANES:SUBLANES + tm, :] = s_b
        prevs.append(prev_buf[b, SUBLANES - 1:SUBLANES - 1 + tm, :])
        prev_buf[b, SUBLANES - 1:SUBLANES, :] = s_b[tm - 1:tm, :]
        shift_ref[b] = s_b[tm - 1:tm, :]
    prev = jnp.concatenate(prevs, axis=0)

    r, decay, k_mod, v, kk, a, g, bonus = _rwkv_prep(s, prev, prm)
    for b in range(nb):
        rb = rows[b]
        for q, val in ((PK_W, decay), (PK_NKK, -kk), (PK_BB, kk * a), (PK_K, k_mod), (PK_R, r), (PK_V, v)):
            fm = val[rb].T
            for hi in range(2):
                pk_ref[q, hi, b] = jnp.concatenate(
                    [fm[h * RWKV_HEAD_DIM + hi * K_LO:h * RWKV_HEAD_DIM + (hi + 1) * K_LO] for h in range(RWKV_HEADS)],
                    axis=0)
        g_ref[b] = g[rb]
        bonus_ref[b] = bonus[rb]

    for b in range(nb):
        gates_ref[b] = gates[rows[b]]

    cc = cc_ref[...]
    ss = ss_ref[...]
    for b in range(nb):
        rb = rows[b]
        for h in range(RET_HEADS):
            c0 = h * RET_DIM
            qc = _rope(ret[rb, c0:c0 + RET_DIM], cc, ss).astype(BF16)
            kc = _rope(ret[rb, RET_WIDTH + c0:RET_WIDTH + c0 + RET_DIM], cc, ss) * (RET_DIM ** -0.5)
            vc = ret[rb, 2 * RET_WIDTH + c0:2 * RET_WIDTH + c0 + RET_DIM].astype(BF16)
            gr = ret[rb, 3 * RET_WIDTH + c0:3 * RET_WIDTH + c0 + RET_DIM]
            state = ret_state[b, h]
            scores = lax.dot_general(qc, kc.astype(BF16), (((1,), (1,)), ((), ())),
                                     preferred_element_type=F32) * rt_ref[h, 0]
            intra = jnp.dot(scores.astype(BF16), vc, preferred_element_type=F32)
            cross = jnp.dot(qc, state.astype(BF16), preferred_element_type=F32) * rt_ref[h, 1]
            kt = (kc * rt_ref[h, 2]).T.astype(BF16)
            ret_state[b, h] = gc_ref[h, 0:1, :] * state + jnp.dot(kt, vc, preferred_element_type=F32)
            o = intra + cross
            o = o * lax.rsqrt(jnp.mean(o * o, axis=-1, keepdims=True) + RMS_EPS)
            ob_ref[b, :, c0:c0 + RET_DIM] = (gr * _sigmoid(gr) * o).astype(ob_ref.dtype)

    @pl.when(t == pl.num_programs(1) - 1)
    def _():
        ret_ref[...] = ret_state[...]


def _inproj_sample_kernel(n_w, x_ref, prev_ref, cc_ref, ss_ref, n1_ref, *rest):
    w_in = rest[:n_w]
    prm, rest = _take_prep_params(rest[n_w:], n_w)
    (pk_ref, v_ref, g_ref, bonus_ref, gates_ref, retq_ref, shift_ref) = rest
    u = _rmsnorm(x_ref[...], n1_ref[...])
    s = _mm(u, w_in, (0, SHIFT_WIDTH))
    shift_ref[...] = s
    _store_rwkv_operands(_rwkv_prep(s, prev_ref[...], prm), pk_ref, v_ref, g_ref, bonus_ref)
    gates_ref[...] = _mm(u, w_in, (GATE_COL0, IN_WIDTH))
    ret = _mm(u, w_in, (RET_COL0, GATE_COL0))
    cc = cc_ref[...]
    ss = ss_ref[...]
    for h in range(RET_HEADS):
        c0 = h * RET_DIM
        retq_ref[:, c0:c0 + RET_DIM] = _rope(ret[:, c0:c0 + RET_DIM], cc, ss)
        retq_ref[:, RET_WIDTH + c0:RET_WIDTH + c0 + RET_DIM] = (
            _rope(ret[:, RET_WIDTH + c0:RET_WIDTH + c0 + RET_DIM], cc, ss) * (RET_DIM ** -0.5))
    retq_ref[:, 2 * RET_WIDTH:] = ret[:, 2 * RET_WIDTH:]


def _const_spec(arr):
    nd = arr.ndim
    return pl.BlockSpec(arr.shape, lambda *_: (0,) * nd, pipeline_mode=pl.Buffered(1))


def _inproj_prompt(x, lead, cc, ss, rt_tab, gc_tab, n1, w_in, prm, nb):
    B, T, _ = x.shape
    tm = RET_CHUNK
    assert lead.shape[0] == tm and T % tm == 0 and B % nb == 0
    TP = tm + T
    n_w = len(w_in)
    consts = (lead, rt_tab, gc_tab, n1) + tuple(w_in) + tuple(prm)
    prow = lambda width: pl.BlockSpec((nb, tm, width), lambda b, t: (b, jnp.maximum(t - 1, 0), 0))
    in_specs = [prow(D_MODEL),
                pl.BlockSpec((tm, LANES), lambda b, t: (t, 0)),
                pl.BlockSpec((tm, LANES), lambda b, t: (t, 0))] + [_const_spec(c) for c in consts]
    out_shape = (
        jax.ShapeDtypeStruct((6, 2, B, RWKV_WIDTH // 2, TP), F32),
        jax.ShapeDtypeStruct((B, T, RWKV_WIDTH), F32),
        jax.ShapeDtypeStruct((B, T, RWKV_WIDTH), F32),
        jax.ShapeDtypeStruct((B, T, 2 * D_MODEL), F32),
        jax.ShapeDtypeStruct((B, T, RET_WIDTH), BF16),
        jax.ShapeDtypeStruct((B, 1, SHIFT_WIDTH), F32),
        jax.ShapeDtypeStruct((B, RET_HEADS, RET_DIM, RET_DIM), F32),
    )
    out_specs = (
        pl.BlockSpec((6, 2, nb, RWKV_WIDTH // 2, tm), lambda b, t: (0, 0, b, 0, t)),
        prow(RWKV_WIDTH), prow(RWKV_WIDTH), prow(2 * D_MODEL), prow(RET_WIDTH),
        pl.BlockSpec((nb, 1, SHIFT_WIDTH), lambda b, t: (b, 0, 0)),
        pl.BlockSpec((nb, RET_HEADS, RET_DIM, RET_DIM), lambda b, t: (b, 0, 0, 0)),
    )
    return pl.pallas_call(
        functools.partial(_inproj_prompt_kernel, n_w),
        grid=(B // nb, TP // tm),
        in_specs=in_specs,
        out_specs=out_specs,
        out_shape=out_shape,
        scratch_shapes=[pltpu.VMEM((nb, tm + SUBLANES, SHIFT_WIDTH), F32),
                        pltpu.VMEM((nb, RET_HEADS, RET_DIM, RET_DIM), F32)],
        compiler_params=pltpu.CompilerParams(
            dimension_semantics=("arbitrary", "arbitrary"), vmem_limit_bytes=VMEM_LIMIT),
        name="inproj_prompt",
    )(x, cc, ss, *consts)


def _inproj_sample(xs, prev, cc, ss, n1, w_in, prm):
    N = xs.shape[0]
    n_w = len(w_in)
    args = (xs, prev, cc, ss, n1) + tuple(w_in) + tuple(prm)
    out_shape = (
        jax.ShapeDtypeStruct((N, 5 * RWKV_WIDTH), F32),
        jax.ShapeDtypeStruct((N, RWKV_WIDTH), F32),
        jax.ShapeDtypeStruct((N, RWKV_WIDTH), F32),
        jax.ShapeDtypeStruct((N, RWKV_WIDTH), F32),
        jax.ShapeDtypeStruct((N, 2 * D_MODEL), F32),
        jax.ShapeDtypeStruct((N, 4 * RET_WIDTH), F32),
        jax.ShapeDtypeStruct((N, SHIFT_WIDTH), F32),
    )
    return pl.pallas_call(
        functools.partial(_inproj_sample_kernel, n_w),
        out_shape=out_shape,
        compiler_params=pltpu.CompilerParams(vmem_limit_bytes=VMEM_LIMIT),
        name="inproj_sample",
    )(*args)


def _rwkv_prompt_kernel(pk_ref, nxt_ref, y_ref, sout_ref, state, sa_buf, kd):
    i = pl.program_id(0)
    tb = pk_ref.shape[0]
    HD = RWKV_HEAD_DIM
    VL = K_LO

    @pl.when(i == 0)
    def _():
        state[...] = jnp.zeros(state.shape, F32)
        sa_buf[...] = jnp.zeros(sa_buf.shape, F32)

    first_half = lax.broadcasted_iota(jnp.int32, (K_LO, LANES), 1) < LANES // 2

    def spread(x, t, q):
        swapped = pltpu.roll(x, LANES // 2, 1)
        kd[t, q, 0:K_LO, :] = jnp.where(first_half, x, swapped)
        kd[t, q, K_LO:HD, :] = jnp.where(first_half, swapped, x)

    for t in range(tb):
        for q in (PK_W, PK_NKK, PK_BB, PK_K, PK_R):
            spread(pk_ref[t, q], t, q)
    spread(nxt_ref[0, 0], tb, PK_NKK)

    def step(t, sa):
        vv = pk_ref[t, PK_V]

        def key_rows(c, accs):
            yacc, sacc = accs
            for j in range(KL_UNROLL):
                k = c * KL_UNROLL + j
                row = lambda tt, q: kd[tt, q, pl.ds(k, 1), :]
                rows = pl.ds(pl.multiple_of(k * VL, VL), VL)
                new = state[rows, :] * row(t, PK_W) + sa * row(t, PK_BB) + vv * row(t, PK_K)
                state[rows, :] = new
                yacc = yacc + new * row(t, PK_R)
                sacc = sacc + new * row(t + 1, PK_NKK)
            return yacc, sacc

        zero = jnp.zeros((VL, LANES), F32)
        yacc, sacc = lax.fori_loop(0, HD // KL_UNROLL, key_rows, (zero, zero))
        y_ref[t] = yacc
        return sacc

    sa_buf[...] = lax.fori_loop(0, tb, step, sa_buf[...])

    @pl.when(i == pl.num_programs(0) - 1)
    def _():
        sout_ref[...] = state[...]


def _rwkv_prompt(pk_t, tb):
    TP = pk_t.shape[0]
    HD = RWKV_HEAD_DIM
    assert LEAD % tb == 0
    skip = (LEAD - N_META) // tb
    lead_blocks = LEAD // tb - skip
    return pl.pallas_call(
        _rwkv_prompt_kernel,
        grid=(TP // tb - skip,),
        in_specs=[pl.BlockSpec((tb, 6, K_LO, LANES), lambda i: (i + skip, 0, 0, 0)),
                  pl.BlockSpec((1, 1, K_LO, LANES),
                               lambda i: (jnp.minimum((i + skip + 1) * tb, TP - 1), PK_NKK, 0, 0))],
        out_specs=(pl.BlockSpec((tb, K_LO, LANES), lambda i: (jnp.maximum(i - lead_blocks, 0), 0, 0)),
                   pl.BlockSpec((K_LO * HD, LANES), lambda i: (0, 0))),
        out_shape=(jax.ShapeDtypeStruct((TP - LEAD, K_LO, LANES), F32),
                   jax.ShapeDtypeStruct((K_LO * HD, LANES), F32)),
        scratch_shapes=[pltpu.VMEM((HD * K_LO, LANES), F32), pltpu.VMEM((K_LO, LANES), F32),
                        pltpu.VMEM((tb + 1, 5, HD, LANES), F32)],
        compiler_params=pltpu.CompilerParams(dimension_semantics=("arbitrary",)),
        name="rwkv_prompt",
    )(pk_t, pk_t)


def _lane_col(tile, h):
    lane = lax.broadcasted_iota(jnp.int32, tile.shape, 1)
    return jnp.sum(jnp.where(lane == h, tile, 0.0), axis=-1, keepdims=True)


def _rwkv_sample_kernel(s_ref, pk_ref, v_ref, sout_ref, y_ref):
    def value_row(v, carry):
        S = s_ref[v]
        sa = jnp.sum(S * pk_ref[PK_NKK], axis=0, keepdims=True)
        new = S * pk_ref[PK_W] + sa * pk_ref[PK_BB] + v_ref[pl.ds(v, 1), :] * pk_ref[PK_K]
        sout_ref[v] = new
        y_ref[pl.ds(v, 1), :] = jnp.sum(new * pk_ref[PK_R], axis=0, keepdims=True)
        return carry

    lax.fori_loop(0, s_ref.shape[0], value_row, 0, unroll=2)


def _rwkv_sample(state_t, pk_t, v_t):
    H, HD, _, N = state_t.shape
    return pl.pallas_call(
        _rwkv_sample_kernel,
        grid=(H,),
        in_specs=[pl.BlockSpec((None, HD, HD, N), lambda h: (h, 0, 0, 0)),
                  pl.BlockSpec((5, None, HD, N), lambda h: (0, h, 0, 0)),
                  pl.BlockSpec((None, HD, N), lambda h: (h, 0, 0))],
        out_specs=(pl.BlockSpec((None, HD, HD, N), lambda h: (h, 0, 0, 0)),
                   pl.BlockSpec((None, HD, N), lambda h: (h, 0, 0))),
        out_shape=(jax.ShapeDtypeStruct(state_t.shape, F32),
                   jax.ShapeDtypeStruct((H, HD, N), F32)),
        compiler_params=pltpu.CompilerParams(dimension_semantics=("arbitrary",)),
        name="rwkv_sample",
    )(state_t, pk_t, v_t)


def _ret_sample_kernel(s_ref, qt_ref, kt_ref, v_ref, g_ref, gam_ref, sout_ref, ob_ref):
    gam = gam_ref[...][:, None, :]
    for b in range(s_ref.shape[0]):
        q_tile = qt_ref[b]
        k_tile = kt_ref[b]
        qc = jnp.stack([_lane_col(q_tile, h) for h in range(RET_HEADS)], axis=0)
        kc = jnp.stack([_lane_col(k_tile, h) for h in range(RET_HEADS)], axis=0)
        vrow = v_ref[b][:, None, :]
        S = s_ref[b]
        qk = jnp.sum(qc * kc, axis=1, keepdims=True)
        cross = jnp.sum(qc * S, axis=1, keepdims=True)
        o = qk * vrow + cross * gam
        sout_ref[b] = gam * S + kc * vrow
        o = o * lax.rsqrt(jnp.mean(o * o, axis=-1, keepdims=True) + RMS_EPS)
        gr = g_ref[b][:, None, :]
        ob_ref[b] = (gr * _sigmoid(gr) * o)[:, 0, :]


def _ret_sample(state, q_t, k_t, v_h, g_h, gam, bt):
    N = state.shape[0]
    H, DK = RET_HEADS, RET_DIM
    col = pl.BlockSpec((bt, DK, H), lambda i: (i, 0, 0))
    rowspec = pl.BlockSpec((bt, H, DK), lambda i: (i, 0, 0))
    st = pl.BlockSpec((bt, H, DK, DK), lambda i: (i, 0, 0, 0))
    return pl.pallas_call(
        _ret_sample_kernel,
        grid=(N // bt,),
        in_specs=[st, col, col, rowspec, rowspec, pl.BlockSpec((H, DK), lambda i: (0, 0))],
        out_specs=(st, rowspec),
        out_shape=(jax.ShapeDtypeStruct(state.shape, F32), jax.ShapeDtypeStruct((N, H, DK), F32)),
        compiler_params=pltpu.CompilerParams(dimension_semantics=("arbitrary",)),
        name="ret_sample",
    )(state, q_t, k_t, v_h, g_h, gam)


def _mixtail_kernel(n_w, y_feature_major, x_ref, y_ref, g_ref, bonus_ref, ob_ref, gates_ref,
                    lnw_ref, lnb_ref, n2_ref, br_ref, *rest):
    wa = rest[0:n_w]
    wb = rest[n_w:2 * n_w]
    wo = rest[2 * n_w:3 * n_w]
    wr = rest[3 * n_w:3 * n_w + 2]
    x1_ref, u2_ref, comb_ref = rest[3 * n_w + 2:]

    inv_n = 1.0 / RWKV_HEAD_DIM
    if y_feature_major:
        normed = []
        for h in range(RWKV_HEADS):
            yh = jnp.concatenate([y_ref[hi, h * K_LO:(h + 1) * K_LO, :] for hi in range(2)], axis=0)
            d = yh - jnp.sum(yh, axis=0, keepdims=True) * inv_n
            var = jnp.sum(d * d, axis=0, keepdims=True) * inv_n
            normed.append(d * lax.rsqrt(var + LNX_EPS))
        yhat = jnp.concatenate(normed, axis=0).T
    else:
        y = y_ref[...]
        d = y - _head64_sum(y) * inv_n
        var = _head64_sum(d * d) * inv_n
        yhat = d * lax.rsqrt(var + LNX_EPS)
    yn = yhat * lnw_ref[...] + lnb_ref[...]
    ya = (yn + bonus_ref[...]) * g_ref[...]
    branch_a = _mm(ya, wa)
    branch_b = _mm(ob_ref[...], wb)
    gates = gates_ref[...]
    merged = _sigmoid(gates[:, :D_MODEL]) * branch_a + _sigmoid(gates[:, D_MODEL:]) * branch_b
    x1 = x_ref[...] + _mm(merged, wo)
    x1_ref[...] = x1
    u2 = _rmsnorm(x1, n2_ref[...])
    u2_ref[...] = u2.astype(u2_ref.dtype)

    logits = (_mm(u2, wr) + br_ref[...]).T
    tm = logits.shape[1]
    neg = jnp.float32(-jnp.inf)
    big = jnp.float32(1e9)
    first = lambda hit, idx: jnp.min(jnp.where(hit, idx, big), axis=0, keepdims=True)
    lg = logits[N_EXPERTS:N_EXPERTS + N_GROUPS]
    gidx = lax.broadcasted_iota(jnp.int32, lg.shape, 0).astype(F32)
    lg_max = jnp.max(lg, axis=0, keepdims=True)
    p_grp = 1.0 / jnp.sum(jnp.exp(lg - lg_max), axis=0, keepdims=True)
    grp = first(lg == lg_max, gidx)
    eidx = lax.broadcasted_iota(jnp.int32, (N_EXPERTS, tm), 0).astype(F32)
    in_grp = (eidx >= grp * EXPERTS_PER_GROUP) & (eidx < (grp + 1.0) * EXPERTS_PER_GROUP)
    le = jnp.where(in_grp, logits[0:N_EXPERTS], neg)
    v1 = jnp.max(le, axis=0, keepdims=True)
    i1 = first(le == v1, eidx)
    le2 = jnp.where(eidx == i1, neg, le)
    v2 = jnp.max(le2, axis=0, keepdims=True)
    i2 = first(le2 == v2, eidx)
    e2 = jnp.exp(v2 - v1)
    w1 = 1.0 / (1.0 + e2)
    w2 = e2 / (1.0 + e2)
    comb = p_grp * (jnp.where(eidx == i1, w1, 0.0) + jnp.where(eidx == i2, w2, 0.0))
    comb_ref[...] = jnp.concatenate([comb, jnp.zeros((LANES - N_EXPERTS, tm), F32)], axis=0).T


def _mixtail(x, y, g, bonus, ob, gates, lnw, lnb, n2, br, wa, wb, wo, wr, tm, y_feature_major):
    G, n, _ = x.shape
    assert n % tm == 0
    n_w = len(wa)
    consts = (lnw, lnb, n2, br) + tuple(wa) + tuple(wb) + tuple(wo) + tuple(wr)
    row = lambda width: pl.BlockSpec((None, tm, width), lambda b, i: (b, i, 0))
    y_spec = (pl.BlockSpec((2, None, RWKV_WIDTH // 2, tm), lambda b, i: (0, b, 0, i)) if y_feature_major
              else row(RWKV_WIDTH))
    in_specs = [row(D_MODEL), y_spec, row(RWKV_WIDTH), row(RWKV_WIDTH), row(RET_WIDTH),
                row(2 * D_MODEL)] + [_const_spec(c) for c in consts]
    return pl.pallas_call(
        functools.partial(_mixtail_kernel, n_w, y_feature_major),
        grid=(G, n // tm),
        in_specs=in_specs,
        out_specs=(row(D_MODEL), row(D_MODEL), row(LANES)),
        out_shape=(jax.ShapeDtypeStruct((G, n, D_MODEL), F32),
                   jax.ShapeDtypeStruct((G, n, D_MODEL), BF16),
                   jax.ShapeDtypeStruct((G, n, LANES), F32)),
        compiler_params=pltpu.CompilerParams(
            dimension_semantics=("arbitrary", "arbitrary"), vmem_limit_bytes=VMEM_LIMIT),
        name="mixtail",
    )(x, y, g, bonus, ob, gates, *consts)


def _moe_kernel(x1_ref, u2_ref, comb_ref, wgu_ref, wd_ref, fg_ref, o_ref, acc):
    grp = pl.program_id(1)

    @pl.when(grp == 0)
    def _():
        acc[...] = jnp.zeros(acc.shape, F32)

    u2 = u2_ref[...]
    comb = comb_ref[...]
    lane = lax.broadcasted_iota(jnp.int32, comb.shape, 1)
    total = acc[...]
    for j in range(EXPERTS_PER_GROUP):
        c = jnp.sum(jnp.where(lane == grp * EXPERTS_PER_GROUP + j, comb, 0.0), axis=-1, keepdims=True)
        hgu = jnp.dot(u2, wgu_ref[j], preferred_element_type=F32)
        hg = hgu[:, :EXPERT_FF]
        h = hg * _sigmoid(hg) * hgu[:, EXPERT_FF:] * c
        total = total + jnp.dot(h.astype(BF16), wd_ref[j], preferred_element_type=F32)
    acc[...] = total

    @pl.when(grp == pl.num_programs(1) - 1)
    def _():
        o_ref[...] = _rmsnorm(x1_ref[...] + acc[...], fg_ref[...])


def _moe(x1, u2, comb, wgu, wd, fg, tm):
    N = x1.shape[0]
    E = EXPERTS_PER_GROUP
    return pl.pallas_call(
        _moe_kernel,
        grid=(N // tm, N_GROUPS),
        in_specs=[pl.BlockSpec((tm, D_MODEL), lambda i, g: (i, 0)),
                  pl.BlockSpec((tm, D_MODEL), lambda i, g: (i, 0)),
                  pl.BlockSpec((tm, LANES), lambda i, g: (i, 0)),
                  pl.BlockSpec((E, D_MODEL, 2 * EXPERT_FF), lambda i, g: (g, 0, 0)),
                  pl.BlockSpec((E, EXPERT_FF, D_MODEL), lambda i, g: (g, 0, 0)),
                  pl.BlockSpec((1, D_MODEL), lambda i, g: (0, 0))],
        out_specs=pl.BlockSpec((tm, D_MODEL), lambda i, g: (i, 0)),
        out_shape=jax.ShapeDtypeStruct((N, D_MODEL), F32),
        scratch_shapes=[pltpu.VMEM((tm, D_MODEL), F32)],
        compiler_params=pltpu.CompilerParams(
            dimension_semantics=("arbitrary", "arbitrary"), vmem_limit_bytes=VMEM_LIMIT),
        name="moe",
    )(x1, u2, comb, wgu, wd, fg)


def _pieces(w, n):
    return tuple(_split_bf16(w.astype(F32), n))


def _rope_tables(pos):
    half = RET_DIM // 2
    inv = ROPE_BASE ** (-jnp.arange(half, dtype=F32) / half)
    ang = pos.astype(F32)[:, None] * inv[None, :]
    c = jnp.cos(ang)
    s = jnp.sin(ang)
    cc = jnp.repeat(c, 2, axis=-1)
    ss = jnp.stack([-s, s], axis=-1).reshape(pos.shape[0], RET_DIM)
    return cc, ss


def _retention_tables():
    C = RET_CHUNK
    log_gamma = jnp.log(1.0 - 2.0 ** (-5.0 - jnp.arange(RET_HEADS, dtype=F32)))
    idx = jnp.arange(C, dtype=F32)
    diff = idx[:, None] - idx[None, :]
    lg = log_gamma[:, None, None]
    dmask = jnp.where(diff[None] >= 0, jnp.exp(jnp.maximum(diff, 0.0)[None] * lg), 0.0)
    q_dec = jnp.exp((idx[None, :] + 1.0) * log_gamma[:, None])
    k_dec = jnp.exp((C - 1.0 - idx[None, :]) * log_gamma[:, None])
    bc = lambda a: jnp.broadcast_to(a[:, :, None], (RET_HEADS, C, C))
    rt_tab = jnp.stack([dmask, bc(q_dec), bc(k_dec)], axis=1)
    gc_tab = jnp.broadcast_to(jnp.exp(C * log_gamma)[:, None, None], (RET_HEADS, SUBLANES, RET_DIM))
    gam1 = jnp.broadcast_to(jnp.exp(log_gamma)[:, None], (RET_HEADS, RET_DIM))
    return rt_tab, gc_tab, gam1


def kernel(x_prompt, x_sample, state_wkv, state_shift, state_ret, meta_tokens, norm1_g, w_in, shift_mu, decay_w0, decay_up, aaa_a0, aaa_up, gate_up, k_k, k_a, r_k, lnx_w, lnx_b, w_branch_a, w_branch_b, w_out, norm2_g, router_group_w, router_group_b, router_expert_w, router_expert_b, expert_w_gate, expert_w_up, expert_w_down, final_norm_g):
    B, T, D = x_prompt.shape
    NS = x_sample.shape[0]
    TP = LEAD + T
    H, HD = RWKV_HEADS, RWKV_HEAD_DIM
    l = 0
    row = lambda a: a.reshape(1, -1)

    zpad = jnp.zeros((LORA_PAD - 64, RWKV_WIDTH), F32)
    dup_pad = jnp.concatenate([decay_up[l], zpad], axis=0)
    aup_pad = jnp.concatenate([zpad, aaa_up[l]], axis=0)
    wr_full = jnp.concatenate([router_expert_w[l], router_group_w[l]], axis=1)
    wr_full = jnp.pad(wr_full, ((0, 0), (0, LANES - wr_full.shape[1])))
    br_full = jnp.pad(jnp.concatenate([router_expert_b[l], router_group_b[l]]), (0, LANES - N_EXPERTS - N_GROUPS))
    wr = _pieces(wr_full, 2)
    wgu = jnp.concatenate([expert_w_gate[l], expert_w_up[l]], axis=-1).astype(BF16)
    wd = expert_w_down[l].astype(BF16)
    rt_tab, gc_tab, gam1 = _retention_tables()

    split = {name: _pieces(w, 2) for name, w in (
        ("w_in", w_in[l]), ("dup", dup_pad), ("aup", aup_pad), ("gup", gate_up[l]),
        ("wa", w_branch_a[l]), ("wb", w_branch_b[l]), ("wo", w_out[l]))}

    def prep_params(n):
        return (row(shift_mu[l]), row(decay_w0[l]), split["dup"][:n], row(aaa_a0[l]), split["aup"][:n],
                split["gup"][:n], row(k_k[l]), row(k_a[l]), row(r_k[l]))

    def flat_params(p):
        out = []
        for a in p:
            out.extend(a if isinstance(a, tuple) else (a,))
        return tuple(out)

    def tail(n, x, y, g, bonus, ob, gates, tm_tail, tm_moe, y_feature_major):
        x1, u2, comb = _mixtail(x, y, g, bonus, ob, gates, row(lnx_w[l]), row(lnx_b[l]), row(norm2_g[l]),
                                row(br_full), split["wa"][:n], split["wb"][:n], split["wo"][:n], wr, tm_tail,
                                y_feature_major)
        flat = lambda a: a.reshape(-1, a.shape[-1])
        return _moe(flat(x1), flat(u2), flat(comb), wgu, wd, row(final_norm_g), tm_moe)

    lead = jnp.concatenate([jnp.zeros((LEAD - N_META, D), F32), meta_tokens.astype(F32)], axis=0)
    cc_p, ss_p = _rope_tables(jnp.arange(TP, dtype=jnp.int32) - (LEAD - N_META))
    pk, g, bonus, gates, ob, shift_p, ret_p = _inproj_prompt(
        x_prompt, lead, cc_p, ss_p, rt_tab, gc_tab, row(norm1_g[l]), split["w_in"][:1],
        flat_params(prep_params(1)), nb=4)

    pk_t = pk.reshape(6, 2 * B * H, K_LO, TP).transpose(3, 0, 2, 1)

    shift_p, xs = lax.optimization_barrier((shift_p, x_sample.reshape(NS, D)))
    cc_s, ss_s = _rope_tables(jnp.full((NS,), PAST_LEN, jnp.int32))
    pk_s, v_s, g_s, bonus_s, gates_s, retq, shift_s = _inproj_sample(
        xs, state_shift[l], cc_s, ss_s, row(norm1_g[l]), split["w_in"], flat_params(prep_params(2)))
    wkv_st, y_st = _rwkv_sample(state_wkv[l].transpose(1, 2, 3, 0),
                                pk_s.reshape(NS, 5, H, HD).transpose(1, 2, 3, 0),
                                v_s.reshape(NS, H, HD).transpose(1, 2, 0))
    wkv_s = wkv_st.transpose(3, 0, 1, 2)
    y_s = y_st.transpose(2, 0, 1).reshape(NS, RWKV_WIDTH)
    heads = lambda a: a.reshape(NS, RET_HEADS, RET_DIM)
    q_s, k_s, v_r, g_r = (retq[:, i * RET_WIDTH:(i + 1) * RET_WIDTH] for i in range(4))
    ret_s, ob_s = _ret_sample(state_ret[l], heads(q_s).transpose(0, 2, 1), heads(k_s).transpose(0, 2, 1),
                              heads(v_r), heads(g_r), gam1, bt=8)
    y_sample = tail(2, xs[None], y_s[None], g_s[None], bonus_s[None], ob_s.reshape(1, NS, RET_WIDTH), gates_s[None],
                    NS, NS, False)

    pk_t, y_sample, wkv_s, ret_s = lax.optimization_barrier((pk_t, y_sample, wkv_s, ret_s))
    y_t, s_fin = _rwkv_prompt(pk_t, tb=32)
    y_fm = y_t.transpose(2, 1, 0).reshape(2, B, RWKV_WIDTH // 2, T)
    wkv_p = s_fin.reshape(HD, K_LO, 2, B, H).transpose(3, 4, 2, 1, 0).reshape(B, H, HD, HD)
    y_prompt = tail(1, x_prompt, y_fm, g, bonus, ob, gates, 512, 1024, True).reshape(B, T, D)

    return (y_prompt, y_sample.reshape(NS, 1, D),
            wkv_p[None], shift_p.reshape(1, B, SHIFT_WIDTH), ret_p[None],
            wkv_s[None], shift_s[None], ret_s[None])
```

```python
import functools

import jax
import jax.numpy as jnp
from jax import lax
from jax.experimental import pallas as pl
from jax.experimental.pallas import tpu as pltpu

F32 = jnp.float32
BF16 = jnp.bfloat16

D_MODEL = 1024
N_META = 16
RWKV_WIDTH = 512
RWKV_HEADS = 8
RWKV_HEAD_DIM = 64
LORA_PAD = 128
GATE_LORA = 128
SHIFT_WIDTH = 3 * RWKV_WIDTH + 64 + 64 + GATE_LORA
LNX_EPS = 64e-5
RET_WIDTH = 512
RET_HEADS = 4
RET_DIM = 128
RET_CHUNK = 128
ROPE_BASE = 10000.0
RET_COL0 = SHIFT_WIDTH
GATE_COL0 = SHIFT_WIDTH + 4 * RET_WIDTH
IN_WIDTH = GATE_COL0 + 2 * D_MODEL
N_GROUPS = 4
EXPERTS_PER_GROUP = 4
N_EXPERTS = 16
EXPERT_FF = 256
RMS_EPS = 1e-6
PAST_LEN = 16384

LANES = 128
SUBLANES = 8
VMEM_LIMIT = 60 * 1024 * 1024

LEAD = 128
PK_W, PK_NKK, PK_BB, PK_K, PK_R, PK_V = range(6)
K_LO = 32
KL_UNROLL = 64


def _split_bf16(a, nterms):
    pieces = []
    rem = a
    for _ in range(nterms - 1):
        c = rem * 65537.0
        hi = c - (c - rem)
        pieces.append(hi.astype(BF16))
        rem = rem - hi
    pieces.append(rem.astype(BF16))
    return pieces


def _mm(a, w_refs, cols=None):
    def w(i):
        r = w_refs[i]
        return r[...] if cols is None else r[:, cols[0]:cols[1]]

    if a.dtype == BF16:
        acc = jnp.dot(a, w(0), preferred_element_type=F32)
        if len(w_refs) > 1:
            acc = acc + jnp.dot(a, w(1), preferred_element_type=F32)
        return acc
    if len(w_refs) == 1:
        return jnp.dot(a.astype(BF16), w(0), preferred_element_type=F32)
    a_hi, a_lo = _split_bf16(a, 2)
    acc = jnp.dot(a_hi, w(0), preferred_element_type=F32)
    acc = acc + jnp.dot(a_lo, w(0), preferred_element_type=F32)
    acc = acc + jnp.dot(a_hi, w(1), preferred_element_type=F32)
    return acc


def _rmsnorm(x, g):
    return x * lax.rsqrt(jnp.mean(x * x, axis=-1, keepdims=True) + RMS_EPS) * g


def _sigmoid(x):
    return 1.0 / (1.0 + jnp.exp(-x))


def _head64_sum(x):
    lane = lax.broadcasted_iota(jnp.int32, (x.shape[0], LANES), 1)
    lo = lane < RWKV_HEAD_DIM
    outs = []
    for j in range(x.shape[1] // LANES):
        xj = x[:, j * LANES:(j + 1) * LANES]
        s_lo = jnp.sum(jnp.where(lo, xj, 0.0), axis=-1, keepdims=True)
        s_hi = jnp.sum(jnp.where(lo, 0.0, xj), axis=-1, keepdims=True)
        outs.append(jnp.where(lo, s_lo, s_hi))
    return jnp.concatenate(outs, axis=-1)


def _rope(x, cc, ss):
    lane = lax.broadcasted_iota(jnp.int32, x.shape, 1)
    even = (lane & 1) == 0
    swapped = jnp.where(even, pltpu.roll(x, LANES - 1, 1), pltpu.roll(x, 1, 1))
    return x * cc + swapped * ss


def _rwkv_prep(s, prev, prm):
    (mu, w0, dup, a0, aup, gup, k_k, k_a, r_k) = prm
    xs = s + mu[...] * (prev - s)
    W = RWKV_WIDTH
    r = xs[:, 0:W]
    k = xs[:, W:2 * W]
    v = xs[:, 2 * W:3 * W]
    wa = xs[:, 3 * W:3 * W + LORA_PAD]
    gd = xs[:, 3 * W + LORA_PAD:]
    dec = _mm(jnp.tanh(wa), dup)
    z = -(w0[...] + dec)
    softplus = jnp.maximum(z, 0.0) + jnp.log1p(jnp.exp(-jnp.abs(z)))
    decay = jnp.exp(-jnp.exp(-softplus - 0.5))
    a = _sigmoid(a0[...] + _mm(wa, aup))
    g = _mm(_sigmoid(gd), gup)
    kk = k * k_k[...]
    kk = kk / jnp.maximum(jnp.sqrt(_head64_sum(kk * kk)), 1e-12)
    k_mod = k * (1.0 + (a - 1.0) * k_a[...])
    bonus = _head64_sum(r * k_mod * r_k[...]) * v
    return r, decay, k_mod, v, kk, a, g, bonus


def _take_prep_params(refs, n_w):
    it = iter(refs)
    take = lambda n: tuple(next(it) for _ in range(n))
    (mu, w0), dup, (a0,), aup, gup, (k_k, k_a, r_k) = take(2), take(n_w), take(1), take(n_w), take(n_w), take(3)
    return (mu, w0, dup, a0, aup, gup, k_k, k_a, r_k), tuple(it)


def _store_rwkv_operands(ops, pk_ref, v_ref, g_ref, bonus_ref):
    r, decay, k_mod, v, kk, a, g, bonus = ops
    W = RWKV_WIDTH
    pk_ref[:, PK_W * W:(PK_W + 1) * W] = decay
    pk_ref[:, PK_NKK * W:(PK_NKK + 1) * W] = -kk
    pk_ref[:, PK_BB * W:(PK_BB + 1) * W] = kk * a
    pk_ref[:, PK_K * W:(PK_K + 1) * W] = k_mod
    pk_ref[:, PK_R * W:(PK_R + 1) * W] = r
    v_ref[...] = v
    g_ref[...] = g
    bonus_ref[...] = bonus


def _inproj_prompt_kernel(n_w, x_ref, cc_ref, ss_ref, lead_ref, rt_ref, gc_ref, n1_ref, *rest):
    w_in = rest[:n_w]
    prm, rest = _take_prep_params(rest[n_w:], n_w)
    (pk_ref, g_ref, bonus_ref, gates_ref, ob_ref, shift_ref, ret_ref, prev_buf, ret_state) = rest
    t = pl.program_id(1)
    nb, tm, _ = x_ref.shape
    rows = [slice(b * tm, (b + 1) * tm) for b in range(nb)]

    @pl.when(t == 0)
    def _():
        prev_buf[:, 0:SUBLANES, :] = jnp.zeros((nb, SUBLANES, SHIFT_WIDTH), F32)
        ret_state[...] = jnp.zeros(ret_state.shape, F32)

    x = jnp.where(t == 0, lead_ref[...][None], x_ref[...])
    u = _rmsnorm(x.reshape(nb * tm, D_MODEL), n1_ref[...])
    u_b = u.astype(BF16) if n_w == 1 else u

    s = _mm(u_b, w_in, (0, SHIFT_WIDTH))
    ret = _mm(u_b, w_in, (RET_COL0, GATE_COL0))
    prevs = []
    for b in range(nb):
        s_b = s[rows[b]]
        prev_buf[b, SUBLANES:SUBLANES + tm, :] = s_b
        prevs.append(prev_buf[b, SUBLANES - 1:SUBLANES - 1 + tm, :])
        prev_buf[b, SUBLANES - 1:SUBLANES, :] = s_b[tm - 1:tm, :]
        shift_ref[b] = s_b[tm - 1:tm, :]
    prev = jnp.concatenate(prevs, axis=0)

    r, decay, k_mod, v, kk, a, g, bonus = _rwkv_prep(s, prev, prm)
    for b in range(nb):
        rb = rows[b]
        for q, val in ((PK_W, decay), (PK_NKK, -kk), (PK_BB, kk * a), (PK_K, k_mod), (PK_R, r), (PK_V, v)):
            fm = val[rb].T
            for hi in range(2):
                pk_ref[q, hi, b] = jnp.concatenate(
                    [fm[h * RWKV_HEAD_DIM + hi * K_LO:h * RWKV_HEAD_DIM + (hi + 1) * K_LO] for h in range(RWKV_HEADS)],
                    axis=0)
        g_ref[b] = g[rb]
        bonus_ref[b] = bonus[rb]

    gates = _mm(u_b, w_in, (GATE_COL0, IN_WIDTH))
    for b in range(nb):
        gates_ref[b] = gates[rows[b]]

    cc = cc_ref[...]
    ss = ss_ref[...]
    for b in range(nb):
        rb = rows[b]
        for h in range(RET_HEADS):
            c0 = h * RET_DIM
            qc = _rope(ret[rb, c0:c0 + RET_DIM], cc, ss).astype(BF16)
            kc = _rope(ret[rb, RET_WIDTH + c0:RET_WIDTH + c0 + RET_DIM], cc, ss) * (RET_DIM ** -0.5)
            vc = ret[rb, 2 * RET_WIDTH + c0:2 * RET_WIDTH + c0 + RET_DIM].astype(BF16)
            gr = ret[rb, 3 * RET_WIDTH + c0:3 * RET_WIDTH + c0 + RET_DIM]
            state = ret_state[b, h]
            scores = lax.dot_general(qc, kc.astype(BF16), (((1,), (1,)), ((), ())),
                                     preferred_element_type=F32) * rt_ref[h, 0]
            intra = jnp.dot(scores.astype(BF16), vc, preferred_element_type=F32)
            cross = jnp.dot(qc, state.astype(BF16), preferred_element_type=F32) * rt_ref[h, 1]
            kt = (kc * rt_ref[h, 2]).T.astype(BF16)
            ret_state[b, h] = gc_ref[h, 0:1, :] * state + jnp.dot(kt, vc, preferred_element_type=F32)
            o = intra + cross
            o = o * lax.rsqrt(jnp.mean(o * o, axis=-1, keepdims=True) + RMS_EPS)
            ob_ref[b, :, c0:c0 + RET_DIM] = (gr * _sigmoid(gr) * o).astype(ob_ref.dtype)

    @pl.when(t == pl.num_programs(1) - 1)
    def _():
        ret_ref[...] = ret_state[...]


def _inproj_sample_kernel(n_w, x_ref, prev_ref, cc_ref, ss_ref, n1_ref, *rest):
    w_in = rest[:n_w]
    prm, rest = _take_prep_params(rest[n_w:], n_w)
    (pk_ref, v_ref, g_ref, bonus_ref, gates_ref, retq_ref, shift_ref) = rest
    u = _rmsnorm(x_ref[...], n1_ref[...])
    s = _mm(u, w_in, (0, SHIFT_WIDTH))
    shift_ref[...] = s
    _store_rwkv_operands(_rwkv_prep(s, prev_ref[...], prm), pk_ref, v_ref, g_ref, bonus_ref)
    gates_ref[...] = _mm(u, w_in, (GATE_COL0, IN_WIDTH))
    ret = _mm(u, w_in, (RET_COL0, GATE_COL0))
    cc = cc_ref[...]
    ss = ss_ref[...]
    for h in range(RET_HEADS):
        c0 = h * RET_DIM
        retq_ref[:, c0:c0 + RET_DIM] = _rope(ret[:, c0:c0 + RET_DIM], cc, ss)
        retq_ref[:, RET_WIDTH + c0:RET_WIDTH + c0 + RET_DIM] = (
            _rope(ret[:, RET_WIDTH + c0:RET_WIDTH + c0 + RET_DIM], cc, ss) * (RET_DIM ** -0.5))
    retq_ref[:, 2 * RET_WIDTH:] = ret[:, 2 * RET_WIDTH:]


def _const_spec(arr):
    nd = arr.ndim
    return pl.BlockSpec(arr.shape, lambda *_: (0,) * nd, pipeline_mode=pl.Buffered(1))


def _inproj_prompt(x, lead, cc, ss, rt_tab, gc_tab, n1, w_in, prm, nb):
    B, T, _ = x.shape
    tm = RET_CHUNK
    assert lead.shape[0] == tm and T % tm == 0 and B % nb == 0
    TP = tm + T
    n_w = len(w_in)
    consts = (lead, rt_tab, gc_tab, n1) + tuple(w_in) + tuple(prm)
    prow = lambda width: pl.BlockSpec((nb, tm, width), lambda b, t: (b, jnp.maximum(t - 1, 0), 0))
    in_specs = [prow(D_MODEL),
                pl.BlockSpec((tm, LANES), lambda b, t: (t, 0)),
                pl.BlockSpec((tm, LANES), lambda b, t: (t, 0))] + [_const_spec(c) for c in consts]
    out_shape = (
        jax.ShapeDtypeStruct((6, 2, B, RWKV_WIDTH // 2, TP), F32),
        jax.ShapeDtypeStruct((B, T, RWKV_WIDTH), F32),
        jax.ShapeDtypeStruct((B, T, RWKV_WIDTH), F32),
        jax.ShapeDtypeStruct((B, T, 2 * D_MODEL), F32),
        jax.ShapeDtypeStruct((B, T, RET_WIDTH), BF16),
        jax.ShapeDtypeStruct((B, 1, SHIFT_WIDTH), F32),
        jax.ShapeDtypeStruct((B, RET_HEADS, RET_DIM, RET_DIM), F32),
    )
    out_specs = (
        pl.BlockSpec((6, 2, nb, RWKV_WIDTH // 2, tm), lambda b, t: (0, 0, b, 0, t)),
        prow(RWKV_WIDTH), prow(RWKV_WIDTH), prow(2 * D_MODEL), prow(RET_WIDTH),
        pl.BlockSpec((nb, 1, SHIFT_WIDTH), lambda b, t: (b, 0, 0)),
        pl.BlockSpec((nb, RET_HEADS, RET_DIM, RET_DIM), lambda b, t: (b, 0, 0, 0)),
    )
    return pl.pallas_call(
        functools.partial(_inproj_prompt_kernel, n_w),
        grid=(B // nb, TP // tm),
        in_specs=in_specs,
        out_specs=out_specs,
        out_shape=out_shape,
        scratch_shapes=[pltpu.VMEM((nb, tm + SUBLANES, SHIFT_WIDTH), F32),
                        pltpu.VMEM((nb, RET_HEADS, RET_DIM, RET_DIM), F32)],
        compiler_params=pltpu.CompilerParams(
            dimension_semantics=("arbitrary", "arbitrary"), vmem_limit_bytes=VMEM_LIMIT),
        name="inproj_prompt",
    )(x, cc, ss, *consts)


def _inproj_sample(xs, prev, cc, ss, n1, w_in, prm):
    N = xs.shape[0]
    n_w = len(w_in)
    args = (xs, prev, cc, ss, n1) + tuple(w_in) + tuple(prm)
    out_shape = (
        jax.ShapeDtypeStruct((N, 5 * RWKV_WIDTH), F32),
        jax.ShapeDtypeStruct((N, RWKV_WIDTH), F32),
        jax.ShapeDtypeStruct((N, RWKV_WIDTH), F32),
        jax.ShapeDtypeStruct((N, RWKV_WIDTH), F32),
        jax.ShapeDtypeStruct((N, 2 * D_MODEL), F32),
        jax.ShapeDtypeStruct((N, 4 * RET_WIDTH), F32),
        jax.ShapeDtypeStruct((N, SHIFT_WIDTH), F32),
    )
    return pl.pallas_call(
        functools.partial(_inproj_sample_kernel, n_w),
        out_shape=out_shape,
        compiler_params=pltpu.CompilerParams(vmem_limit_bytes=VMEM_LIMIT),
        name="inproj_sample",
    )(*args)


def _rwkv_prompt_kernel(pk_ref, nxt_ref, y_ref, sout_ref, state, sa_buf, kd):
    i = pl.program_id(0)
    tb = pk_ref.shape[0]
    HD = RWKV_HEAD_DIM
    VL = K_LO

    @pl.when(i == 0)
    def _():
        state[...] = jnp.zeros(state.shape, F32)
        sa_buf[...] = jnp.zeros(sa_buf.shape, F32)

    first_half = lax.broadcasted_iota(jnp.int32, (K_LO, LANES), 1) < LANES // 2

    def spread(x, t, q):
        swapped = pltpu.roll(x, LANES // 2, 1)
        kd[t, q, 0:K_LO, :] = jnp.where(first_half, x, swapped)
        kd[t, q, K_LO:HD, :] = jnp.where(first_half, swapped, x)

    for t in range(tb):
        for q in (PK_W, PK_NKK, PK_BB, PK_K, PK_R):
            spread(pk_ref[t, q], t, q)
    spread(nxt_ref[0, 0], tb, PK_NKK)

    def step(t, sa):
        vv = pk_ref[t, PK_V]

        def key_rows(c, accs):
            yacc, sacc = accs
            for j in range(KL_UNROLL):
                k = c * KL_UNROLL + j
                row = lambda tt, q: kd[tt, q, pl.ds(k, 1), :]
                rows = pl.ds(pl.multiple_of(k * VL, VL), VL)
                new = state[rows, :] * row(t, PK_W) + sa * row(t, PK_BB) + vv * row(t, PK_K)
                state[rows, :] = new
                yacc = yacc + new * row(t, PK_R)
                sacc = sacc + new * row(t + 1, PK_NKK)
            return yacc, sacc

        zero = jnp.zeros((VL, LANES), F32)
        yacc, sacc = lax.fori_loop(0, HD // KL_UNROLL, key_rows, (zero, zero))
        y_ref[t] = yacc
        return sacc

    sa_buf[...] = lax.fori_loop(0, tb, step, sa_buf[...])

    @pl.when(i == pl.num_programs(0) - 1)
    def _():
        sout_ref[...] = state[...]


def _rwkv_prompt(pk_t, tb):
    TP = pk_t.shape[0]
    HD = RWKV_HEAD_DIM
    assert LEAD % tb == 0
    skip = (LEAD - N_META) // tb
    lead_blocks = LEAD // tb - skip
    return pl.pallas_call(
        _rwkv_prompt_kernel,
        grid=(TP // tb - skip,),
        in_specs=[pl.BlockSpec((tb, 6, K_LO, LANES), lambda i: (i + skip, 0, 0, 0)),
                  pl.BlockSpec((1, 1, K_LO, LANES),
                               lambda i: (jnp.minimum((i + skip + 1) * tb, TP - 1), PK_NKK, 0, 0))],
        out_specs=(pl.BlockSpec((tb, K_LO, LANES), lambda i: (jnp.maximum(i - lead_blocks, 0), 0, 0)),
                   pl.BlockSpec((K_LO * HD, LANES), lambda i: (0, 0))),
        out_shape=(jax.ShapeDtypeStruct((TP - LEAD, K_LO, LANES), F32),
                   jax.ShapeDtypeStruct((K_LO * HD, LANES), F32)),
        scratch_shapes=[pltpu.VMEM((HD * K_LO, LANES), F32), pltpu.VMEM((K_LO, LANES), F32),
                        pltpu.VMEM((tb + 1, 5, HD, LANES), F32)],
        compiler_params=pltpu.CompilerParams(dimension_semantics=("arbitrary",)),
        name="rwkv_prompt",
    )(pk_t, pk_t)


def _lane_col(tile, h):
    lane = lax.broadcasted_iota(jnp.int32, tile.shape, 1)
    return jnp.sum(jnp.where(lane == h, tile, 0.0), axis=-1, keepdims=True)


def _rwkv_sample_kernel(s_ref, pk_ref, v_ref, sout_ref, y_ref):
    def value_row(v, carry):
        S = s_ref[v]
        sa = jnp.sum(S * pk_ref[PK_NKK], axis=0, keepdims=True)
        new = S * pk_ref[PK_W] + sa * pk_ref[PK_BB] + v_ref[pl.ds(v, 1), :] * pk_ref[PK_K]
        sout_ref[v] = new
        y_ref[pl.ds(v, 1), :] = jnp.sum(new * pk_ref[PK_R], axis=0, keepdims=True)
        return carry

    lax.fori_loop(0, s_ref.shape[0], value_row, 0, unroll=2)


def _rwkv_sample(state_t, pk_t, v_t):
    H, HD, _, N = state_t.shape
    return pl.pallas_call(
        _rwkv_sample_kernel,
        grid=(H,),
        in_specs=[pl.BlockSpec((None, HD, HD, N), lambda h: (h, 0, 0, 0)),
                  pl.BlockSpec((5, None, HD, N), lambda h: (0, h, 0, 0)),
                  pl.BlockSpec((None, HD, N), lambda h: (h, 0, 0))],
        out_specs=(pl.BlockSpec((None, HD, HD, N), lambda h: (h, 0, 0, 0)),
                   pl.BlockSpec((None, HD, N), lambda h: (h, 0, 0))),
        out_shape=(jax.ShapeDtypeStruct(state_t.shape, F32),
                   jax.ShapeDtypeStruct((H, HD, N), F32)),
        compiler_params=pltpu.CompilerParams(dimension_semantics=("arbitrary",)),
        name="rwkv_sample",
    )(state_t, pk_t, v_t)


def _ret_sample_kernel(s_ref, qt_ref, kt_ref, v_ref, g_ref, gam_ref, sout_ref, ob_ref):
    gam = gam_ref[...][:, None, :]
    for b in range(s_ref.shape[0]):
        q_tile = qt_ref[b]
        k_tile = kt_ref[b]
        qc = jnp.stack([_lane_col(q_tile, h) for h in range(RET_HEADS)], axis=0)
        kc = jnp.stack([_lane_col(k_tile, h) for h in range(RET_HEADS)], axis=0)
        vrow = v_ref[b][:, None, :]
        S = s_ref[b]
        qk = jnp.sum(qc * kc, axis=1, keepdims=True)
        cross = jnp.sum(qc * S, axis=1, keepdims=True)
        o = qk * vrow + cross * gam
        sout_ref[b] = gam * S + kc * vrow
        o = o * lax.rsqrt(jnp.mean(o * o, axis=-1, keepdims=True) + RMS_EPS)
        gr = g_ref[b][:, None, :]
        ob_ref[b] = (gr * _sigmoid(gr) * o)[:, 0, :]


def _ret_sample(state, q_t, k_t, v_h, g_h, gam, bt):
    N = state.shape[0]
    H, DK = RET_HEADS, RET_DIM
    col = pl.BlockSpec((bt, DK, H), lambda i: (i, 0, 0))
    rowspec = pl.BlockSpec((bt, H, DK), lambda i: (i, 0, 0))
    st = pl.BlockSpec((bt, H, DK, DK), lambda i: (i, 0, 0, 0))
    return pl.pallas_call(
        _ret_sample_kernel,
        grid=(N // bt,),
        in_specs=[st, col, col, rowspec, rowspec, pl.BlockSpec((H, DK), lambda i: (0, 0))],
        out_specs=(st, rowspec),
        out_shape=(jax.ShapeDtypeStruct(state.shape, F32), jax.ShapeDtypeStruct((N, H, DK), F32)),
        compiler_params=pltpu.CompilerParams(dimension_semantics=("arbitrary",)),
        name="ret_sample",
    )(state, q_t, k_t, v_h, g_h, gam)


def _mixtail_kernel(n_w, y_feature_major, x_ref, y_ref, g_ref, bonus_ref, ob_ref, gates_ref,
                    lnw_ref, lnb_ref, n2_ref, br_ref, *rest):
    wa = rest[0:n_w]
    wb = rest[n_w:2 * n_w]
    wo = rest[2 * n_w:3 * n_w]
    wr = rest[3 * n_w:3 * n_w + 2]
    x1_ref, u2_ref, comb_ref = rest[3 * n_w + 2:]

    inv_n = 1.0 / RWKV_HEAD_DIM
    if y_feature_major:
        normed = []
        for h in range(RWKV_HEADS):
            yh = jnp.concatenate([y_ref[hi, h * K_LO:(h + 1) * K_LO, :] for hi in range(2)], axis=0)
            d = yh - jnp.sum(yh, axis=0, keepdims=True) * inv_n
            var = jnp.sum(d * d, axis=0, keepdims=True) * inv_n
            normed.append(d * lax.rsqrt(var + LNX_EPS))
        yhat = jnp.concatenate(normed, axis=0).T
    else:
        y = y_ref[...]
        d = y - _head64_sum(y) * inv_n
        var = _head64_sum(d * d) * inv_n
        yhat = d * lax.rsqrt(var + LNX_EPS)
    yn = yhat * lnw_ref[...] + lnb_ref[...]
    ya = (yn + bonus_ref[...]) * g_ref[...]
    branch_a = _mm(ya, wa)
    branch_b = _mm(ob_ref[...], wb)
    gates = gates_ref[...]
    merged = _sigmoid(gates[:, :D_MODEL]) * branch_a + _sigmoid(gates[:, D_MODEL:]) * branch_b
    x1 = x_ref[...] + _mm(merged, wo)
    x1_ref[...] = x1
    u2 = _rmsnorm(x1, n2_ref[...])
    u2_ref[...] = u2.astype(u2_ref.dtype)

    logits = (_mm(u2, wr) + br_ref[...]).T
    tm = logits.shape[1]
    neg = jnp.float32(-jnp.inf)
    big = jnp.float32(1e9)
    first = lambda hit, idx: jnp.min(jnp.where(hit, idx, big), axis=0, keepdims=True)
    lg = logits[N_EXPERTS:N_EXPERTS + N_GROUPS]
    gidx = lax.broadcasted_iota(jnp.int32, lg.shape, 0).astype(F32)
    lg_max = jnp.max(lg, axis=0, keepdims=True)
    p_grp = 1.0 / jnp.sum(jnp.exp(lg - lg_max), axis=0, keepdims=True)
    grp = first(lg == lg_max, gidx)
    eidx = lax.broadcasted_iota(jnp.int32, (N_EXPERTS, tm), 0).astype(F32)
    in_grp = (eidx >= grp * EXPERTS_PER_GROUP) & (eidx < (grp + 1.0) * EXPERTS_PER_GROUP)
    le = jnp.where(in_grp, logits[0:N_EXPERTS], neg)
    v1 = jnp.max(le, axis=0, keepdims=True)
    i1 = first(le == v1, eidx)
    le2 = jnp.where(eidx == i1, neg, le)
    v2 = jnp.max(le2, axis=0, keepdims=True)
    i2 = first(le2 == v2, eidx)
    e2 = jnp.exp(v2 - v1)
    w1 = 1.0 / (1.0 + e2)
    w2 = e2 / (1.0 + e2)
    comb = p_grp * (jnp.where(eidx == i1, w1, 0.0) + jnp.where(eidx == i2, w2, 0.0))
    comb_ref[...] = jnp.concatenate([comb, jnp.zeros((LANES - N_EXPERTS, tm), F32)], axis=0).T


def _mixtail(x, y, g, bonus, ob, gates, lnw, lnb, n2, br, wa, wb, wo, wr, tm, y_feature_major):
    G, n, _ = x.shape
    assert n % tm == 0
    n_w = len(wa)
    consts = (lnw, lnb, n2, br) + tuple(wa) + tuple(wb) + tuple(wo) + tuple(wr)
    row = lambda width: pl.BlockSpec((None, tm, width), lambda b, i: (b, i, 0))
    y_spec = (pl.BlockSpec((2, None, RWKV_WIDTH // 2, tm), lambda b, i: (0, b, 0, i)) if y_feature_major
              else row(RWKV_WIDTH))
    in_specs = [row(D_MODEL), y_spec, row(RWKV_WIDTH), row(RWKV_WIDTH), row(RET_WIDTH),
                row(2 * D_MODEL)] + [_const_spec(c) for c in consts]
    return pl.pallas_call(
        functools.partial(_mixtail_kernel, n_w, y_feature_major),
        grid=(G, n // tm),
        in_specs=in_specs,
        out_specs=(row(D_MODEL), row(D_MODEL), row(LANES)),
        out_shape=(jax.ShapeDtypeStruct((G, n, D_MODEL), F32),
                   jax.ShapeDtypeStruct((G, n, D_MODEL), BF16),
                   jax.ShapeDtypeStruct((G, n, LANES), F32)),
        compiler_params=pltpu.CompilerParams(
            dimension_semantics=("arbitrary", "arbitrary"), vmem_limit_bytes=VMEM_LIMIT),
        name="mixtail",
    )(x, y, g, bonus, ob, gates, *consts)


def _moe_kernel(x1_ref, u2_ref, comb_ref, wgu_ref, wd_ref, fg_ref, o_ref, acc):
    grp = pl.program_id(1)

    @pl.when(grp == 0)
    def _():
        acc[...] = jnp.zeros(acc.shape, F32)

    u2 = u2_ref[...]
    comb = comb_ref[...]
    lane = lax.broadcasted_iota(jnp.int32, comb.shape, 1)
    total = acc[...]
    for j in range(EXPERTS_PER_GROUP):
        c = jnp.sum(jnp.where(lane == grp * EXPERTS_PER_GROUP + j, comb, 0.0), axis=-1, keepdims=True)
        hgu = jnp.dot(u2, wgu_ref[j], preferred_element_type=F32)
        hg = hgu[:, :EXPERT_FF]
        h = hg * _sigmoid(hg) * hgu[:, EXPERT_FF:] * c
        total = total + jnp.dot(h.astype(BF16), wd_ref[j], preferred_element_type=F32)
    acc[...] = total

    @pl.when(grp == pl.num_programs(1) - 1)
    def _():
        o_ref[...] = _rmsnorm(x1_ref[...] + acc[...], fg_ref[...])


def _moe(x1, u2, comb, wgu, wd, fg, tm):
    N = x1.shape[0]
    E = EXPERTS_PER_GROUP
    return pl.pallas_call(
        _moe_kernel,
        grid=(N // tm, N_GROUPS),
        in_specs=[pl.BlockSpec((tm, D_MODEL), lambda i, g: (i, 0)),
                  pl.BlockSpec((tm, D_MODEL), lambda i, g: (i, 0)),
                  pl.BlockSpec((tm, LANES), lambda i, g: (i, 0)),
                  pl.BlockSpec((E, D_MODEL, 2 * EXPERT_FF), lambda i, g: (g, 0, 0)),
                  pl.BlockSpec((E, EXPERT_FF, D_MODEL), lambda i, g: (g, 0, 0)),
                  pl.BlockSpec((1, D_MODEL), lambda i, g: (0, 0))],
        out_specs=pl.BlockSpec((tm, D_MODEL), lambda i, g: (i, 0)),
        out_shape=jax.ShapeDtypeStruct((N, D_MODEL), F32),
        scratch_shapes=[pltpu.VMEM((tm, D_MODEL), F32)],
        compiler_params=pltpu.CompilerParams(
            dimension_semantics=("arbitrary", "arbitrary"), vmem_limit_bytes=VMEM_LIMIT),
        name="moe",
    )(x1, u2, comb, wgu, wd, fg)


def _pieces(w, n):
    return tuple(_split_bf16(w.astype(F32), n))


def _rope_tables(pos):
    half = RET_DIM // 2
    inv = ROPE_BASE ** (-jnp.arange(half, dtype=F32) / half)
    ang = pos.astype(F32)[:, None] * inv[None, :]
    c = jnp.cos(ang)
    s = jnp.sin(ang)
    cc = jnp.repeat(c, 2, axis=-1)
    ss = jnp.stack([-s, s], axis=-1).reshape(pos.shape[0], RET_DIM)
    return cc, ss


def _retention_tables():
    C = RET_CHUNK
    log_gamma = jnp.log(1.0 - 2.0 ** (-5.0 - jnp.arange(RET_HEADS, dtype=F32)))
    idx = jnp.arange(C, dtype=F32)
    diff = idx[:, None] - idx[None, :]
    lg = log_gamma[:, None, None]
    dmask = jnp.where(diff[None] >= 0, jnp.exp(jnp.maximum(diff, 0.0)[None] * lg), 0.0)
    q_dec = jnp.exp((idx[None, :] + 1.0) * log_gamma[:, None])
    k_dec = jnp.exp((C - 1.0 - idx[None, :]) * log_gamma[:, None])
    bc = lambda a: jnp.broadcast_to(a[:, :, None], (RET_HEADS, C, C))
    rt_tab = jnp.stack([dmask, bc(q_dec), bc(k_dec)], axis=1)
    gc_tab = jnp.broadcast_to(jnp.exp(C * log_gamma)[:, None, None], (RET_HEADS, SUBLANES, RET_DIM))
    gam1 = jnp.broadcast_to(jnp.exp(log_gamma)[:, None], (RET_HEADS, RET_DIM))
    return rt_tab, gc_tab, gam1


def kernel(x_prompt, x_sample, state_wkv, state_shift, state_ret, meta_tokens, norm1_g, w_in, shift_mu, decay_w0, decay_up, aaa_a0, aaa_up, gate_up, k_k, k_a, r_k, lnx_w, lnx_b, w_branch_a, w_branch_b, w_out, norm2_g, router_group_w, router_group_b, router_expert_w, router_expert_b, expert_w_gate, expert_w_up, expert_w_down, final_norm_g):
    B, T, D = x_prompt.shape
    NS = x_sample.shape[0]
    TP = LEAD + T
    H, HD = RWKV_HEADS, RWKV_HEAD_DIM
    l = 0
    row = lambda a: a.reshape(1, -1)

    zpad = jnp.zeros((LORA_PAD - 64, RWKV_WIDTH), F32)
    dup_pad = jnp.concatenate([decay_up[l], zpad], axis=0)
    aup_pad = jnp.concatenate([zpad, aaa_up[l]], axis=0)
    wr_full = jnp.concatenate([router_expert_w[l], router_group_w[l]], axis=1)
    wr_full = jnp.pad(wr_full, ((0, 0), (0, LANES - wr_full.shape[1])))
    br_full = jnp.pad(jnp.concatenate([router_expert_b[l], router_group_b[l]]), (0, LANES - N_EXPERTS - N_GROUPS))
    wr = _pieces(wr_full, 2)
    wgu = jnp.concatenate([expert_w_gate[l], expert_w_up[l]], axis=-1).astype(BF16)
    wd = expert_w_down[l].astype(BF16)
    rt_tab, gc_tab, gam1 = _retention_tables()

    split = {name: _pieces(w, 2) for name, w in (
        ("w_in", w_in[l]), ("dup", dup_pad), ("aup", aup_pad), ("gup", gate_up[l]),
        ("wa", w_branch_a[l]), ("wb", w_branch_b[l]), ("wo", w_out[l]))}

    def prep_params(n):
        return (row(shift_mu[l]), row(decay_w0[l]), split["dup"][:n], row(aaa_a0[l]), split["aup"][:n],
                split["gup"][:n], row(k_k[l]), row(k_a[l]), row(r_k[l]))

    def flat_params(p):
        out = []
        for a in p:
            out.extend(a if isinstance(a, tuple) else (a,))
        return tuple(out)

    def tail(n, x, y, g, bonus, ob, gates, tm_tail, tm_moe, y_feature_major):
        x1, u2, comb = _mixtail(x, y, g, bonus, ob, gates, row(lnx_w[l]), row(lnx_b[l]), row(norm2_g[l]),
                                row(br_full), split["wa"][:n], split["wb"][:n], split["wo"][:n], wr, tm_tail,
                                y_feature_major)
        flat = lambda a: a.reshape(-1, a.shape[-1])
        return _moe(flat(x1), flat(u2), flat(comb), wgu, wd, row(final_norm_g), tm_moe)

    lead = jnp.concatenate([jnp.zeros((LEAD - N_META, D), F32), meta_tokens.astype(F32)], axis=0)
    cc_p, ss_p = _rope_tables(jnp.arange(TP, dtype=jnp.int32) - (LEAD - N_META))
    pk, g, bonus, gates, ob, shift_p, ret_p = _inproj_prompt(
        x_prompt, lead, cc_p, ss_p, rt_tab, gc_tab, row(norm1_g[l]), split["w_in"][:1],
        flat_params(prep_params(1)), nb=4)

    pk_t = pk.reshape(6, 2 * B * H, K_LO, TP).transpose(3, 0, 2, 1)

    shift_p, xs = lax.optimization_barrier((shift_p, x_sample.reshape(NS, D)))
    cc_s, ss_s = _rope_tables(jnp.full((NS,), PAST_LEN, jnp.int32))
    pk_s, v_s, g_s, bonus_s, gates_s, retq, shift_s = _inproj_sample(
        xs, state_shift[l], cc_s, ss_s, row(norm1_g[l]), split["w_in"], flat_params(prep_params(2)))
    wkv_st, y_st = _rwkv_sample(state_wkv[l].transpose(1, 2, 3, 0),
                                pk_s.reshape(NS, 5, H, HD).transpose(1, 2, 3, 0),
                                v_s.reshape(NS, H, HD).transpose(1, 2, 0))
    wkv_s = wkv_st.transpose(3, 0, 1, 2)
    y_s = y_st.transpose(2, 0, 1).reshape(NS, RWKV_WIDTH)
    heads = lambda a: a.reshape(NS, RET_HEADS, RET_DIM)
    q_s, k_s, v_r, g_r = (retq[:, i * RET_WIDTH:(i + 1) * RET_WIDTH] for i in range(4))
    ret_s, ob_s = _ret_sample(state_ret[l], heads(q_s).transpose(0, 2, 1), heads(k_s).transpose(0, 2, 1),
                              heads(v_r), heads(g_r), gam1, bt=8)
    y_sample = tail(2, xs[None], y_s[None], g_s[None], bonus_s[None], ob_s.reshape(1, NS, RET_WIDTH), gates_s[None],
                    NS, NS, False)

    pk_t, y_sample, wkv_s, ret_s = lax.optimization_barrier((pk_t, y_sample, wkv_s, ret_s))
    y_t, s_fin = _rwkv_prompt(pk_t, tb=32)
    y_fm = y_t.transpose(2, 1, 0).reshape(2, B, RWKV_WIDTH // 2, T)
    wkv_p = s_fin.reshape(HD, K_LO, 2, B, H).transpose(3, 4, 2, 1, 0).reshape(B, H, HD, HD)
    y_prompt = tail(1, x_prompt, y_fm, g, bonus, ob, gates, 512, 1024, True).reshape(B, T, D)

    return (y_prompt, y_sample.reshape(NS, 1, D),
            wkv_p[None], shift_p.reshape(1, B, SHIFT_WIDTH), ret_p[None],
            wkv_s[None], shift_s[None], ret_s[None])
```

```python
import functools

import jax
import jax.numpy as jnp
from jax import lax
from jax.experimental import pallas as pl
from jax.experimental.pallas import tpu as pltpu

F32 = jnp.float32
BF16 = jnp.bfloat16

D_MODEL = 1024
N_META = 16
RWKV_WIDTH = 512
RWKV_HEADS = 8
RWKV_HEAD_DIM = 64
LORA_PAD = 128
GATE_LORA = 128
SHIFT_WIDTH = 3 * RWKV_WIDTH + 64 + 64 + GATE_LORA
LNX_EPS = 64e-5
RET_WIDTH = 512
RET_HEADS = 4
RET_DIM = 128
RET_CHUNK = 128
ROPE_BASE = 10000.0
RET_COL0 = SHIFT_WIDTH
GATE_COL0 = SHIFT_WIDTH + 4 * RET_WIDTH
IN_WIDTH = GATE_COL0 + 2 * D_MODEL
N_GROUPS = 4
EXPERTS_PER_GROUP = 4
N_EXPERTS = 16
EXPERT_FF = 256
RMS_EPS = 1e-6
PAST_LEN = 16384

LANES = 128
SUBLANES = 8
VMEM_LIMIT = 60 * 1024 * 1024

LEAD = 128
PK_W, PK_NKK, PK_BB, PK_K, PK_R, PK_V = range(6)
K_LO = 32
KL_UNROLL = 64


def _split_bf16(a, nterms):
    pieces = []
    rem = a
    for _ in range(nterms - 1):
        c = rem * 65537.0
        hi = c - (c - rem)
        pieces.append(hi.astype(BF16))
        rem = rem - hi
    pieces.append(rem.astype(BF16))
    return pieces


def _mm(a, w_refs, cols=None):
    def w(i):
        r = w_refs[i]
        return r[...] if cols is None else r[:, cols[0]:cols[1]]

    if a.dtype == BF16:
        acc = jnp.dot(a, w(0), preferred_element_type=F32)
        if len(w_refs) > 1:
            acc = acc + jnp.dot(a, w(1), preferred_element_type=F32)
        return acc
    if len(w_refs) == 1:
        return jnp.dot(a.astype(BF16), w(0), preferred_element_type=F32)
    a_hi, a_lo = _split_bf16(a, 2)
    acc = jnp.dot(a_hi, w(0), preferred_element_type=F32)
    acc = acc + jnp.dot(a_lo, w(0), preferred_element_type=F32)
    acc = acc + jnp.dot(a_hi, w(1), preferred_element_type=F32)
    return acc


def _rmsnorm(x, g):
    return x * lax.rsqrt(jnp.mean(x * x, axis=-1, keepdims=True) + RMS_EPS) * g


def _sigmoid(x):
    return 1.0 / (1.0 + jnp.exp(-x))


def _head64_sum(x):
    lane = lax.broadcasted_iota(jnp.int32, (x.shape[0], LANES), 1)
    lo = lane < RWKV_HEAD_DIM
    outs = []
    for j in range(x.shape[1] // LANES):
        xj = x[:, j * LANES:(j + 1) * LANES]
        s_lo = jnp.sum(jnp.where(lo, xj, 0.0), axis=-1, keepdims=True)
        s_hi = jnp.sum(jnp.where(lo, 0.0, xj), axis=-1, keepdims=True)
        outs.append(jnp.where(lo, s_lo, s_hi))
    return jnp.concatenate(outs, axis=-1)


def _rope(x, cc, ss):
    lane = lax.broadcasted_iota(jnp.int32, x.shape, 1)
    even = (lane & 1) == 0
    swapped = jnp.where(even, pltpu.roll(x, LANES - 1, 1), pltpu.roll(x, 1, 1))
    return x * cc + swapped * ss


def _rwkv_prep(s, prev, prm):
    (mu, w0, dup, a0, aup, gup, k_k, k_a, r_k) = prm
    xs = s + mu[...] * (prev - s)
    W = RWKV_WIDTH
    r = xs[:, 0:W]
    k = xs[:, W:2 * W]
    v = xs[:, 2 * W:3 * W]
    wa = xs[:, 3 * W:3 * W + LORA_PAD]
    gd = xs[:, 3 * W + LORA_PAD:]
    dec = _mm(jnp.tanh(wa), dup)
    z = -(w0[...] + dec)
    softplus = jnp.maximum(z, 0.0) + jnp.log1p(jnp.exp(-jnp.abs(z)))
    decay = jnp.exp(-jnp.exp(-softplus - 0.5))
    a = _sigmoid(a0[...] + _mm(wa, aup))
    g = _mm(_sigmoid(gd), gup)
    kk = k * k_k[...]
    kk = kk / jnp.maximum(jnp.sqrt(_head64_sum(kk * kk)), 1e-12)
    k_mod = k * (1.0 + (a - 1.0) * k_a[...])
    bonus = _head64_sum(r * k_mod * r_k[...]) * v
    return r, decay, k_mod, v, kk, a, g, bonus


def _take_prep_params(refs, n_w):
    it = iter(refs)
    take = lambda n: tuple(next(it) for _ in range(n))
    (mu, w0), dup, (a0,), aup, gup, (k_k, k_a, r_k) = take(2), take(n_w), take(1), take(n_w), take(n_w), take(3)
    return (mu, w0, dup, a0, aup, gup, k_k, k_a, r_k), tuple(it)


def _store_rwkv_operands(ops, pk_ref, v_ref, g_ref, bonus_ref):
    r, decay, k_mod, v, kk, a, g, bonus = ops
    W = RWKV_WIDTH
    pk_ref[:, PK_W * W:(PK_W + 1) * W] = decay
    pk_ref[:, PK_NKK * W:(PK_NKK + 1) * W] = -kk
    pk_ref[:, PK_BB * W:(PK_BB + 1) * W] = kk * a
    pk_ref[:, PK_K * W:(PK_K + 1) * W] = k_mod
    pk_ref[:, PK_R * W:(PK_R + 1) * W] = r
    v_ref[...] = v
    g_ref[...] = g
    bonus_ref[...] = bonus


def _inproj_prompt_kernel(n_w, x_ref, cc_ref, ss_ref, lead_ref, rt_ref, gc_ref, n1_ref, *rest):
    w_in = rest[:n_w]
    prm, rest = _take_prep_params(rest[n_w:], n_w)
    (pk_ref, g_ref, bonus_ref, gates_ref, ob_ref, shift_ref, ret_ref, prev_buf, ret_state) = rest
    t = pl.program_id(1)
    nb, tm, _ = x_ref.shape
    rows = [slice(b * tm, (b + 1) * tm) for b in range(nb)]

    @pl.when(t == 0)
    def _():
        prev_buf[:, 0:SUBLANES, :] = jnp.zeros((nb, SUBLANES, SHIFT_WIDTH), F32)
        ret_state[...] = jnp.zeros(ret_state.shape, F32)

    x = jnp.where(t == 0, lead_ref[...][None], x_ref[...])
    u = _rmsnorm(x.reshape(nb * tm, D_MODEL), n1_ref[...])
    u_b = u.astype(BF16) if n_w == 1 else u

    s = _mm(u_b, w_in, (0, SHIFT_WIDTH))
    ret = _mm(u_b, w_in, (RET_COL0, GATE_COL0))
    prevs = []
    for b in range(nb):
        s_b = s[rows[b]]
        prev_buf[b, SUBLANES:SUBLANES + tm, :] = s_b
        prevs.append(prev_buf[b, SUBLANES - 1:SUBLANES - 1 + tm, :])
        prev_buf[b, SUBLANES - 1:SUBLANES, :] = s_b[tm - 1:tm, :]
        shift_ref[b] = s_b[tm - 1:tm, :]
    prev = jnp.concatenate(prevs, axis=0)

    r, decay, k_mod, v, kk, a, g, bonus = _rwkv_prep(s, prev, prm)
    for b in range(nb):
        rb = rows[b]
        for q, val in ((PK_W, decay), (PK_NKK, -kk), (PK_BB, kk * a), (PK_K, k_mod), (PK_R, r), (PK_V, v)):
            fm = val[rb].T
            for hi in range(2):
                pk_ref[q, hi, b] = jnp.concatenate(
                    [fm[h * RWKV_HEAD_DIM + hi * K_LO:h * RWKV_HEAD_DIM + (hi + 1) * K_LO] for h in range(RWKV_HEADS)],
                    axis=0)
        g_ref[b] = g[rb]
        bonus_ref[b] = bonus[rb]

    gates = _mm(u_b, w_in, (GATE_COL0, IN_WIDTH))
    for b in range(nb):
        gates_ref[b] = gates[rows[b]]

    cc = cc_ref[...]
    ss = ss_ref[...]
    for b in range(nb):
        rb = rows[b]
        for h in range(RET_HEADS):
            c0 = h * RET_DIM
            qc = _rope(ret[rb, c0:c0 + RET_DIM], cc, ss).astype(BF16)
            kc = _rope(ret[rb, RET_WIDTH + c0:RET_WIDTH + c0 + RET_DIM], cc, ss) * (RET_DIM ** -0.5)
            vc = ret[rb, 2 * RET_WIDTH + c0:2 * RET_WIDTH + c0 + RET_DIM].astype(BF16)
            gr = ret[rb, 3 * RET_WIDTH + c0:3 * RET_WIDTH + c0 + RET_DIM]
            state = ret_state[b, h]
            scores = lax.dot_general(qc, kc.astype(BF16), (((1,), (1,)), ((), ())),
                                     preferred_element_type=F32) * rt_ref[h, 0]
            intra = jnp.dot(scores.astype(BF16), vc, preferred_element_type=F32)
            cross = jnp.dot(qc, state.astype(BF16), preferred_element_type=F32) * rt_ref[h, 1]
            kt = (kc * rt_ref[h, 2]).T.astype(BF16)
            ret_state[b, h] = gc_ref[h, 0:1, :] * state + jnp.dot(kt, vc, preferred_element_type=F32)
            o = intra + cross
            o = o * lax.rsqrt(jnp.mean(o * o, axis=-1, keepdims=True) + RMS_EPS)
            ob_ref[b, :, c0:c0 + RET_DIM] = (gr * _sigmoid(gr) * o).astype(ob_ref.dtype)

    @pl.when(t == pl.num_programs(1) - 1)
    def _():
        ret_ref[...] = ret_state[...]


def _inproj_sample_kernel(n_w, x_ref, prev_ref, cc_ref, ss_ref, n1_ref, *rest):
    w_in = rest[:n_w]
    prm, rest = _take_prep_params(rest[n_w:], n_w)
    (pk_ref, v_ref, g_ref, bonus_ref, gates_ref, retq_ref, shift_ref) = rest
    u = _rmsnorm(x_ref[...], n1_ref[...])
    s = _mm(u, w_in, (0, SHIFT_WIDTH))
    shift_ref[...] = s
    _store_rwkv_operands(_rwkv_prep(s, prev_ref[...], prm), pk_ref, v_ref, g_ref, bonus_ref)
    gates_ref[...] = _mm(u, w_in, (GATE_COL0, IN_WIDTH))
    ret = _mm(u, w_in, (RET_COL0, GATE_COL0))
    cc = cc_ref[...]
    ss = ss_ref[...]
    for h in range(RET_HEADS):
        c0 = h * RET_DIM
        retq_ref[:, c0:c0 + RET_DIM] = _rope(ret[:, c0:c0 + RET_DIM], cc, ss)
        retq_ref[:, RET_WIDTH + c0:RET_WIDTH + c0 + RET_DIM] = (
            _rope(ret[:, RET_WIDTH + c0:RET_WIDTH + c0 + RET_DIM], cc, ss) * (RET_DIM ** -0.5))
    retq_ref[:, 2 * RET_WIDTH:] = ret[:, 2 * RET_WIDTH:]


def _const_spec(arr):
    nd = arr.ndim
    return pl.BlockSpec(arr.shape, lambda *_: (0,) * nd, pipeline_mode=pl.Buffered(1))


def _inproj_prompt(x, lead, cc, ss, rt_tab, gc_tab, n1, w_in, prm, nb):
    B, T, _ = x.shape
    tm = RET_CHUNK
    assert lead.shape[0] == tm and T % tm == 0 and B % nb == 0
    TP = tm + T
    n_w = len(w_in)
    consts = (lead, rt_tab, gc_tab, n1) + tuple(w_in) + tuple(prm)
    prow = lambda width: pl.BlockSpec((nb, tm, width), lambda b, t: (b, jnp.maximum(t - 1, 0), 0))
    in_specs = [prow(D_MODEL),
                pl.BlockSpec((tm, LANES), lambda b, t: (t, 0)),
                pl.BlockSpec((tm, LANES), lambda b, t: (t, 0))] + [_const_spec(c) for c in consts]
    out_shape = (
        jax.ShapeDtypeStruct((6, 2, B, RWKV_WIDTH // 2, TP), F32),
        jax.ShapeDtypeStruct((B, T, RWKV_WIDTH), F32),
        jax.ShapeDtypeStruct((B, T, RWKV_WIDTH), F32),
        jax.ShapeDtypeStruct((B, T, 2 * D_MODEL), F32),
        jax.ShapeDtypeStruct((B, T, RET_WIDTH), BF16),
        jax.ShapeDtypeStruct((B, 1, SHIFT_WIDTH), F32),
        jax.ShapeDtypeStruct((B, RET_HEADS, RET_DIM, RET_DIM), F32),
    )
    out_specs = (
        pl.BlockSpec((6, 2, nb, RWKV_WIDTH // 2, tm), lambda b, t: (0, 0, b, 0, t)),
        prow(RWKV_WIDTH), prow(RWKV_WIDTH), prow(2 * D_MODEL), prow(RET_WIDTH),
        pl.BlockSpec((nb, 1, SHIFT_WIDTH), lambda b, t: (b, 0, 0)),
        pl.BlockSpec((nb, RET_HEADS, RET_DIM, RET_DIM), lambda b, t: (b, 0, 0, 0)),
    )
    return pl.pallas_call(
        functools.partial(_inproj_prompt_kernel, n_w),
        grid=(B // nb, TP // tm),
        in_specs=in_specs,
        out_specs=out_specs,
        out_shape=out_shape,
        scratch_shapes=[pltpu.VMEM((nb, tm + SUBLANES, SHIFT_WIDTH), F32),
                        pltpu.VMEM((nb, RET_HEADS, RET_DIM, RET_DIM), F32)],
        compiler_params=pltpu.CompilerParams(
            dimension_semantics=("arbitrary", "arbitrary"), vmem_limit_bytes=VMEM_LIMIT),
        name="inproj_prompt",
    )(x, cc, ss, *consts)


def _inproj_sample(xs, prev, cc, ss, n1, w_in, prm):
    N = xs.shape[0]
    n_w = len(w_in)
    args = (xs, prev, cc, ss, n1) + tuple(w_in) + tuple(prm)
    out_shape = (
        jax.ShapeDtypeStruct((N, 5 * RWKV_WIDTH), F32),
        jax.ShapeDtypeStruct((N, RWKV_WIDTH), F32),
        jax.ShapeDtypeStruct((N, RWKV_WIDTH), F32),
        jax.ShapeDtypeStruct((N, RWKV_WIDTH), F32),
        jax.ShapeDtypeStruct((N, 2 * D_MODEL), F32),
        jax.ShapeDtypeStruct((N, 4 * RET_WIDTH), F32),
        jax.ShapeDtypeStruct((N, SHIFT_WIDTH), F32),
    )
    return pl.pallas_call(
        functools.partial(_inproj_sample_kernel, n_w),
        out_shape=out_shape,
        compiler_params=pltpu.CompilerParams(vmem_limit_bytes=VMEM_LIMIT),
        name="inproj_sample",
    )(*args)


def _rwkv_prompt_kernel(pk_ref, nxt_ref, y_ref, sout_ref, state, sa_buf, kd):
    i = pl.program_id(0)
    tb = pk_ref.shape[0]
    HD = RWKV_HEAD_DIM
    VL = K_LO

    @pl.when(i == 0)
    def _():
        state[...] = jnp.zeros(state.shape, F32)
        sa_buf[...] = jnp.zeros(sa_buf.shape, F32)

    first_half = lax.broadcasted_iota(jnp.int32, (K_LO, LANES), 1) < LANES // 2

    def spread(x, t, q):
        swapped = pltpu.roll(x, LANES // 2, 1)
        kd[t, q, 0:K_LO, :] = jnp.where(first_half, x, swapped)
        kd[t, q, K_LO:HD, :] = jnp.where(first_half, swapped, x)

    for t in range(tb):
        for q in (PK_W, PK_NKK, PK_BB, PK_K, PK_R):
            spread(pk_ref[t, q], t, q)
    spread(nxt_ref[0, 0], tb, PK_NKK)

    def step(t, sa):
        vv = pk_ref[t, PK_V]

        def key_rows(c, accs):
            yacc, sacc = accs
            for j in range(KL_UNROLL):
                k = c * KL_UNROLL + j
                row = lambda tt, q: kd[tt, q, pl.ds(k, 1), :]
                rows = pl.ds(pl.multiple_of(k * VL, VL), VL)
                new = state[rows, :] * row(t, PK_W) + sa * row(t, PK_BB) + vv * row(t, PK_K)
                state[rows, :] = new
                yacc = yacc + new * row(t, PK_R)
                sacc = sacc + new * row(t + 1, PK_NKK)
            return yacc, sacc

        zero = jnp.zeros((VL, LANES), F32)
        yacc, sacc = lax.fori_loop(0, HD // KL_UNROLL, key_rows, (zero, zero))
        y_ref[t] = yacc
        return sacc

    sa_buf[...] = lax.fori_loop(0, tb, step, sa_buf[...])

    @pl.when(i == pl.num_programs(0) - 1)
    def _():
        sout_ref[...] = state[...]


def _rwkv_prompt(pk_t, tb):
    TP = pk_t.shape[0]
    HD = RWKV_HEAD_DIM
    assert LEAD % tb == 0
    skip = (LEAD - N_META) // tb
    lead_blocks = LEAD // tb - skip
    return pl.pallas_call(
        _rwkv_prompt_kernel,
        grid=(TP // tb - skip,),
        in_specs=[pl.BlockSpec((tb, 6, K_LO, LANES), lambda i: (i + skip, 0, 0, 0)),
                  pl.BlockSpec((1, 1, K_LO, LANES),
                               lambda i: (jnp.minimum((i + skip + 1) * tb, TP - 1), PK_NKK, 0, 0))],
        out_specs=(pl.BlockSpec((tb, K_LO, LANES), lambda i: (jnp.maximum(i - lead_blocks, 0), 0, 0)),
                   pl.BlockSpec((K_LO * HD, LANES), lambda i: (0, 0))),
        out_shape=(jax.ShapeDtypeStruct((TP - LEAD, K_LO, LANES), F32),
                   jax.ShapeDtypeStruct((K_LO * HD, LANES), F32)),
        scratch_shapes=[pltpu.VMEM((HD * K_LO, LANES), F32), pltpu.VMEM((K_LO, LANES), F32),
                        pltpu.VMEM((tb + 1, 5, HD, LANES), F32)],
        compiler_params=pltpu.CompilerParams(dimension_semantics=("arbitrary",)),
        name="rwkv_prompt",
    )(pk_t, pk_t)


def _lane_col(tile, h):
    lane = lax.broadcasted_iota(jnp.int32, tile.shape, 1)
    return jnp.sum(jnp.where(lane == h, tile, 0.0), axis=-1, keepdims=True)


def _rwkv_sample_kernel(s_ref, pk_ref, v_ref, sout_ref, y_ref):
    def value_row(v, carry):
        S = s_ref[v]
        sa = jnp.sum(S * pk_ref[PK_NKK], axis=0, keepdims=True)
        new = S * pk_ref[PK_W] + sa * pk_ref[PK_BB] + v_ref[pl.ds(v, 1), :] * pk_ref[PK_K]
        sout_ref[v] = new
        y_ref[pl.ds(v, 1), :] = jnp.sum(new * pk_ref[PK_R], axis=0, keepdims=True)
        return carry

    lax.fori_loop(0, s_ref.shape[0], value_row, 0, unroll=2)


def _rwkv_sample(state_t, pk_t, v_t):
    H, HD, _, N = state_t.shape
    return pl.pallas_call(
        _rwkv_sample_kernel,
        grid=(H,),
        in_specs=[pl.BlockSpec((None, HD, HD, N), lambda h: (h, 0, 0, 0)),
                  pl.BlockSpec((5, None, HD, N), lambda h: (0, h, 0, 0)),
                  pl.BlockSpec((None, HD, N), lambda h: (h, 0, 0))],
        out_specs=(pl.BlockSpec((None, HD, HD, N), lambda h: (h, 0, 0, 0)),
                   pl.BlockSpec((None, HD, N), lambda h: (h, 0, 0))),
        out_shape=(jax.ShapeDtypeStruct(state_t.shape, F32),
                   jax.ShapeDtypeStruct((H, HD, N), F32)),
        compiler_params=pltpu.CompilerParams(dimension_semantics=("arbitrary",)),
        name="rwkv_sample",
    )(state_t, pk_t, v_t)


def _ret_sample_kernel(s_ref, qt_ref, kt_ref, v_ref, g_ref, gam_ref, sout_ref, ob_ref):
    gam = gam_ref[...][:, None, :]
    for b in range(s_ref.shape[0]):
        q_tile = qt_ref[b]
        k_tile = kt_ref[b]
        qc = jnp.stack([_lane_col(q_tile, h) for h in range(RET_HEADS)], axis=0)
        kc = jnp.stack([_lane_col(k_tile, h) for h in range(RET_HEADS)], axis=0)
        vrow = v_ref[b][:, None, :]
        S = s_ref[b]
        qk = jnp.sum(qc * kc, axis=1, keepdims=True)
        cross = jnp.sum(qc * S, axis=1, keepdims=True)
        o = qk * vrow + cross * gam
        sout_ref[b] = gam * S + kc * vrow
        o = o * lax.rsqrt(jnp.mean(o * o, axis=-1, keepdims=True) + RMS_EPS)
        gr = g_ref[b][:, None, :]
        ob_ref[b] = (gr * _sigmoid(gr) * o)[:, 0, :]


def _ret_sample(state, q_t, k_t, v_h, g_h, gam, bt):
    N = state.shape[0]
    H, DK = RET_HEADS, RET_DIM
    col = pl.BlockSpec((bt, DK, H), lambda i: (i, 0, 0))
    rowspec = pl.BlockSpec((bt, H, DK), lambda i: (i, 0, 0))
    st = pl.BlockSpec((bt, H, DK, DK), lambda i: (i, 0, 0, 0))
    return pl.pallas_call(
        _ret_sample_kernel,
        grid=(N // bt,),
        in_specs=[st, col, col, rowspec, rowspec, pl.BlockSpec((H, DK), lambda i: (0, 0))],
        out_specs=(st, rowspec),
        out_shape=(jax.ShapeDtypeStruct(state.shape, F32), jax.ShapeDtypeStruct((N, H, DK), F32)),
        compiler_params=pltpu.CompilerParams(dimension_semantics=("arbitrary",)),
        name="ret_sample",
    )(state, q_t, k_t, v_h, g_h, gam)


def _mixtail_kernel(n_w, y_feature_major, x_ref, y_ref, g_ref, bonus_ref, ob_ref, gates_ref,
                    lnw_ref, lnb_ref, n2_ref, br_ref, *rest):
    wa = rest[0:n_w]
    wb = rest[n_w:2 * n_w]
    wo = rest[2 * n_w:3 * n_w]
    wr = rest[3 * n_w:3 * n_w + 2]
    x1_ref, u2_ref, comb_ref = rest[3 * n_w + 2:]

    inv_n = 1.0 / RWKV_HEAD_DIM
    if y_feature_major:
        normed = []
        for h in range(RWKV_HEADS):
            yh = jnp.concatenate([y_ref[hi, h * K_LO:(h + 1) * K_LO, :] for hi in range(2)], axis=0)
            d = yh - jnp.sum(yh, axis=0, keepdims=True) * inv_n
            var = jnp.sum(d * d, axis=0, keepdims=True) * inv_n
            normed.append(d * lax.rsqrt(var + LNX_EPS))
        yhat = jnp.concatenate(normed, axis=0).T
    else:
        y = y_ref[...]
        d = y - _head64_sum(y) * inv_n
        var = _head64_sum(d * d) * inv_n
        yhat = d * lax.rsqrt(var + LNX_EPS)
    yn = yhat * lnw_ref[...] + lnb_ref[...]
    ya = (yn + bonus_ref[...]) * g_ref[...]
    branch_a = _mm(ya, wa)
    branch_b = _mm(ob_ref[...], wb)
    gates = gates_ref[...]
    merged = _sigmoid(gates[:, :D_MODEL]) * branch_a + _sigmoid(gates[:, D_MODEL:]) * branch_b
    x1 = x_ref[...] + _mm(merged, wo)
    x1_ref[...] = x1
    u2 = _rmsnorm(x1, n2_ref[...])
    u2_ref[...] = u2.astype(u2_ref.dtype)

    logits = (_mm(u2, wr) + br_ref[...]).T
    tm = logits.shape[1]
    neg = jnp.float32(-jnp.inf)
    big = jnp.float32(1e9)
    first = lambda hit, idx: jnp.min(jnp.where(hit, idx, big), axis=0, keepdims=True)
    lg = logits[N_EXPERTS:N_EXPERTS + N_GROUPS]
    gidx = lax.broadcasted_iota(jnp.int32, lg.shape, 0).astype(F32)
    lg_max = jnp.max(lg, axis=0, keepdims=True)
    p_grp = 1.0 / jnp.sum(jnp.exp(lg - lg_max), axis=0, keepdims=True)
    grp = first(lg == lg_max, gidx)
    eidx = lax.broadcasted_iota(jnp.int32, (N_EXPERTS, tm), 0).astype(F32)
    in_grp = (eidx >= grp * EXPERTS_PER_GROUP) & (eidx < (grp + 1.0) * EXPERTS_PER_GROUP)
    le = jnp.where(in_grp, logits[0:N_EXPERTS], neg)
    v1 = jnp.max(le, axis=0, keepdims=True)
    i1 = first(le == v1, eidx)
    le2 = jnp.where(eidx == i1, neg, le)
    v2 = jnp.max(le2, axis=0, keepdims=True)
    i2 = first(le2 == v2, eidx)
    e2 = jnp.exp(v2 - v1)
    w1 = 1.0 / (1.0 + e2)
    w2 = e2 / (1.0 + e2)
    comb = p_grp * (jnp.where(eidx == i1, w1, 0.0) + jnp.where(eidx == i2, w2, 0.0))
    comb_ref[...] = jnp.concatenate([comb, jnp.zeros((LANES - N_EXPERTS, tm), F32)], axis=0).T


def _mixtail(x, y, g, bonus, ob, gates, lnw, lnb, n2, br, wa, wb, wo, wr, tm, y_feature_major):
    G, n, _ = x.shape
    assert n % tm == 0
    n_w = len(wa)
    consts = (lnw, lnb, n2, br) + tuple(wa) + tuple(wb) + tuple(wo) + tuple(wr)
    row = lambda width: pl.BlockSpec((None, tm, width), lambda b, i: (b, i, 0))
    y_spec = (pl.BlockSpec((2, None, RWKV_WIDTH // 2, tm), lambda b, i: (0, b, 0, i)) if y_feature_major
              else row(RWKV_WIDTH))
    in_specs = [row(D_MODEL), y_spec, row(RWKV_WIDTH), row(RWKV_WIDTH), row(RET_WIDTH),
                row(2 * D_MODEL)] + [_const_spec(c) for c in consts]
    return pl.pallas_call(
        functools.partial(_mixtail_kernel, n_w, y_feature_major),
        grid=(G, n // tm),
        in_specs=in_specs,
        out_specs=(row(D_MODEL), row(D_MODEL), row(LANES)),
        out_shape=(jax.ShapeDtypeStruct((G, n, D_MODEL), F32),
                   jax.ShapeDtypeStruct((G, n, D_MODEL), BF16),
                   jax.ShapeDtypeStruct((G, n, LANES), F32)),
        compiler_params=pltpu.CompilerParams(
            dimension_semantics=("arbitrary", "arbitrary"), vmem_limit_bytes=VMEM_LIMIT),
        name="mixtail",
    )(x, y, g, bonus, ob, gates, *consts)


def _moe_kernel(x1_ref, u2_ref, comb_ref, wg_ref, wu_ref, wd_ref, fg_ref, o_ref, acc):
    grp = pl.program_id(1)

    @pl.when(grp == 0)
    def _():
        acc[...] = jnp.zeros(acc.shape, F32)

    u2 = u2_ref[...]
    comb = comb_ref[...]
    lane = lax.broadcasted_iota(jnp.int32, comb.shape, 1)
    total = acc[...]
    for j in range(EXPERTS_PER_GROUP):
        c = jnp.sum(jnp.where(lane == grp * EXPERTS_PER_GROUP + j, comb, 0.0), axis=-1, keepdims=True)
        hg = jnp.dot(u2, wg_ref[j].astype(BF16), preferred_element_type=F32)
        hu = jnp.dot(u2, wu_ref[j].astype(BF16), preferred_element_type=F32)
        h = hg * _sigmoid(hg) * hu * c
        total = total + jnp.dot(h.astype(BF16), wd_ref[j].astype(BF16), preferred_element_type=F32)
    acc[...] = total

    @pl.when(grp == pl.num_programs(1) - 1)
    def _():
        o_ref[...] = _rmsnorm(x1_ref[...] + acc[...], fg_ref[...])


def _moe(x1, u2, comb, wg, wu, wd, fg, tm):
    N = x1.shape[0]
    E = EXPERTS_PER_GROUP
    return pl.pallas_call(
        _moe_kernel,
        grid=(N // tm, N_GROUPS),
        in_specs=[pl.BlockSpec((tm, D_MODEL), lambda i, g: (i, 0)),
                  pl.BlockSpec((tm, D_MODEL), lambda i, g: (i, 0)),
                  pl.BlockSpec((tm, LANES), lambda i, g: (i, 0)),
                  pl.BlockSpec((E, D_MODEL, EXPERT_FF), lambda i, g: (g, 0, 0)),
                  pl.BlockSpec((E, D_MODEL, EXPERT_FF), lambda i, g: (g, 0, 0)),
                  pl.BlockSpec((E, EXPERT_FF, D_MODEL), lambda i, g: (g, 0, 0)),
                  pl.BlockSpec((1, D_MODEL), lambda i, g: (0, 0))],
        out_specs=pl.BlockSpec((tm, D_MODEL), lambda i, g: (i, 0)),
        out_shape=jax.ShapeDtypeStruct((N, D_MODEL), F32),
        scratch_shapes=[pltpu.VMEM((tm, D_MODEL), F32)],
        compiler_params=pltpu.CompilerParams(
            dimension_semantics=("arbitrary", "arbitrary"), vmem_limit_bytes=VMEM_LIMIT),
        name="moe",
    )(x1, u2, comb, wg, wu, wd, fg)


def _pieces(w, n):
    return tuple(_split_bf16(w.astype(F32), n))


def _rope_tables(pos):
    half = RET_DIM // 2
    inv = ROPE_BASE ** (-jnp.arange(half, dtype=F32) / half)
    ang = pos.astype(F32)[:, None] * inv[None, :]
    c = jnp.cos(ang)
    s = jnp.sin(ang)
    cc = jnp.repeat(c, 2, axis=-1)
    ss = jnp.stack([-s, s], axis=-1).reshape(pos.shape[0], RET_DIM)
    return cc, ss


def _retention_tables():
    C = RET_CHUNK
    log_gamma = jnp.log(1.0 - 2.0 ** (-5.0 - jnp.arange(RET_HEADS, dtype=F32)))
    idx = jnp.arange(C, dtype=F32)
    diff = idx[:, None] - idx[None, :]
    lg = log_gamma[:, None, None]
    dmask = jnp.where(diff[None] >= 0, jnp.exp(jnp.maximum(diff, 0.0)[None] * lg), 0.0)
    q_dec = jnp.exp((idx[None, :] + 1.0) * log_gamma[:, None])
    k_dec = jnp.exp((C - 1.0 - idx[None, :]) * log_gamma[:, None])
    bc = lambda a: jnp.broadcast_to(a[:, :, None], (RET_HEADS, C, C))
    rt_tab = jnp.stack([dmask, bc(q_dec), bc(k_dec)], axis=1)
    gc_tab = jnp.broadcast_to(jnp.exp(C * log_gamma)[:, None, None], (RET_HEADS, SUBLANES, RET_DIM))
    gam1 = jnp.broadcast_to(jnp.exp(log_gamma)[:, None], (RET_HEADS, RET_DIM))
    return rt_tab, gc_tab, gam1


def kernel(x_prompt, x_sample, state_wkv, state_shift, state_ret, meta_tokens, norm1_g, w_in, shift_mu, decay_w0, decay_up, aaa_a0, aaa_up, gate_up, k_k, k_a, r_k, lnx_w, lnx_b, w_branch_a, w_branch_b, w_out, norm2_g, router_group_w, router_group_b, router_expert_w, router_expert_b, expert_w_gate, expert_w_up, expert_w_down, final_norm_g):
    B, T, D = x_prompt.shape
    NS = x_sample.shape[0]
    TP = LEAD + T
    H, HD = RWKV_HEADS, RWKV_HEAD_DIM
    l = 0
    row = lambda a: a.reshape(1, -1)

    zpad = jnp.zeros((LORA_PAD - 64, RWKV_WIDTH), F32)
    dup_pad = jnp.concatenate([decay_up[l], zpad], axis=0)
    aup_pad = jnp.concatenate([zpad, aaa_up[l]], axis=0)
    wr_full = jnp.concatenate([router_expert_w[l], router_group_w[l]], axis=1)
    wr_full = jnp.pad(wr_full, ((0, 0), (0, LANES - wr_full.shape[1])))
    br_full = jnp.pad(jnp.concatenate([router_expert_b[l], router_group_b[l]]), (0, LANES - N_EXPERTS - N_GROUPS))
    wr = _pieces(wr_full, 2)
    wg, wu, wd = expert_w_gate[l], expert_w_up[l], expert_w_down[l]
    rt_tab, gc_tab, gam1 = _retention_tables()

    split = {name: _pieces(w, 2) for name, w in (
        ("w_in", w_in[l]), ("dup", dup_pad), ("aup", aup_pad), ("gup", gate_up[l]),
        ("wa", w_branch_a[l]), ("wb", w_branch_b[l]), ("wo", w_out[l]))}

    def prep_params(n):
        return (row(shift_mu[l]), row(decay_w0[l]), split["dup"][:n], row(aaa_a0[l]), split["aup"][:n],
                split["gup"][:n], row(k_k[l]), row(k_a[l]), row(r_k[l]))

    def flat_params(p):
        out = []
        for a in p:
            out.extend(a if isinstance(a, tuple) else (a,))
        return tuple(out)

    def tail(n, x, y, g, bonus, ob, gates, tm_tail, tm_moe, y_feature_major):
        x1, u2, comb = _mixtail(x, y, g, bonus, ob, gates, row(lnx_w[l]), row(lnx_b[l]), row(norm2_g[l]),
                                row(br_full), split["wa"][:n], split["wb"][:n], split["wo"][:n], wr, tm_tail,
                                y_feature_major)
        flat = lambda a: a.reshape(-1, a.shape[-1])
        return _moe(flat(x1), flat(u2), flat(comb), wg, wu, wd, row(final_norm_g), tm_moe)

    lead = jnp.concatenate([jnp.zeros((LEAD - N_META, D), F32), meta_tokens.astype(F32)], axis=0)
    cc_p, ss_p = _rope_tables(jnp.arange(TP, dtype=jnp.int32) - (LEAD - N_META))
    pk, g, bonus, gates, ob, shift_p, ret_p = _inproj_prompt(
        x_prompt, lead, cc_p, ss_p, rt_tab, gc_tab, row(norm1_g[l]), split["w_in"][:1],
        flat_params(prep_params(1)), nb=4)

    pk_t = pk.reshape(6, 2 * B * H, K_LO, TP).transpose(3, 0, 2, 1)

    shift_p, xs = lax.optimization_barrier((shift_p, x_sample.reshape(NS, D)))
    cc_s, ss_s = _rope_tables(jnp.full((NS,), PAST_LEN, jnp.int32))
    pk_s, v_s, g_s, bonus_s, gates_s, retq, shift_s = _inproj_sample(
        xs, state_shift[l], cc_s, ss_s, row(norm1_g[l]), split["w_in"], flat_params(prep_params(2)))
    wkv_st, y_st = _rwkv_sample(state_wkv[l].transpose(1, 2, 3, 0),
                                pk_s.reshape(NS, 5, H, HD).transpose(1, 2, 3, 0),
                                v_s.reshape(NS, H, HD).transpose(1, 2, 0))
    wkv_s = wkv_st.transpose(3, 0, 1, 2)
    y_s = y_st.transpose(2, 0, 1).reshape(NS, RWKV_WIDTH)
    heads = lambda a: a.reshape(NS, RET_HEADS, RET_DIM)
    q_s, k_s, v_r, g_r = (retq[:, i * RET_WIDTH:(i + 1) * RET_WIDTH] for i in range(4))
    ret_s, ob_s = _ret_sample(state_ret[l], heads(q_s).transpose(0, 2, 1), heads(k_s).transpose(0, 2, 1),
                              heads(v_r), heads(g_r), gam1, bt=8)
    y_sample = tail(2, xs[None], y_s[None], g_s[None], bonus_s[None], ob_s.reshape(1, NS, RET_WIDTH), gates_s[None],
                    NS, NS, False)

    pk_t, y_sample, wkv_s, ret_s = lax.optimization_barrier((pk_t, y_sample, wkv_s, ret_s))
    y_t, s_fin = _rwkv_prompt(pk_t, tb=32)
    y_fm = y_t.transpose(2, 1, 0).reshape(2, B, RWKV_WIDTH // 2, T)
    wkv_p = s_fin.reshape(HD, K_LO, 2, B, H).transpose(3, 4, 2, 1, 0).reshape(B, H, HD, HD)
    y_prompt = tail(1, x_prompt, y_fm, g, bonus, ob, gates, 512, 1024, True).reshape(B, T, D)

    return (y_prompt, y_sample.reshape(NS, 1, D),
            wkv_p[None], shift_p.reshape(1, B, SHIFT_WIDTH), ret_p[None],
            wkv_s[None], shift_s[None], ret_s[None])
```
